```python
import math
import jax, jax.numpy as jnp
from jax import lax
import numpy as np

D_MODEL = 1024
BATCH = 2
SEQ = 16384
DEPTH = 2
DEC_BATCH = 16
DEC_SEQ = 2048
PAST_LEN = 128

ATTN_HEADS = 8
ATTN_HEAD_DIM = 64
ATTN_V_DIM = 2 * ATTN_HEAD_DIM
ATTN_WIDTH = ATTN_HEADS * ATTN_V_DIM
QK_COLS = ATTN_HEADS * 2 * ATTN_HEAD_DIM
Q_BLOCK = 128
SSM_EXPAND = 2
D_INNER = SSM_EXPAND * D_MODEL
SSM_HEAD_DIM = 64
SSM_HEADS = D_INNER // SSM_HEAD_DIM
SSM_GROUPS = 4
D_STATE = 128
D_CONV = 5
CHUNK = 128
XBC_DIM = D_INNER + 2 * SSM_GROUPS * D_STATE
N_BRANCHES = 2
O_Q = QK_COLS
O_K = O_Q + QK_COLS
O_V = O_K + ATTN_WIDTH
O_Z = O_V + D_INNER
O_XBC = O_Z + XBC_DIM
O_DT = O_XBC + 2 * SSM_HEADS
IN_COLS = O_DT + N_BRANCHES * D_MODEL
MEM_TOKENS = 256
X_HEADS = 4
X_HEAD_DIM = 128
X_WIDTH = X_HEADS * X_HEAD_DIM
N_EXPERTS = 16
CAPACITY_FACTOR = 2
D_FF_EXPERT = 2816
EPS = 1e-6
SUBLN_EPS = 1e-5

kernel_name = 'hybrid_diffattn_ssd_ec_encoder'


def rms_norm(x, w, eps=EPS):
    xf = x.astype(jnp.float32)
    y = xf * lax.rsqrt(jnp.mean(xf * xf, axis=-1, keepdims=True) + eps)
    return (y * w.astype(jnp.float32)).astype(x.dtype)


def alibi_slopes(n_heads):
    return jnp.exp2(-8.0 * (jnp.arange(n_heads, dtype=jnp.float32) + 1.0) / n_heads)


def diff_attention(q, k, v, lam, subln_w, lambda_init):
    b, s = q.shape[0], q.shape[1]
    nb = s // Q_BLOCK
    scale = ATTN_HEAD_DIM ** -0.5
    slopes = alibi_slopes(ATTN_HEADS)
    key_pos = jnp.arange(s)
    qb = jnp.moveaxis(q.reshape(b, nb, Q_BLOCK, ATTN_HEADS, 2, ATTN_HEAD_DIM), 1, 0)

    def block(args):
        qi, bi = args
        qpos = bi * Q_BLOCK + jnp.arange(Q_BLOCK)
        dist = jnp.abs(qpos[:, None] - key_pos[None, :]).astype(jnp.float32)
        bias = -slopes[:, None, None] * dist[None]
        sc = jnp.einsum('bqhcd,bkhcd->bhcqk', qi, k).astype(jnp.float32) * scale + bias[None, :, None]
        p = jax.nn.softmax(sc, axis=-1)
        a = p[:, :, 0] - lam * p[:, :, 1]
        return jnp.einsum('bhqk,bkhe->bqhe', a.astype(v.dtype), v)

    out = lax.map(block, (qb, jnp.arange(nb)))
    out = jnp.moveaxis(out, 0, 1).reshape(b, s, ATTN_HEADS, ATTN_V_DIM)
    out = rms_norm(out, subln_w, SUBLN_EPS) * (1.0 - lambda_init)
    return out.reshape(b, s, ATTN_WIDTH)


def depthwise_conv(x, w, bias):
    c = x.shape[-1]
    y = lax.conv_general_dilated(x, w.astype(x.dtype)[:, None, :], window_strides=(1,),
                                 padding=[(D_CONV // 2, D_CONV // 2)],
                                 dimension_numbers=('NWC', 'WIO', 'NWC'), feature_group_count=c)
    return y + bias.astype(x.dtype)


def ssd_scan(x, dt, a_head, bm, cm):
    b, s = x.shape[0], x.shape[1]
    g, n = bm.shape[2], bm.shape[3]
    hpg = SSM_HEADS // g
    nc = s // CHUNK

    def chunks(t):
        return jnp.moveaxis(t.reshape((b, nc, CHUNK) + t.shape[2:]), 1, 0)

    xdt = (x * dt[..., None]).reshape(b, s, g, hpg, SSM_HEAD_DIM)
    a = (dt * a_head).reshape(b, s, g, hpg)
    xs, acs, bcs, ccs = chunks(xdt), chunks(a), chunks(bm), chunks(cm)
    mask = jnp.tril(jnp.ones((CHUNK, CHUNK), dtype=bool))[None, :, :, None, None]

    def step(state, inp):
        xc, ac, bc, cc = inp
        cum = jnp.cumsum(ac, axis=1)
        seg = cum[:, :, None] - cum[:, None, :]
        decay = jnp.exp(jnp.where(mask, seg, -jnp.inf))
        cb = jnp.einsum('btgn,bsgn->btsg', cc, bc)
        y = jnp.einsum('btsg,btsgh,bsghp->btghp', cb, decay, xc)
        y = y + jnp.einsum('btgn,bghpn->btghp', cc, state) * jnp.exp(cum)[..., None]
        to_end = jnp.exp(cum[:, -1:] - cum)
        state = state * jnp.exp(cum[:, -1])[..., None, None] + jnp.einsum('bsgn,bsgh,bsghp->bghpn', bc, to_end, xc)
        return state, y

    init = jnp.zeros((b, g, hpg, SSM_HEAD_DIM, n), jnp.float32)
    _, ys = lax.scan(step, init, (xs, acs, bcs, ccs))
    return jnp.moveaxis(ys, 0, 1).reshape(b, s, SSM_HEADS, SSM_HEAD_DIM)


def gated_group_rms_norm(y, z, w):
    yf = y.astype(jnp.float32) * jax.nn.silu(z.astype(jnp.float32))
    shp = yf.shape
    yg = yf.reshape(shp[:-1] + (SSM_GROUPS, shp[-1] // SSM_GROUPS))
    yg = yg * lax.rsqrt(jnp.mean(yg * yg, axis=-1, keepdims=True) + EPS)
    return (yg.reshape(shp) * w.astype(jnp.float32)).astype(z.dtype)


def parallel_mixer(h, lambda_init, w_in, lq1, lk1, lq2, lk2, subln_w, conv_w, conv_b,
                   a_log, dt_bias, d_skip, ssm_norm_w, w_attn_branch, w_ssm_branch, w_out):
    b, s, _ = h.shape
    proj = h @ w_in
    q, k, v, z, xbc, dt_raw, gates = jnp.split(proj, [O_Q, O_K, O_V, O_Z, O_XBC, O_DT], axis=-1)
    lam = (jnp.exp(jnp.sum(lq1.astype(jnp.float32) * lk1.astype(jnp.float32)))
           - jnp.exp(jnp.sum(lq2.astype(jnp.float32) * lk2.astype(jnp.float32))) + lambda_init)
    q = q.reshape(b, s, ATTN_HEADS, 2, ATTN_HEAD_DIM)
    k = k.reshape(b, s, ATTN_HEADS, 2, ATTN_HEAD_DIM)
    v = v.reshape(b, s, ATTN_HEADS, ATTN_V_DIM)
    attn_out = diff_attention(q, k, v, lam, subln_w, lambda_init)
    xbc = jax.nn.silu(depthwise_conv(xbc, conv_w, conv_b))
    gn = SSM_GROUPS * D_STATE
    xs = xbc[..., :D_INNER].reshape(b, s, SSM_HEADS, SSM_HEAD_DIM).astype(jnp.float32)
    bm = xbc[..., D_INNER:D_INNER + gn].reshape(b, s, SSM_GROUPS, D_STATE).astype(jnp.float32)
    cm = xbc[..., D_INNER + gn:].reshape(b, s, SSM_GROUPS, D_STATE).astype(jnp.float32)
    dt = jax.nn.softplus(dt_raw.astype(jnp.float32).reshape(b, s, 2, SSM_HEADS) + dt_bias.astype(jnp.float32))
    a_heads = -jnp.exp(a_log.astype(jnp.float32))
    y_fwd = ssd_scan(xs, dt[:, :, 0], a_heads[0], bm, cm)
    y_bwd = jnp.flip(ssd_scan(jnp.flip(xs, 1), jnp.flip(dt[:, :, 1], 1), a_heads[1],
                              jnp.flip(bm, 1), jnp.flip(cm, 1)), 1)
    y = y_fwd + y_bwd + xs * d_skip.astype(jnp.float32)[:, None]
    ssm_out = gated_group_rms_norm(y.reshape(b, s, D_INNER), z, ssm_norm_w)
    g = jax.nn.sigmoid(gates.astype(jnp.float32)).reshape(b, s, N_BRANCHES, D_MODEL).astype(h.dtype)
    merged = g[:, :, 0] * (attn_out @ w_attn_branch) + g[:, :, 1] * (ssm_out @ w_ssm_branch)
    return merged @ w_out


def memory_cross_attention(h, mem, norm_mem_w, w_xq, w_xkv, w_xo):
    b, s, _ = h.shape
    m = rms_norm(mem, norm_mem_w)
    q = (h @ w_xq).reshape(b, s, X_HEADS, X_HEAD_DIM)
    kv = (m @ w_xkv).reshape(b, m.shape[1], 2, X_HEADS, X_HEAD_DIM)
    sc = jnp.einsum('bshd,bmhd->bhsm', q, kv[:, :, 0]).astype(jnp.float32) * (X_HEAD_DIM ** -0.5)
    p = jax.nn.softmax(sc, axis=-1).astype(h.dtype)
    o = jnp.einsum('bhsm,bmhd->bshd', p, kv[:, :, 1]).reshape(b, s, X_WIDTH)
    return o @ w_xo


def expert_choice_ffn(h, w_router, b_router, w_gate, w_up, w_down):
    b, s, d = h.shape
    tokens = h.reshape(b * s, d)
    n_tok = b * s
    cap = max(1, CAPACITY_FACTOR * n_tok // N_EXPERTS)
    logits = tokens.astype(jnp.float32) @ w_router.astype(jnp.float32) + b_router.astype(jnp.float32)
    aff = jax.nn.softmax(logits, axis=-1)
    gate, idx = lax.top_k(aff.T, cap)

    def one_expert(args):
        xe, wg, wu, wd, ge = args
        hid = jax.nn.silu(xe @ wg) * (xe @ wu)
        return (hid @ wd) * ge[:, None].astype(xe.dtype)

    ye = lax.map(one_expert, (tokens[idx], w_gate, w_up, w_down, gate))
    out = jnp.zeros_like(tokens).at[idx.reshape(-1)].add(ye.reshape(-1, d))
    return out.reshape(b, s, d)


def setup_inputs(seed: int = 0) -> dict:
    key = jax.random.key(seed)
    ks = jax.random.split(key, 40)
    f32 = jnp.float32

    def nrm(k, shape, scale):
        return jax.random.normal(k, shape, f32) * scale

    def gain(k, shape):
        return 1.0 + 0.02 * jax.random.normal(k, shape, f32)

    dt0 = jnp.exp(jax.random.uniform(ks[12], (DEPTH, 2, SSM_HEADS), f32, math.log(1e-3), math.log(1e-1)))
    return {
        'x_prompt': nrm(ks[0], (BATCH, SEQ, D_MODEL), 1.0),
        'x_sample': nrm(ks[1], (DEC_BATCH, DEC_SEQ, D_MODEL), 1.0),
        'mem_prompt': nrm(ks[2], (BATCH, MEM_TOKENS, D_MODEL), 1.0),
        'mem_sample': nrm(ks[3], (DEC_BATCH, MEM_TOKENS, D_MODEL), 1.0),
        'norm_mix_w': gain(ks[4], (DEPTH, D_MODEL)),
        'w_in': nrm(ks[5], (DEPTH, D_MODEL, IN_COLS), D_MODEL ** -0.5),
        'lambda_q1': nrm(ks[6], (DEPTH, ATTN_HEAD_DIM), 0.1),
        'lambda_k1': nrm(ks[7], (DEPTH, ATTN_HEAD_DIM), 0.1),
        'lambda_q2': nrm(ks[8], (DEPTH, ATTN_HEAD_DIM), 0.1),
        'lambda_k2': nrm(ks[9], (DEPTH, ATTN_HEAD_DIM), 0.1),
        'attn_subln_w': gain(ks[10], (DEPTH, ATTN_V_DIM)),
        'conv_w': nrm(ks[11], (DEPTH, D_CONV, XBC_DIM), D_CONV ** -0.5),
        'conv_b': nrm(ks[13], (DEPTH, XBC_DIM), 0.01),
        'a_log': jnp.log(jax.random.uniform(ks[14], (DEPTH, 2, SSM_HEADS), f32, 1.0, 16.0)),
        'dt_bias': dt0 + jnp.log(-jnp.expm1(-dt0)),
        'd_skip': gain(ks[15], (DEPTH, SSM_HEADS)),
        'ssm_norm_w': gain(ks[16], (DEPTH, D_INNER)),
        'w_attn_branch': nrm(ks[17], (DEPTH, ATTN_WIDTH, D_MODEL), ATTN_WIDTH ** -0.5),
        'w_ssm_branch': nrm(ks[18], (DEPTH, D_INNER, D_MODEL), D_INNER ** -0.5),
        'w_out': nrm(ks[19], (DEPTH, D_MODEL, D_MODEL), D_MODEL ** -0.5),
        'norm_cross_w': gain(ks[20], (DEPTH, D_MODEL)),
        'norm_mem_w': gain(ks[21], (DEPTH, D_MODEL)),
        'w_xq': nrm(ks[22], (DEPTH, D_MODEL, X_WIDTH), D_MODEL ** -0.5),
        'w_xkv': nrm(ks[23], (DEPTH, D_MODEL, 2 * X_WIDTH), D_MODEL ** -0.5),
        'w_xo': nrm(ks[24], (DEPTH, X_WIDTH, D_MODEL), X_WIDTH ** -0.5),
        'norm_ffn_w': gain(ks[25], (DEPTH, D_MODEL)),
        'w_router': nrm(ks[26], (DEPTH, D_MODEL, N_EXPERTS), D_MODEL ** -0.5),
        'b_router': nrm(ks[27], (DEPTH, N_EXPERTS), 0.01),
        'w_gate': nrm(ks[28], (DEPTH, N_EXPERTS, D_MODEL, D_FF_EXPERT), D_MODEL ** -0.5),
        'w_up': nrm(ks[29], (DEPTH, N_EXPERTS, D_MODEL, D_FF_EXPERT), D_MODEL ** -0.5),
        'w_down': nrm(ks[30], (DEPTH, N_EXPERTS, D_FF_EXPERT, D_MODEL), D_FF_EXPERT ** -0.5),
        'norm_final_w': gain(ks[31], (D_MODEL,)),
    }


def reference(x_prompt, x_sample, mem_prompt, mem_sample, norm_mix_w, w_in, lambda_q1, lambda_k1,
              lambda_q2, lambda_k2, attn_subln_w, conv_w, conv_b, a_log, dt_bias, d_skip, ssm_norm_w,
              w_attn_branch, w_ssm_branch, w_out, norm_cross_w, norm_mem_w, w_xq, w_xkv, w_xo,
              norm_ffn_w, w_router, b_router, w_gate, w_up, w_down, norm_final_w):
    def trunk(x, mem):
        for l in range(DEPTH):
            lambda_init = 0.8 - 0.6 * math.exp(-0.3 * l)
            h = rms_norm(x, norm_mix_w[l])
            x = x + parallel_mixer(h, lambda_init, w_in[l], lambda_q1[l], lambda_k1[l], lambda_q2[l],
                                   lambda_k2[l], attn_subln_w[l], conv_w[l], conv_b[l], a_log[l],
                                   dt_bias[l], d_skip[l], ssm_norm_w[l], w_attn_branch[l],
                                   w_ssm_branch[l], w_out[l])
            h = rms_norm(x, norm_cross_w[l])
            x = x + memory_cross_attention(h, mem, norm_mem_w[l], w_xq[l], w_xkv[l], w_xo[l])
            h = rms_norm(x, norm_ffn_w[l])
            x = x + expert_choice_ffn(h, w_router[l], b_router[l], w_gate[l], w_up[l], w_down[l])
        return rms_norm(x, norm_final_w)

    y_prompt = trunk(x_prompt, mem_prompt)
    y_sample = trunk(x_sample, mem_sample)
    return (y_prompt, y_sample)
```

```python
import functools
import math

import jax
import jax.numpy as jnp
from jax import lax
from jax.experimental import pallas as pl
from jax.experimental.pallas import tpu as pltpu

F32 = jnp.float32
BF16 = jnp.bfloat16
I32 = jnp.int32

D_MODEL = 1024
ATTN_HEADS = 8
ATTN_HEAD_DIM = 64
ATTN_V_DIM = 128
SSM_HEADS = 32
SSM_HEAD_DIM = 64
SSM_GROUPS = 4
D_STATE = 128
D_INNER = 2048
D_CONV = 5
XBC_DIM = 3072
CHUNK = 128
MEM_TOKENS = 256
X_HEADS = 4
X_HEAD_DIM = 128
X_WIDTH = 512
N_EXPERTS = 16
CAPACITY_FACTOR = 2
D_FF = 2816
EPS = 1e-6
SUBLN_EPS = 1e-5

C_Z = 0
C_GATE = 2048
C_XBC = 4096
C_Q = 7168
C_K = 8192
C_V = 9216
PROJ_COLS = 10240

LANE = 128
BF16_SUBLANE = 16
VMEM_LIMIT = 56 * 1024 * 1024


def _cparams(sem, vmem=None):
    return pltpu.CompilerParams(dimension_semantics=sem, vmem_limit_bytes=vmem)


def _rms(x, w, eps):
    return x * lax.rsqrt(jnp.mean(x * x, axis=-1, keepdims=True) + eps) * w


def _softplus(x):
    return jnp.maximum(x, 0.0) + jnp.log(1.0 + jnp.exp(-jnp.abs(x)))


def _silu(x):
    return x * (1.0 / (1.0 + jnp.exp(-x)))


def _norm_matmul_kernel(x_ref, nw_ref, w_ref, o_ref, xn_ref):
    @pl.when(pl.program_id(1) == 0)
    def _():
        xn_ref[...] = _rms(x_ref[...], nw_ref[...], EPS).astype(xn_ref.dtype)

    o_ref[...] = jnp.dot(xn_ref[...], w_ref[...], preferred_element_type=F32).astype(o_ref.dtype)


def norm_matmul(x, nw, w, out_dtype, tm, tn):
    t, d = x.shape
    n = w.shape[1]
    return pl.pallas_call(
        _norm_matmul_kernel,
        grid=(t // tm, n // tn),
        in_specs=[
            pl.BlockSpec((tm, d), lambda i, j: (i, 0)),
            pl.BlockSpec((1, d), lambda i, j: (0, 0)),
            pl.BlockSpec((d, tn), lambda i, j: (0, j)),
        ],
        out_specs=pl.BlockSpec((tm, tn), lambda i, j: (i, j)),
        out_shape=jax.ShapeDtypeStruct((t, n), out_dtype),
        scratch_shapes=[pltpu.VMEM((tm, d), BF16)],
        compiler_params=_cparams(("parallel", "arbitrary"), VMEM_LIMIT),
        name="norm_matmul",
    )(x, nw.reshape(1, d), w)


def _attn_kernel(scal_ref, q_ref, k_ref, v_ref, w_ref, o_ref, q0_s, q1_s, m_s, l_s, acc_s, *, tq, tk):
    h = pl.program_id(1)
    qi = pl.program_id(2)
    ki = pl.program_id(3)

    @pl.when(ki == 0)
    def _():
        q = q_ref[...].astype(F32) * (ATTN_HEAD_DIM ** -0.5)
        lane = lax.broadcasted_iota(I32, q.shape, 1)
        q0_s[...] = jnp.where(lane < ATTN_HEAD_DIM, q, 0.0).astype(BF16)
        q1_s[...] = jnp.where(lane >= ATTN_HEAD_DIM, q, 0.0).astype(BF16)
        m_s[...] = jnp.full(m_s.shape, -jnp.inf, F32)
        l_s[...] = jnp.zeros(l_s.shape, F32)
        acc_s[...] = jnp.zeros(acc_s.shape, F32)

    slope = scal_ref[h]
    k = k_ref[...]
    v = v_ref[...]
    row = lax.broadcasted_iota(I32, (tq, tk), 0)
    col = lax.broadcasted_iota(I32, (tq, tk), 1)
    dist = jnp.abs(row - col + (qi * tq - ki * tk)).astype(F32)
    bias = dist * (-slope)
    for c, qs in enumerate((q0_s, q1_s)):
        s = lax.dot_general(qs[...], k, (((1,), (1,)), ((), ())), preferred_element_type=F32) + bias
        m_prev = m_s[c]
        m_new = jnp.maximum(m_prev, jnp.max(s, axis=-1, keepdims=True))
        alpha = jnp.exp(m_prev - m_new)
        p = jnp.exp(s - m_new)
        l_s[c] = alpha * l_s[c] + jnp.sum(p, axis=-1, keepdims=True)
        acc_s[c] = alpha * acc_s[c] + jnp.dot(p.astype(BF16), v, preferred_element_type=F32)
        m_s[c] = m_new

    @pl.when(ki == pl.num_programs(3) - 1)
    def _():
        lam = scal_ref[ATTN_HEADS]
        post = scal_ref[ATTN_HEADS + 1]
        o = acc_s[0] / l_s[0] - lam * (acc_s[1] / l_s[1])
        o_ref[...] = (_rms(o, w_ref[...], SUBLN_EPS) * post).astype(o_ref.dtype)


def diff_attention(proj3, scal, subln_w, tq, tk):
    b, s, _ = proj3.shape
    kq, kk, kv = C_Q // LANE, C_K // LANE, C_V // LANE
    return pl.pallas_call(
        functools.partial(_attn_kernel, tq=tq, tk=tk),
        grid=(b, ATTN_HEADS, s // tq, s // tk),
        in_specs=[
            pl.BlockSpec(memory_space=pltpu.SMEM),
            pl.BlockSpec((None, tq, LANE), lambda bi, h, qi, ki: (bi, qi, kq + h)),
            pl.BlockSpec((None, tk, LANE), lambda bi, h, qi, ki: (bi, ki, kk + h)),
            pl.BlockSpec((None, tk, LANE), lambda bi, h, qi, ki: (bi, ki, kv + h)),
            pl.BlockSpec((1, LANE), lambda bi, h, qi, ki: (0, 0)),
        ],
        out_specs=pl.BlockSpec((None, tq, LANE), lambda bi, h, qi, ki: (bi, qi, h)),
        out_shape=jax.ShapeDtypeStruct((b, s, ATTN_HEADS * ATTN_V_DIM), BF16),
        scratch_shapes=[
            pltpu.VMEM((tq, LANE), BF16),
            pltpu.VMEM((tq, LANE), BF16),
            pltpu.VMEM((2, tq, 1), F32),
            pltpu.VMEM((2, tq, 1), F32),
            pltpu.VMEM((2, tq, LANE), F32),
        ],
        compiler_params=_cparams(("parallel", "parallel", "parallel", "arbitrary"), VMEM_LIMIT),
        name="diff_attention",
    )(scal, proj3, proj3, proj3, subln_w.reshape(1, LANE))


HALO = 16


def _conv_kernel(prev_ref, cur_ref, next_ref, w_ref, b_ref, o_ref, ext_ref, *, tm):
    i = pl.program_id(1)
    last = pl.num_programs(1) - 1
    prev = prev_ref[...].astype(F32)
    nxt = next_ref[...].astype(F32)
    ext_ref[pl.ds(0, HALO), :] = jnp.where(i > 0, prev, 0.0)
    ext_ref[pl.ds(HALO, tm), :] = cur_ref[...].astype(F32)
    ext_ref[pl.ds(HALO + tm, HALO), :] = jnp.where(i < last, nxt, 0.0)
    w = w_ref[...]
    acc = jnp.zeros(o_ref.shape, F32) + b_ref[...]
    for j in range(D_CONV):
        acc = acc + ext_ref[pl.ds(HALO - D_CONV // 2 + j, tm), :] * w[j:j + 1, :]
    o_ref[...] = _silu(acc).astype(o_ref.dtype)


def conv_silu(proj3, conv_w, conv_b, tm, tc):
    b, s, _ = proj3.shape
    c0 = C_XBC // tc
    hb = tm // HALO
    nh = s // HALO
    return pl.pallas_call(
        functools.partial(_conv_kernel, tm=tm),
        grid=(b, s // tm, XBC_DIM // tc),
        in_specs=[
            pl.BlockSpec((None, HALO, tc), lambda bi, i, j: (bi, jnp.maximum(i * hb - 1, 0), c0 + j)),
            pl.BlockSpec((None, tm, tc), lambda bi, i, j: (bi, i, c0 + j)),
            pl.BlockSpec((None, HALO, tc), lambda bi, i, j: (bi, jnp.minimum((i + 1) * hb, nh - 1), c0 + j)),
            pl.BlockSpec((D_CONV, tc), lambda bi, i, j: (0, j)),
            pl.BlockSpec((1, tc), lambda bi, i, j: (0, j)),
        ],
        out_specs=pl.BlockSpec((None, tm, tc), lambda bi, i, j: (bi, i, j)),
        out_shape=jax.ShapeDtypeStruct((b, s, XBC_DIM), BF16),
        scratch_shapes=[pltpu.VMEM((tm + 2 * HALO, tc), F32)],
        compiler_params=_cparams(("parallel", "parallel", "parallel"), VMEM_LIMIT),
        name="conv_silu",
    )(proj3, proj3, proj3, conv_w, conv_b.reshape(1, XBC_DIM))


def _expand_heads(a):
    rows = a.shape[0]
    lane = lax.broadcasted_iota(I32, (rows, LANE), 1)
    parts = []
    for hp in range(SSM_HEADS // 2):
        lo = a[:, 2 * hp:2 * hp + 1]
        hi = a[:, 2 * hp + 1:2 * hp + 2]
        parts.append(jnp.where(lane < SSM_HEAD_DIM, lo, hi))
    return jnp.concatenate(parts, axis=1)


def _tri(n, fn):
    r = lax.broadcasted_iota(I32, (n, n), 0)
    c = lax.broadcasted_iota(I32, (n, n), 1)
    return jnp.where(fn(r, c), 1.0, 0.0).astype(F32)


def _dot_hi(a, b):
    return jnp.dot(a, b, preferred_element_type=F32, precision=lax.Precision.HIGHEST)


def _ssd_state_kernel(xf_ref, bf_ref, dtf_ref, xb_ref, bb_ref, dtb_ref, par_ref, hf_ref, hb_ref, sf_s, sb_s):
    j = pl.program_id(1)

    @pl.when(j == 0)
    def _():
        sf_s[...] = jnp.zeros(sf_s.shape, F32)
        sb_s[...] = jnp.zeros(sb_s.shape, F32)

    hf_ref[...] = sf_s[...].astype(hf_ref.dtype)
    hb_ref[...] = sb_s[...].astype(hb_ref.dtype)

    par = par_ref[...]
    lower_incl = _tri(CHUNK, lambda r, c: c <= r)
    lower_strict = _tri(CHUNK, lambda r, c: c < r)

    def one_direction(x_ref, b_ref, dt_ref, s_ref, col0, backward):
        raw = dt_ref[...][:, col0:col0 + SSM_HEADS]
        dt = _softplus(raw + par[0:1, col0:col0 + SSM_HEADS])
        a = dt * par[1:2, col0:col0 + SSM_HEADS]
        if backward:
            excl = _dot_hi(lower_strict, a)
            wgt = dt * jnp.exp(excl)
            total = excl[CHUNK - 1:CHUNK, :] + a[CHUNK - 1:CHUNK, :]
        else:
            cum = _dot_hi(lower_incl, a)
            total = cum[CHUNK - 1:CHUNK, :]
            wgt = dt * jnp.exp(total - cum)
        xw = (x_ref[...].astype(F32) * _expand_heads(wgt)).astype(BF16)
        dec = _expand_heads(jnp.exp(jnp.broadcast_to(total, (8, SSM_HEADS))))[0:1, :]
        bmat = b_ref[...]
        gw = D_INNER // SSM_GROUPS
        for g in range(SSM_GROUPS):
            contrib = lax.dot_general(bmat[:, g * D_STATE:(g + 1) * D_STATE], xw[:, g * gw:(g + 1) * gw],
                                      (((0,), (0,)), ((), ())), preferred_element_type=F32)
            s_ref[g] = s_ref[g] * dec[:, g * gw:(g + 1) * gw] + contrib

    one_direction(xf_ref, bf_ref, dtf_ref, sf_s, 0, False)
    one_direction(xb_ref, bb_ref, dtb_ref, sb_s, SSM_HEADS, True)


def ssd_states(xbc, dt3, par):
    b, s, _ = xbc.shape
    nc = s // CHUNK
    gw = D_INNER // SSM_GROUPS
    xblk = D_INNER // D_INNER
    bcol = D_INNER // (SSM_GROUPS * D_STATE)
    hshape = jax.ShapeDtypeStruct((b, nc, SSM_GROUPS, D_STATE, gw), BF16)
    hspec_f = pl.BlockSpec((None, None, SSM_GROUPS, D_STATE, gw), lambda bi, j: (bi, j, 0, 0, 0))
    hspec_b = pl.BlockSpec((None, None, SSM_GROUPS, D_STATE, gw), lambda bi, j: (bi, nc - 1 - j, 0, 0, 0))
    del xblk
    return pl.pallas_call(
        _ssd_state_kernel,
        grid=(b, nc),
        in_specs=[
            pl.BlockSpec((None, CHUNK, D_INNER), lambda bi, j: (bi, j, 0)),
            pl.BlockSpec((None, CHUNK, SSM_GROUPS * D_STATE), lambda bi, j: (bi, j, bcol)),
            pl.BlockSpec((None, CHUNK, LANE), lambda bi, j: (bi, j, 0)),
            pl.BlockSpec((None, CHUNK, D_INNER), lambda bi, j: (bi, nc - 1 - j, 0)),
            pl.BlockSpec((None, CHUNK, SSM_GROUPS * D_STATE), lambda bi, j: (bi, nc - 1 - j, bcol)),
            pl.BlockSpec((None, CHUNK, LANE), lambda bi, j: (bi, nc - 1 - j, 0)),
            pl.BlockSpec((8, LANE), lambda bi, j: (0, 0)),
        ],
        out_specs=[hspec_f, hspec_b],
        out_shape=[hshape, hshape],
        scratch_shapes=[pltpu.VMEM((SSM_GROUPS, D_STATE, gw), F32), pltpu.VMEM((SSM_GROUPS, D_STATE, gw), F32)],
        compiler_params=_cparams(("parallel", "arbitrary"), VMEM_LIMIT),
        name="ssd_states",
    )(xbc, xbc, dt3, xbc, xbc, dt3, par)


def _ssd_out_kernel(x_ref, b_ref, c_ref, z_ref, dt_ref, dtt_ref, par_ref, part_ref, dsk_ref, nw_ref,
                    hf_ref, hb_ref, o_ref, y_s):
    par = par_ref[...]
    part = part_ref[...]
    lower_incl = _tri(CHUNK, lambda r, c: c <= r)
    lower_strict = _tri(CHUNK, lambda r, c: c < r)
    upper_incl = _tri(CHUNK, lambda r, c: r <= c)
    upper_strict = _tri(CHUNK, lambda r, c: r < c)

    dtc = _softplus(dt_ref[...][:, 0:2 * SSM_HEADS] + par[0:1, 0:2 * SSM_HEADS])
    ac = dtc * par[1:2, 0:2 * SSM_HEADS]
    cum_c = _dot_hi(lower_incl, ac)
    excl_c = _dot_hi(lower_strict, ac)
    dtr = _softplus(dtt_ref[...][0:2 * SSM_HEADS, :] + part[0:2 * SSM_HEADS, 0:1])
    ar = dtr * part[0:2 * SSM_HEADS, 1:2]
    cum_r = _dot_hi(ar, upper_incl)
    excl_r = _dot_hi(ar, upper_strict)

    cumf_c = cum_c[:, 0:SSM_HEADS]
    exclb_c = excl_c[:, SSM_HEADS:2 * SSM_HEADS]
    totb = cum_c[CHUNK - 1:CHUNK, SSM_HEADS:2 * SSM_HEADS]
    cumf_r = cum_r[0:SSM_HEADS, :]
    exclb_r = excl_r[SSM_HEADS:2 * SSM_HEADS, :]
    dtf_r = dtr[0:SSM_HEADS, :]
    dtb_r = dtr[SSM_HEADS:2 * SSM_HEADS, :]

    row = lax.broadcasted_iota(I32, (CHUNK, CHUNK), 0)
    col = lax.broadcasted_iota(I32, (CHUNK, CHUNK), 1)
    low = col < row
    diag = col == row

    x = x_ref[...]
    bm = b_ref[...]
    cm = c_ref[...]
    gw = D_INNER // SSM_GROUPS
    hpg = SSM_HEADS // SSM_GROUPS
    scale_f = _expand_heads(jnp.exp(cumf_c))
    scale_b = _expand_heads(jnp.exp(totb - exclb_c))
    for g in range(SSM_GROUPS):
        cg = cm[:, g * D_STATE:(g + 1) * D_STATE]
        bg = bm[:, g * D_STATE:(g + 1) * D_STATE]
        cb = lax.dot_general(cg, bg, (((1,), (1,)), ((), ())), preferred_element_type=F32)
        inter_f = jnp.dot(cg, hf_ref[g], preferred_element_type=F32)
        inter_b = jnp.dot(cg, hb_ref[g], preferred_element_type=F32)
        sl = slice(g * gw, (g + 1) * gw)
        y_s[:, sl] = inter_f * scale_f[:, sl] + inter_b * scale_b[:, sl]
        for hh in range(hpg):
            h = g * hpg + hh
            arg = jnp.where(low, cumf_c[:, h:h + 1] - cumf_r[h:h + 1, :],
                            exclb_r[h:h + 1, :] - exclb_c[:, h:h + 1])
            arg = jnp.where(diag, 0.0, arg)
            wdt = jnp.where(low, dtf_r[h:h + 1, :], dtb_r[h:h + 1, :])
            wdt = jnp.where(diag, dtf_r[h:h + 1, :] + dtb_r[h:h + 1, :], wdt)
            gmat = (cb * jnp.exp(arg) * wdt).astype(BF16)
            hs = slice(h * SSM_HEAD_DIM, (h + 1) * SSM_HEAD_DIM)
            y_s[:, hs] += jnp.dot(gmat, x[:, hs], preferred_element_type=F32)

    y = y_s[...] + x.astype(F32) * dsk_ref[...]
    yf = y * _silu(z_ref[...].astype(F32))
    nw = nw_ref[...]
    for g in range(SSM_GROUPS):
        sl = slice(g * gw, (g + 1) * gw)
        yg = yf[:, sl]
        o_ref[:, sl] = (yg * lax.rsqrt(jnp.mean(yg * yg, axis=-1, keepdims=True) + EPS)
                        * nw[:, sl]).astype(o_ref.dtype)


def ssd_output(xbc, proj3, dt3, dtt3, par, part, dskip_x, nw, hf, hb):
    b, s, _ = xbc.shape
    nc = s // CHUNK
    gw = D_INNER // SSM_GROUPS
    gs = SSM_GROUPS * D_STATE
    hspec = pl.BlockSpec((None, None, SSM_GROUPS, D_STATE, gw), lambda bi, j: (bi, j, 0, 0, 0))
    return pl.pallas_call(
        _ssd_out_kernel,
        grid=(b, nc),
        in_specs=[
            pl.BlockSpec((None, CHUNK, D_INNER), lambda bi, j: (bi, j, 0)),
            pl.BlockSpec((None, CHUNK, gs), lambda bi, j: (bi, j, D_INNER // gs)),
            pl.BlockSpec((None, CHUNK, gs), lambda bi, j: (bi, j, D_INNER // gs + 1)),
            pl.BlockSpec((None, CHUNK, D_INNER), lambda bi, j: (bi, j, C_Z // D_INNER)),
            pl.BlockSpec((None, CHUNK, LANE), lambda bi, j: (bi, j, 0)),
            pl.BlockSpec((None, LANE, CHUNK), lambda bi, j: (bi, 0, j)),
            pl.BlockSpec((8, LANE), lambda bi, j: (0, 0)),
            pl.BlockSpec((LANE, 8), lambda bi, j: (0, 0)),
            pl.BlockSpec((1, D_INNER), lambda bi, j: (0, 0)),
            pl.BlockSpec((1, D_INNER), lambda bi, j: (0, 0)),
            hspec, hspec,
        ],
        out_specs=pl.BlockSpec((None, CHUNK, D_INNER), lambda bi, j: (bi, j, 0)),
        out_shape=jax.ShapeDtypeStruct((b, s, D_INNER), BF16),
        scratch_shapes=[pltpu.VMEM((CHUNK, D_INNER), F32)],
        compiler_params=_cparams(("parallel", "parallel"), VMEM_LIMIT),
        name="ssd_output",
    )(xbc, xbc, xbc, proj3, dt3, dtt3, par, part, dskip_x, nw, hf, hb)


def _merge_kernel(x_ref, a_ref, s_ref, g0_ref, g1_ref, wa_ref, ws_ref, wo_ref, o_ref):
    ya = jnp.dot(a_ref[...], wa_ref[...], preferred_element_type=F32)
    ys = jnp.dot(s_ref[...], ws_ref[...], preferred_element_type=F32)
    g0 = 1.0 / (1.0 + jnp.exp(-g0_ref[...].astype(F32)))
    g1 = 1.0 / (1.0 + jnp.exp(-g1_ref[...].astype(F32)))
    merged = (g0 * ya + g1 * ys).astype(BF16)
    o_ref[...] = x_ref[...] + jnp.dot(merged, wo_ref[...], preferred_element_type=F32)


def merge_out(x, attn, ssm, proj, wa, ws, wo, tm):
    t, d = x.shape
    gb = C_GATE // d
    const = lambda i: (0, 0)
    return pl.pallas_call(
        _merge_kernel,
        grid=(t // tm,),
        in_specs=[
            pl.BlockSpec((tm, d), lambda i: (i, 0)),
            pl.BlockSpec((tm, d), lambda i: (i, 0)),
            pl.BlockSpec((tm, D_INNER), lambda i: (i, 0)),
            pl.BlockSpec((tm, d), lambda i: (i, gb)),
            pl.BlockSpec((tm, d), lambda i: (i, gb + 1)),
            pl.BlockSpec((d, d), const),
            pl.BlockSpec((D_INNER, d), const),
            pl.BlockSpec((d, d), const),
        ],
        out_specs=pl.BlockSpec((tm, d), lambda i: (i, 0)),
        out_shape=jax.ShapeDtypeStruct((t, d), F32),
        compiler_params=_cparams(("parallel",), VMEM_LIMIT),
        name="merge_out",
    )(x, attn, ssm, proj, proj, wa, ws, wo)


def _cross_kernel(x_ref, nw_ref, wq_ref, kv_ref, wo_ref, o_ref):
    x = x_ref[...]
    h = _rms(x, nw_ref[...], EPS).astype(BF16)
    q = jnp.dot(h, wq_ref[...], preferred_element_type=F32).astype(BF16)
    kv = kv_ref[...]
    outs = []
    for hd in range(X_HEADS):
        qh = q[:, hd * X_HEAD_DIM:(hd + 1) * X_HEAD_DIM]
        kh = kv[:, hd * X_HEAD_DIM:(hd + 1) * X_HEAD_DIM]
        vh = kv[:, X_WIDTH + hd * X_HEAD_DIM:X_WIDTH + (hd + 1) * X_HEAD_DIM]
        s = lax.dot_general(qh, kh, (((1,), (1,)), ((), ())), preferred_element_type=F32) * (X_HEAD_DIM ** -0.5)
        s = s - jnp.max(s, axis=-1, keepdims=True)
        p = jnp.exp(s)
        p = (p / jnp.sum(p, axis=-1, keepdims=True)).astype(BF16)
        outs.append(jnp.dot(p, vh, preferred_element_type=F32).astype(BF16))
    o = jnp.concatenate(outs, axis=1)
    o_ref[...] = x + jnp.dot(o, wo_ref[...], preferred_element_type=F32)


def cross_attention(x, nw, wq, kv, wo, seq, tm):
    t, d = x.shape
    per_seq = seq // tm
    const = lambda i: (0, 0)
    return pl.pallas_call(
        _cross_kernel,
        grid=(t // tm,),
        in_specs=[
            pl.BlockSpec((tm, d), lambda i: (i, 0)),
            pl.BlockSpec((1, d), const),
            pl.BlockSpec((d, X_WIDTH), const),
            pl.BlockSpec((None, MEM_TOKENS, 2 * X_WIDTH), lambda i: (i // per_seq, 0, 0)),
            pl.BlockSpec((X_WIDTH, d), const),
        ],
        out_specs=pl.BlockSpec((tm, d), lambda i: (i, 0)),
        out_shape=jax.ShapeDtypeStruct((t, d), F32),
        compiler_params=_cparams(("parallel",), VMEM_LIMIT),
        name="cross_attention",
    )(x, nw.reshape(1, d), wq, kv, wo)


def _router_kernel(x_ref, nw_ref, wrt_ref, br_ref, hb_ref, aff_ref):
    h = _rms(x_ref[...], nw_ref[...], EPS)
    hb_ref[...] = h.astype(hb_ref.dtype)
    logits = lax.dot_general(wrt_ref[...], h, (((1,), (1,)), ((), ())), preferred_element_type=F32,
                             precision=lax.Precision.HIGHEST) + br_ref[...]
    e = jnp.exp(logits - jnp.max(logits, axis=0, keepdims=True))
    aff_ref[...] = e / jnp.sum(e, axis=0, keepdims=True)


def router(x, nw, wrt, br, tm):
    t, d = x.shape
    return pl.pallas_call(
        _router_kernel,
        grid=(t // tm,),
        in_specs=[
            pl.BlockSpec((tm, d), lambda i: (i, 0)),
            pl.BlockSpec((1, d), lambda i: (0, 0)),
            pl.BlockSpec((N_EXPERTS, d), lambda i: (0, 0)),
            pl.BlockSpec((N_EXPERTS, 1), lambda i: (0, 0)),
        ],
        out_specs=[pl.BlockSpec((tm, d), lambda i: (i, 0)), pl.BlockSpec((N_EXPERTS, tm), lambda i: (0, i))],
        out_shape=[jax.ShapeDtypeStruct((t, d), BF16), jax.ShapeDtypeStruct((N_EXPERTS, t), F32)],
        compiler_params=_cparams(("parallel",), VMEM_LIMIT),
        name="router",
    )(x, nw.reshape(1, d), wrt, br.reshape(N_EXPERTS, 1))


TB = 256


def _topk_kernel(aff_ref, pos_ref, gate_ref, roff_ref, *, cap):
    a = aff_ref[...]
    nr = a.shape[1]
    bits = pltpu.bitcast(a, I32)

    def count(mask):
        c = jnp.sum(jnp.where(mask, 1.0, 0.0), axis=2, keepdims=True)
        return jnp.sum(c, axis=1, keepdims=True)

    def body(i, thr):
        cand = thr | jnp.left_shift(jnp.int32(1), 30 - i)
        return jnp.where(count(bits >= cand) >= float(cap), cand, thr)

    thr = lax.fori_loop(0, 31, body, jnp.zeros((N_EXPERTS, 1, 1), I32))
    gt = bits > thr
    eq = bits == thr
    need = float(cap) - count(gt)

    ustrict = _tri(TB, lambda r, c: r < c).astype(BF16)
    ones = jnp.ones((TB, TB), BF16)
    lstrict = _tri(nr, lambda r, c: c < r).astype(BF16)

    def excl_prefix(m):
        mb = m.astype(BF16)
        within = jnp.dot(mb, ustrict, preferred_element_type=F32)
        rowsum = jnp.dot(mb, ones, preferred_element_type=F32)
        rowoff = jnp.dot(lstrict, rowsum.astype(BF16), preferred_element_type=F32)
        return within + rowoff, rowoff

    for e in range(N_EXPERTS):
        eq_e = jnp.where(eq[e], 1.0, 0.0)
        pe, _ = excl_prefix(eq_e)
        keep = jnp.where(pe < need[e], eq_e, 0.0)
        sel = jnp.where(gt[e], 1.0, keep)
        ps, roff = excl_prefix(sel)
        chosen = sel > 0.5
        pos_ref[e] = jnp.where(chosen, ps.astype(I32), -1)
        gate_ref[e] = jnp.where(chosen, a[e], 0.0)
        roff_ref[e] = roff.astype(I32)


def topk_select(aff3, cap):
    e, r, tb = aff3.shape
    full = lambda: (0, 0, 0)
    return pl.pallas_call(
        functools.partial(_topk_kernel, cap=cap),
        grid=(),
        in_specs=[pl.BlockSpec((e, r, tb), full)],
        out_specs=[pl.BlockSpec((e, r, tb), full)] * 3,
        out_shape=[jax.ShapeDtypeStruct((e, r, tb), I32), jax.ShapeDtypeStruct((e, r, tb), F32),
                   jax.ShapeDtypeStruct((e, r, tb), I32)],
        compiler_params=pltpu.CompilerParams(vmem_limit_bytes=VMEM_LIMIT),
        name="topk_select",
    )(aff3)


def _ffn_kernel(lo_ref, hi_ref, pos_ref, gate_ref, hb_hbm, wg_ref, wu_ref, wd_ref, y_ref,
                xbuf, sem, xc_s, g_s, *, ts, nj):
    e = pl.program_id(0)
    j = pl.program_id(1)
    lo = lo_ref[e * nj + j]
    hi = hi_ref[e * nj + j]
    xc_s[...] = jnp.zeros(xc_s.shape, F32)
    g_s[...] = jnp.zeros(g_s.shape, F32)

    def fetch(r, slot):
        return pltpu.make_async_copy(hb_hbm.at[pl.ds(r * TB, TB)], xbuf.at[slot], sem.at[slot])

    @pl.when(lo < hi)
    def _():
        fetch(lo, 0).start()

    slot_ids = lax.broadcasted_iota(I32, (ts, TB), 0) + j * ts

    def body(r, carry):
        slot = (r - lo) & 1
        fetch(r, slot).wait()

        @pl.when(r + 1 < hi)
        def _():
            fetch(r + 1, 1 - slot).start()

        hit = pos_ref[pl.ds(r, 1), :] == slot_ids
        sel = jnp.where(hit, 1.0, 0.0).astype(BF16)
        xc_s[...] += jnp.dot(sel, xbuf[slot], preferred_element_type=F32)
        g_s[...] += jnp.sum(jnp.where(hit, gate_ref[pl.ds(r, 1), :], 0.0), axis=1, keepdims=True)
        return carry

    lax.fori_loop(lo, hi, body, 0)

    xc = xc_s[...].astype(BF16)
    hg = jnp.dot(xc, wg_ref[...], preferred_element_type=F32)
    hu = jnp.dot(xc, wu_ref[...], preferred_element_type=F32)
    hid = (_silu(hg) * hu).astype(BF16)
    y = jnp.dot(hid, wd_ref[...], preferred_element_type=F32) * g_s[...]
    y_ref[...] = y.astype(y_ref.dtype)


def expert_ffn(lo, hi, pos, gate, hb, wg, wu, wd, cap, ts):
    e, r, tb = pos.shape
    d = hb.shape[1]
    nj = cap // ts
    grid_spec = pltpu.PrefetchScalarGridSpec(
        num_scalar_prefetch=2,
        grid=(e, nj),
        in_specs=[
            pl.BlockSpec((None, r, tb), lambda ei, j, lo_r, hi_r: (ei, 0, 0)),
            pl.BlockSpec((None, r, tb), lambda ei, j, lo_r, hi_r: (ei, 0, 0)),
            pl.BlockSpec(memory_space=pl.ANY),
            pl.BlockSpec((None, d, D_FF), lambda ei, j, lo_r, hi_r: (ei, 0, 0), pipeline_mode=pl.Buffered(1)),
            pl.BlockSpec((None, d, D_FF), lambda ei, j, lo_r, hi_r: (ei, 0, 0), pipeline_mode=pl.Buffered(1)),
            pl.BlockSpec((None, D_FF, d), lambda ei, j, lo_r, hi_r: (ei, 0, 0), pipeline_mode=pl.Buffered(1)),
        ],
        out_specs=pl.BlockSpec((None, ts, d), lambda ei, j, lo_r, hi_r: (ei, j, 0)),
        scratch_shapes=[
            pltpu.VMEM((2, TB, d), BF16),
            pltpu.SemaphoreType.DMA((2,)),
            pltpu.VMEM((ts, d), F32),
            pltpu.VMEM((ts, 1), F32),
        ],
    )
    return pl.pallas_call(
        functools.partial(_ffn_kernel, ts=ts, nj=nj),
        grid_spec=grid_spec,
        out_shape=jax.ShapeDtypeStruct((e, cap, d), BF16),
        compiler_params=_cparams(("arbitrary", "arbitrary"), VMEM_LIMIT),
        name="expert_ffn",
    )(lo, hi, pos, gate, hb, wg, wu, wd)


def _combine_kernel(start_ref, x_ref, post_ref, *rest, win):
    y_refs = rest[:N_EXPERTS]
    o_ref = rest[N_EXPERTS]
    r = pl.program_id(0)
    nr = pl.num_programs(0)
    post = post_ref[...]
    lane = lax.broadcasted_iota(I32, (TB, win), 1)
    acc = x_ref[...]
    for e in range(N_EXPERTS):
        rel = post[:, e:e + 1] - start_ref[e * nr + r]
        sel = jnp.where(rel == lane, 1.0, 0.0).astype(BF16)
        acc = acc + jnp.dot(sel, y_refs[e][...], preferred_element_type=F32)
    o_ref[...] = acc


def combine(x, post, start, y, win):
    t, d = x.shape
    e, cap, _ = y.shape
    nr = t // TB

    def y_spec(ei):
        return pl.BlockSpec((pl.Element(win), pl.Element(d)),
                            lambda r, st: (pl.multiple_of(ei * cap + st[ei * nr + r], BF16_SUBLANE), 0))

    grid_spec = pltpu.PrefetchScalarGridSpec(
        num_scalar_prefetch=1,
        grid=(nr,),
        in_specs=[pl.BlockSpec((TB, d), lambda r, st: (r, 0)),
                  pl.BlockSpec((TB, e), lambda r, st: (r, 0))] + [y_spec(ei) for ei in range(e)],
        out_specs=pl.BlockSpec((TB, d), lambda r, st: (r, 0)),
    )
    return pl.pallas_call(
        functools.partial(_combine_kernel, win=win),
        grid_spec=grid_spec,
        out_shape=jax.ShapeDtypeStruct((t, d), F32),
        compiler_params=_cparams(("arbitrary",), VMEM_LIMIT),
        name="combine",
    )(start, x, post, *([y.reshape(e * cap, d)] * e))


def _norm_kernel(x_ref, w_ref, o_ref):
    o_ref[...] = _rms(x_ref[...], w_ref[...], EPS)


def final_norm(x, w, tm):
    t, d = x.shape
    return pl.pallas_call(
        _norm_kernel,
        grid=(t // tm,),
        in_specs=[pl.BlockSpec((tm, d), lambda i: (i, 0)), pl.BlockSpec((1, d), lambda i: (0, 0))],
        out_specs=pl.BlockSpec((tm, d), lambda i: (i, 0)),
        out_shape=jax.ShapeDtypeStruct((t, d), F32),
        compiler_params=_cparams(("parallel",)),
        name="final_norm",
    )(x, w.reshape(1, d))


def _tiles(t, s):
    def fit(n, pref):
        while n % pref:
            pref //= 2
        return pref

    cap = max(1, CAPACITY_FACTOR * t // N_EXPERTS)
    return dict(
        tm_proj=fit(t, 1024), tn_proj=1024,
        tq=fit(s, 1024), tk=fit(s, 512),
        tm_conv=fit(s, 512), tc_conv=1024,
        tm_tok=fit(s, 512),
        cap=cap, ts=fit(cap, 256),
        win=TB + BF16_SUBLANE,
    )


def _prep_layer(l, p):
    w_in = p['w_in'][l]
    o_q, o_k, o_v, o_z, o_xbc, o_dt = 1024, 2048, 3072, 5120, 8192, 8256
    w_main = jnp.concatenate([w_in[:, o_v:o_z], w_in[:, o_dt:], w_in[:, o_z:o_xbc], w_in[:, :o_v]],
                             axis=1).astype(BF16)
    w_dt = jnp.pad(w_in[:, o_xbc:o_dt], ((0, 0), (0, LANE - 2 * SSM_HEADS))).astype(BF16)
    dt_bias = p['dt_bias'][l].reshape(-1).astype(F32)
    a_neg = -jnp.exp(p['a_log'][l].astype(F32)).reshape(-1)
    par = jnp.zeros((8, LANE), F32).at[0, :2 * SSM_HEADS].set(dt_bias).at[1, :2 * SSM_HEADS].set(a_neg)
    part = jnp.zeros((LANE, 8), F32).at[:2 * SSM_HEADS, 0].set(dt_bias).at[:2 * SSM_HEADS, 1].set(a_neg)
    lam_init = 0.8 - 0.6 * math.exp(-0.3 * l)
    lam = (jnp.exp(jnp.sum(p['lambda_q1'][l].astype(F32) * p['lambda_k1'][l].astype(F32)))
           - jnp.exp(jnp.sum(p['lambda_q2'][l].astype(F32) * p['lambda_k2'][l].astype(F32))) + lam_init)
    slopes = jnp.exp2(-8.0 * (jnp.arange(ATTN_HEADS, dtype=F32) + 1.0) / ATTN_HEADS)
    scal = jnp.concatenate([slopes, lam.reshape(1), jnp.full((1,), 1.0 - lam_init, F32)]).astype(F32)
    return dict(
        norm_mix=p['norm_mix_w'][l], w_main=w_main, w_dt=w_dt, par=par, part=part, scal=scal,
        subln=p['attn_subln_w'][l].astype(F32),
        conv_w=p['conv_w'][l].astype(F32), conv_b=p['conv_b'][l].astype(F32),
        dskip_x=jnp.repeat(p['d_skip'][l].astype(F32), SSM_HEAD_DIM).reshape(1, D_INNER),
        ssm_nw=p['ssm_norm_w'][l].astype(F32).reshape(1, D_INNER),
        wa=p['w_attn_branch'][l].astype(BF16), ws=p['w_ssm_branch'][l].astype(BF16),
        wo=p['w_out'][l].astype(BF16),
        norm_cross=p['norm_cross_w'][l], norm_mem=p['norm_mem_w'][l],
        wxq=p['w_xq'][l].astype(BF16), wxkv=p['w_xkv'][l].astype(BF16), wxo=p['w_xo'][l].astype(BF16),
        norm_ffn=p['norm_ffn_w'][l], wrt=p['w_router'][l].astype(F32).T, br=p['b_router'][l].astype(F32),
        wg=p['w_gate'][l].astype(BF16), wu=p['w_up'][l].astype(BF16), wd=p['w_down'][l].astype(BF16),
    )


def _layer(x, mem, lw, b, s):
    t = b * s
    tl = _tiles(t, s)
    proj = norm_matmul(x, lw['norm_mix'], lw['w_main'], BF16, tl['tm_proj'], tl['tn_proj'])
    dt = norm_matmul(x, lw['norm_mix'], lw['w_dt'], F32, tl['tm_proj'], LANE)
    proj3 = proj.reshape(b, s, PROJ_COLS)
    dt3 = dt.reshape(b, s, LANE)
    dtt3 = jnp.swapaxes(dt3, 1, 2)
    attn = diff_attention(proj3, lw['scal'], lw['subln'], tl['tq'], tl['tk'])
    xbc = conv_silu(proj3, lw['conv_w'], lw['conv_b'], tl['tm_conv'], tl['tc_conv'])
    hf, hb = ssd_states(xbc, dt3, lw['par'])
    ssm = ssd_output(xbc, proj3, dt3, dtt3, lw['par'], lw['part'], lw['dskip_x'], lw['ssm_nw'], hf, hb)
    x = merge_out(x, attn.reshape(t, D_MODEL), ssm.reshape(t, D_INNER), proj, lw['wa'], lw['ws'], lw['wo'],
                  tl['tm_tok'])
    nm = mem.shape[0] * mem.shape[1]
    kv = norm_matmul(mem.reshape(nm, D_MODEL), lw['norm_mem'], lw['wxkv'], BF16, min(nm, 512), 2 * X_WIDTH)
    x = cross_attention(x, lw['norm_cross'], lw['wxq'], kv.reshape(b, MEM_TOKENS, 2 * X_WIDTH), lw['wxo'], s,
                        tl['tm_tok'])
    cap, ts = tl['cap'], tl['ts']
    nr = t // TB
    hbf, aff = router(x, lw['norm_ffn'], lw['wrt'], lw['br'], tl['tm_tok'])
    pos, gate, roff = topk_select(aff.reshape(N_EXPERTS, nr, TB), cap)
    roff = roff[:, :, 0]
    rend = jnp.concatenate([roff[:, 1:], jnp.full((N_EXPERTS, 1), cap, I32)], axis=1)
    edges = jnp.arange(cap // ts, dtype=I32) * ts
    lo = jnp.sum(rend[:, None, :] <= edges[None, :, None], axis=2).astype(I32).reshape(-1)
    hi = jnp.sum(roff[:, None, :] < (edges + ts)[None, :, None], axis=2).astype(I32).reshape(-1)
    y = expert_ffn(lo, hi, pos, gate, hbf, lw['wg'], lw['wu'], lw['wd'], cap, ts)
    win = tl['win']
    start = jnp.minimum((roff // BF16_SUBLANE) * BF16_SUBLANE, cap - win).astype(I32).reshape(-1)
    post = jnp.transpose(pos.reshape(N_EXPERTS, t))
    return combine(x, post, start, y, win)


def _trunk(x, mem, layers, norm_final_w):
    b, s, d = x.shape
    xt = x.reshape(b * s, d)
    for lw in layers:
        xt = _layer(xt, mem, lw, b, s)
    return final_norm(xt, norm_final_w, 512).reshape(b, s, d)


def kernel(x_prompt, x_sample, mem_prompt, mem_sample, norm_mix_w, w_in, lambda_q1, lambda_k1, lambda_q2,
           lambda_k2, attn_subln_w, conv_w, conv_b, a_log, dt_bias, d_skip, ssm_norm_w, w_attn_branch,
           w_ssm_branch, w_out, norm_cross_w, norm_mem_w, w_xq, w_xkv, w_xo, norm_ffn_w, w_router, b_router,
           w_gate, w_up, w_down, norm_final_w):
    p = dict(norm_mix_w=norm_mix_w, w_in=w_in, lambda_q1=lambda_q1, lambda_k1=lambda_k1, lambda_q2=lambda_q2,
             lambda_k2=lambda_k2, attn_subln_w=attn_subln_w, conv_w=conv_w, conv_b=conv_b, a_log=a_log,
             dt_bias=dt_bias, d_skip=d_skip, ssm_norm_w=ssm_norm_w, w_attn_branch=w_attn_branch,
             w_ssm_branch=w_ssm_branch, w_out=w_out, norm_cross_w=norm_cross_w, norm_mem_w=norm_mem_w,
             w_xq=w_xq, w_xkv=w_xkv, w_xo=w_xo, norm_ffn_w=norm_ffn_w, w_router=w_router, b_router=b_router,
             w_gate=w_gate, w_up=w_up, w_down=w_down)
    layers = [_prep_layer(l, p) for l in range(w_in.shape[0])]
    y_prompt = _trunk(x_prompt, mem_prompt, layers, norm_final_w)
    y_sample = _trunk(x_sample, mem_sample, layers, norm_final_w)
    return (y_prompt, y_sample)
```

```python
import functools
import math

import jax
import jax.numpy as jnp
from jax import lax
from jax.experimental import pallas as pl
from jax.experimental.pallas import tpu as pltpu

F32 = jnp.float32
BF16 = jnp.bfloat16
I32 = jnp.int32

D_MODEL = 1024
ATTN_HEADS = 8
ATTN_HEAD_DIM = 64
ATTN_V_DIM = 128
SSM_HEADS = 32
SSM_HEAD_DIM = 64
SSM_GROUPS = 4
D_STATE = 128
D_INNER = 2048
D_CONV = 5
XBC_DIM = 3072
CHUNK = 128
MEM_TOKENS = 256
X_HEADS = 4
X_HEAD_DIM = 128
X_WIDTH = 512
N_EXPERTS = 16
CAPACITY_FACTOR = 2
D_FF = 2816
EPS = 1e-6
SUBLN_EPS = 1e-5

C_Z = 0
C_GATE = 2048
C_XBC = 4096
C_Q = 7168
C_K = 8192
PROJ_COLS = 9216

LANE = 128
BF16_SUBLANE = 16
VMEM_LIMIT = 56 * 1024 * 1024


def _cparams(sem, vmem=None):
    return pltpu.CompilerParams(dimension_semantics=sem, vmem_limit_bytes=vmem)


def _rms(x, w, eps):
    return x * lax.rsqrt(jnp.mean(x * x, axis=-1, keepdims=True) + eps) * w


def _softplus(x):
    return jnp.maximum(x, 0.0) + jnp.log(1.0 + jnp.exp(-jnp.abs(x)))


def _silu(x):
    return x * (1.0 / (1.0 + jnp.exp(-x)))


def _norm_matmul_kernel(x_ref, nw_ref, w_ref, o_ref, xn_ref):
    @pl.when(pl.program_id(1) == 0)
    def _():
        xn_ref[...] = _rms(x_ref[...], nw_ref[...], EPS).astype(xn_ref.dtype)

    o_ref[...] = jnp.dot(xn_ref[...], w_ref[...], preferred_element_type=F32).astype(o_ref.dtype)


def norm_matmul(x, nw, w, out_dtype, tm, tn):
    t, d = x.shape
    n = w.shape[1]
    return pl.pallas_call(
        _norm_matmul_kernel,
        grid=(t // tm, n // tn),
        in_specs=[
            pl.BlockSpec((tm, d), lambda i, j: (i, 0)),
            pl.BlockSpec((1, d), lambda i, j: (0, 0)),
            pl.BlockSpec((d, tn), lambda i, j: (0, j)),
        ],
        out_specs=pl.BlockSpec((tm, tn), lambda i, j: (i, j)),
        out_shape=jax.ShapeDtypeStruct((t, n), out_dtype),
        scratch_shapes=[pltpu.VMEM((tm, d), BF16)],
        compiler_params=_cparams(("parallel", "arbitrary"), VMEM_LIMIT),
        name="norm_matmul",
    )(x, nw.reshape(1, d), w)


ONES_ROWS = BF16_SUBLANE
SKIP_MARGIN = 28.0
POS_SPLIT = 32


FIXED_REF_MAX_RANGE = 60.0


def _attn_kernel(scal_ref, q_ref, k_ref, vt_ref, w_ref, o_ref, qa_s, kf_s, m_s, acc_s, kn_s, ref_s, *, tq, tk, seq):
    h = pl.program_id(1)
    qi = pl.program_id(2)
    slope = scal_ref[h]
    nk = seq // tk
    ndiag = tq // tk
    t0 = qi * tq
    dlo = qi * ndiag
    half = ATTN_HEAD_DIM
    shift = POS_SPLIT.bit_length() - 1
    nt = (((1,), (1,)), ((), ()))

    lane_k = lax.broadcasted_iota(I32, (tk, LANE), 1)

    @pl.when(qi == 0)
    def _():
        def body(i, carry):
            kc = k_ref[pl.ds(pl.multiple_of(i * tk, tk), tk), :].astype(F32)
            sq = kc * kc
            n0 = jnp.max(jnp.sum(jnp.where(lane_k < half, sq, 0.0), axis=1, keepdims=True), axis=0, keepdims=True)
            n1 = jnp.max(jnp.sum(jnp.where(lane_k >= half, sq, 0.0), axis=1, keepdims=True), axis=0, keepdims=True)
            return jnp.maximum(carry[0], n0), jnp.maximum(carry[1], n1)

        z = jnp.zeros((1, 1), F32)
        n0, n1 = lax.fori_loop(0, nk, body, (z, z))
        kn_s[0] = jnp.broadcast_to(n0, (8, LANE))
        kn_s[1] = jnp.broadcast_to(n1, (8, LANE))

    q = q_ref[...].astype(F32) * (ATTN_HEAD_DIM ** -0.5)
    lane_q = lax.broadcasted_iota(I32, (tq, LANE), 1)
    tl = lax.broadcasted_iota(I32, (tq, LANE), 0)
    fa = (tl >> shift).astype(F32) * (-float(POS_SPLIT) * slope)
    fb = (tl & (POS_SPLIT - 1)).astype(F32) * (-slope)
    qsq = q * q
    sl = lax.broadcasted_iota(I32, (tk, LANE), 0)
    ka = (sl >> shift).astype(F32) * (float(POS_SPLIT) * slope)
    kb = (sl & (POS_SPLIT - 1)).astype(F32) * slope
    bounds = []
    for c in range(2):
        base = half if c == 0 else 0
        data = (lane_q < half) if c == 0 else (lane_q >= half)
        feat = jnp.where(lane_q == base, fa, jnp.where(lane_q == base + 1, fb,
                         jnp.where(lane_q == base + 2, 1.0, jnp.where(lane_q == base + 3, 1.0, 0.0))))
        qa_s[c] = jnp.where(data, q, feat).astype(BF16)
        pick = jnp.where(lax.broadcasted_iota(I32, (8, LANE), 1) // half == c, 1.0, 0.0)
        qn_row = lax.dot_general(pick, qsq, nt, preferred_element_type=F32,
                                 precision=lax.Precision.HIGHEST)[0:1, :]
        ref_c = jnp.sqrt(qn_row * kn_s[c][0:1, 0:1]) * 1.02
        ref_s[c] = ref_c
        bounds.append(jnp.max(ref_c, axis=1, keepdims=True))
        fl = jnp.where(lane_k == base, 1.0, jnp.where(lane_k == base + 1, 1.0,
                       jnp.where(lane_k == base + 2, ka, jnp.where(lane_k == base + 3, kb, 0.0))))
        kf_s[2 * c] = fl.astype(BF16)
        kf_s[2 * c + 1] = (-fl).astype(BF16)

    bmax = jnp.maximum(bounds[0], bounds[1])
    dskip = jnp.minimum((2.0 * bmax + (math.log(seq) + SKIP_MARGIN)) / slope, 2.0 * seq)
    t0f = jnp.full((1, 1), t0, I32).astype(F32)
    dlof = jnp.full((1, 1), dlo, I32).astype(F32)
    klo_f = jnp.clip(jnp.floor((t0f + 1.0 - dskip) / tk), 0.0, dlof)
    khi_f = jnp.clip(jnp.ceil((dskip + t0f + (tq - 1.0)) / tk), dlof + ndiag, float(nk))
    klo = jnp.max(klo_f).astype(I32)
    khi = jnp.max(khi_f).astype(I32)
    fixed_ref = jnp.max(jnp.where(2.0 * bmax <= FIXED_REF_MAX_RANGE, 1.0, 0.0)).astype(I32)

    m_s[...] = jnp.full(m_s.shape, -jnp.inf, F32)
    acc_s[...] = jnp.zeros(acc_s.shape, F32)
    ones = jnp.ones((ONES_ROWS, tk), BF16)

    def block(ki, side, fixed):
        s0 = pl.multiple_of(ki * tk, tk)
        kblk = k_ref[pl.ds(s0, tk), :]
        vaug = jnp.concatenate([vt_ref[:, pl.ds(s0, tk)], ones], axis=0)
        off = jnp.full((1, tq), t0 - s0, I32).astype(F32) * slope
        cblk = -off if side == 'R' else off
        if side == 'D':
            rel = (lax.broadcasted_iota(I32, (tk, tq), 0) - lax.broadcasted_iota(I32, (tk, tq), 1)
                   + (s0 - t0)).astype(F32)
            corr = jnp.maximum(rel, 0.0) * (-2.0 * slope)
        for c in range(2):
            data = (lane_k < half) if c == 0 else (lane_k >= half)
            kaug = jnp.where(data, kblk, kf_s[2 * c + (1 if side == 'R' else 0)])
            st = lax.dot_general(kaug, qa_s[c], nt, preferred_element_type=F32)
            if side == 'D':
                st = st + corr
            if fixed:
                p = jnp.exp(st - (ref_s[c] + cblk)).astype(BF16)
                acc_s[c] += jnp.dot(vaug, p, preferred_element_type=F32)
            else:
                m_prev = m_s[c]
                m_new = jnp.maximum(m_prev, jnp.max(st, axis=0, keepdims=True) - cblk)
                p = jnp.exp(st - (m_new + cblk)).astype(BF16)
                alpha = jnp.exp(m_prev - m_new)
                acc_s[c] = acc_s[c] * alpha + jnp.dot(vaug, p, preferred_element_type=F32)
                m_s[c] = m_new

    def sweep(fixed):
        def run(side):
            def body(ki, carry):
                block(ki, side, fixed)
                return carry
            return body

        lax.fori_loop(klo, dlo, run('L'), 0)
        for d in range(ndiag):
            block(dlo + d, 'D', fixed)
        lax.fori_loop(dlo + ndiag, khi, run('R'), 0)

    @pl.when(fixed_ref == 1)
    def _():
        sweep(True)

    @pl.when(fixed_ref == 0)
    def _():
        sweep(False)

    lam = scal_ref[ATTN_HEADS]
    post = scal_ref[ATTN_HEADS + 1]
    a0 = acc_s[0]
    a1 = acc_s[1]
    o = a0[0:ATTN_V_DIM] / a0[ATTN_V_DIM:ATTN_V_DIM + 1] - lam * (a1[0:ATTN_V_DIM] / a1[ATTN_V_DIM:ATTN_V_DIM + 1])
    o = o * lax.rsqrt(jnp.mean(o * o, axis=0, keepdims=True) + SUBLN_EPS) * (w_ref[...] * post)
    o_ref[...] = o.T.astype(o_ref.dtype)


def diff_attention(proj3, vt, scal, subln_w, tq, tk):
    b, s, _ = proj3.shape
    kq, kk = C_Q // LANE, C_K // LANE
    return pl.pallas_call(
        functools.partial(_attn_kernel, tq=tq, tk=tk, seq=s),
        grid=(b, ATTN_HEADS, s // tq),
        in_specs=[
            pl.BlockSpec(memory_space=pltpu.SMEM),
            pl.BlockSpec((None, tq, LANE), lambda bi, h, qi: (bi, qi, kq + h)),
            pl.BlockSpec((None, s, LANE), lambda bi, h, qi: (bi, 0, kk + h)),
            pl.BlockSpec((ATTN_V_DIM, s), lambda bi, h, qi: (h, bi)),
            pl.BlockSpec((ATTN_V_DIM, 1), lambda bi, h, qi: (0, 0)),
        ],
        out_specs=pl.BlockSpec((None, tq, LANE), lambda bi, h, qi: (bi, qi, h)),
        out_shape=jax.ShapeDtypeStruct((b, s, ATTN_HEADS * ATTN_V_DIM), BF16),
        scratch_shapes=[
            pltpu.VMEM((2, tq, LANE), BF16),
            pltpu.VMEM((4, tk, LANE), BF16),
            pltpu.VMEM((2, 1, tq), F32),
            pltpu.VMEM((2, ATTN_V_DIM + ONES_ROWS, tq), F32),
            pltpu.VMEM((2, 8, LANE), F32),
            pltpu.VMEM((2, 1, tq), F32),
        ],
        compiler_params=_cparams(("parallel", "parallel", "arbitrary"), VMEM_LIMIT),
        name="diff_attention",
    )(scal, proj3, proj3, vt, subln_w.reshape(ATTN_V_DIM, 1))


def _proj_t_kernel(x_ref, nw_ref, wvt_ref, wdt_ref, wdtt_ref, vt_ref, dt_ref, dtt_ref):
    xn = _rms(x_ref[...], nw_ref[...], EPS).astype(BF16)
    nt = (((1,), (1,)), ((), ()))
    vt_ref[...] = lax.dot_general(wvt_ref[...], xn, nt, preferred_element_type=F32).astype(vt_ref.dtype)
    dt_ref[...] = jnp.dot(xn, wdt_ref[...], preferred_element_type=F32)
    dtt_ref[...] = lax.dot_general(wdtt_ref[...], xn, nt, preferred_element_type=F32)


def proj_transposed(x, nw, wvt, wdt, wdtt, tm):
    t, d = x.shape
    nv = wvt.shape[0]
    const = lambda i: (0, 0)
    return pl.pallas_call(
        _proj_t_kernel,
        grid=(t // tm,),
        in_specs=[
            pl.BlockSpec((tm, d), lambda i: (i, 0)),
            pl.BlockSpec((1, d), const),
            pl.BlockSpec((nv, d), const),
            pl.BlockSpec((d, LANE), const),
            pl.BlockSpec((LANE, d), const),
        ],
        out_specs=[pl.BlockSpec((nv, tm), lambda i: (0, i)), pl.BlockSpec((tm, LANE), lambda i: (i, 0)),
                   pl.BlockSpec((LANE, tm), lambda i: (0, i))],
        out_shape=[jax.ShapeDtypeStruct((nv, t), BF16), jax.ShapeDtypeStruct((t, LANE), F32),
                   jax.ShapeDtypeStruct((LANE, t), F32)],
        compiler_params=_cparams(("parallel",), VMEM_LIMIT),
        name="proj_transposed",
    )(x, nw.reshape(1, d), wvt, wdt, wdtt)


HALO = 16


def _conv_kernel(prev_ref, cur_ref, next_ref, w_ref, b_ref, o_ref, ext_ref, *, tm):
    i = pl.program_id(1)
    last = pl.num_programs(1) - 1
    prev = prev_ref[...].astype(F32)
    nxt = next_ref[...].astype(F32)
    ext_ref[pl.ds(0, HALO), :] = jnp.where(i > 0, prev, 0.0)
    ext_ref[pl.ds(HALO, tm), :] = cur_ref[...].astype(F32)
    ext_ref[pl.ds(HALO + tm, HALO), :] = jnp.where(i < last, nxt, 0.0)
    w = w_ref[...]
    acc = jnp.zeros(o_ref.shape, F32) + b_ref[...]
    for j in range(D_CONV):
        acc = acc + ext_ref[pl.ds(HALO - D_CONV // 2 + j, tm), :] * w[j:j + 1, :]
    o_ref[...] = _silu(acc).astype(o_ref.dtype)


def conv_silu(proj3, conv_w, conv_b, tm, tc):
    b, s, _ = proj3.shape
    c0 = C_XBC // tc
    hb = tm // HALO
    nh = s // HALO
    return pl.pallas_call(
        functools.partial(_conv_kernel, tm=tm),
        grid=(b, s // tm, XBC_DIM // tc),
        in_specs=[
            pl.BlockSpec((None, HALO, tc), lambda bi, i, j: (bi, jnp.maximum(i * hb - 1, 0), c0 + j)),
            pl.BlockSpec((None, tm, tc), lambda bi, i, j: (bi, i, c0 + j)),
            pl.BlockSpec((None, HALO, tc), lambda bi, i, j: (bi, jnp.minimum((i + 1) * hb, nh - 1), c0 + j)),
            pl.BlockSpec((D_CONV, tc), lambda bi, i, j: (0, j)),
            pl.BlockSpec((1, tc), lambda bi, i, j: (0, j)),
        ],
        out_specs=pl.BlockSpec((None, tm, tc), lambda bi, i, j: (bi, i, j)),
        out_shape=jax.ShapeDtypeStruct((b, s, XBC_DIM), BF16),
        scratch_shapes=[pltpu.VMEM((tm + 2 * HALO, tc), F32)],
        compiler_params=_cparams(("parallel", "parallel", "parallel"), VMEM_LIMIT),
        name="conv_silu",
    )(proj3, proj3, proj3, conv_w, conv_b.reshape(1, XBC_DIM))


def _expand_heads(a):
    rows = a.shape[0]
    lane = lax.broadcasted_iota(I32, (rows, LANE), 1)
    parts = []
    for hp in range(SSM_HEADS // 2):
        lo = a[:, 2 * hp:2 * hp + 1]
        hi = a[:, 2 * hp + 1:2 * hp + 2]
        parts.append(jnp.where(lane < SSM_HEAD_DIM, lo, hi))
    return jnp.concatenate(parts, axis=1)


def _tri(n, fn):
    r = lax.broadcasted_iota(I32, (n, n), 0)
    c = lax.broadcasted_iota(I32, (n, n), 1)
    return jnp.where(fn(r, c), 1.0, 0.0).astype(F32)


def _dot_hi(a, b):
    return jnp.dot(a, b, preferred_element_type=F32, precision=lax.Precision.HIGHEST)


def _ssd_state_kernel(xf_ref, bf_ref, dtf_ref, xb_ref, bb_ref, dtb_ref, par_ref, hf_ref, hb_ref, sf_s, sb_s):
    j = pl.program_id(1)

    @pl.when(j == 0)
    def _():
        sf_s[...] = jnp.zeros(sf_s.shape, F32)
        sb_s[...] = jnp.zeros(sb_s.shape, F32)

    hf_ref[...] = sf_s[...].astype(hf_ref.dtype)
    hb_ref[...] = sb_s[...].astype(hb_ref.dtype)

    par = par_ref[...]
    lower_incl = _tri(CHUNK, lambda r, c: c <= r)
    lower_strict = _tri(CHUNK, lambda r, c: c < r)

    def one_direction(x_ref, b_ref, dt_ref, s_ref, col0, backward):
        raw = dt_ref[...][:, col0:col0 + SSM_HEADS]
        dt = _softplus(raw + par[0:1, col0:col0 + SSM_HEADS])
        a = dt * par[1:2, col0:col0 + SSM_HEADS]
        if backward:
            excl = _dot_hi(lower_strict, a)
            wgt = dt * jnp.exp(excl)
            total = excl[CHUNK - 1:CHUNK, :] + a[CHUNK - 1:CHUNK, :]
        else:
            cum = _dot_hi(lower_incl, a)
            total = cum[CHUNK - 1:CHUNK, :]
            wgt = dt * jnp.exp(total - cum)
        xw = (x_ref[...].astype(F32) * _expand_heads(wgt)).astype(BF16)
        dec = _expand_heads(jnp.exp(jnp.broadcast_to(total, (8, SSM_HEADS))))[0:1, :]
        bmat = b_ref[...]
        gw = D_INNER // SSM_GROUPS
        for g in range(SSM_GROUPS):
            contrib = lax.dot_general(bmat[:, g * D_STATE:(g + 1) * D_STATE], xw[:, g * gw:(g + 1) * gw],
                                      (((0,), (0,)), ((), ())), preferred_element_type=F32)
            s_ref[g] = s_ref[g] * dec[:, g * gw:(g + 1) * gw] + contrib

    one_direction(xf_ref, bf_ref, dtf_ref, sf_s, 0, False)
    one_direction(xb_ref, bb_ref, dtb_ref, sb_s, SSM_HEADS, True)


def ssd_states(xbc, dt3, par):
    b, s, _ = xbc.shape
    nc = s // CHUNK
    gw = D_INNER // SSM_GROUPS
    xblk = D_INNER // D_INNER
    bcol = D_INNER // (SSM_GROUPS * D_STATE)
    hshape = jax.ShapeDtypeStruct((b, nc, SSM_GROUPS, D_STATE, gw), BF16)
    hspec_f = pl.BlockSpec((None, None, SSM_GROUPS, D_STATE, gw), lambda bi, j: (bi, j, 0, 0, 0))
    hspec_b = pl.BlockSpec((None, None, SSM_GROUPS, D_STATE, gw), lambda bi, j: (bi, nc - 1 - j, 0, 0, 0))
    del xblk
    return pl.pallas_call(
        _ssd_state_kernel,
        grid=(b, nc),
        in_specs=[
            pl.BlockSpec((None, CHUNK, D_INNER), lambda bi, j: (bi, j, 0)),
            pl.BlockSpec((None, CHUNK, SSM_GROUPS * D_STATE), lambda bi, j: (bi, j, bcol)),
            pl.BlockSpec((None, CHUNK, LANE), lambda bi, j: (bi, j, 0)),
            pl.BlockSpec((None, CHUNK, D_INNER), lambda bi, j: (bi, nc - 1 - j, 0)),
            pl.BlockSpec((None, CHUNK, SSM_GROUPS * D_STATE), lambda bi, j: (bi, nc - 1 - j, bcol)),
            pl.BlockSpec((None, CHUNK, LANE), lambda bi, j: (bi, nc - 1 - j, 0)),
            pl.BlockSpec((8, LANE), lambda bi, j: (0, 0)),
        ],
        out_specs=[hspec_f, hspec_b],
        out_shape=[hshape, hshape],
        scratch_shapes=[pltpu.VMEM((SSM_GROUPS, D_STATE, gw), F32), pltpu.VMEM((SSM_GROUPS, D_STATE, gw), F32)],
        compiler_params=_cparams(("parallel", "arbitrary"), VMEM_LIMIT),
        name="ssd_states",
    )(xbc, xbc, dt3, xbc, xbc, dt3, par)


def _ssd_out_kernel(x_ref, b_ref, c_ref, z_ref, dt_ref, dtt_ref, par_ref, part_ref, dsk_ref, nw_ref,
                    hf_ref, hb_ref, o_ref, y_s):
    par = par_ref[...]
    part = part_ref[...]
    lower_incl = _tri(CHUNK, lambda r, c: c <= r)
    lower_strict = _tri(CHUNK, lambda r, c: c < r)
    upper_incl = _tri(CHUNK, lambda r, c: r <= c)
    upper_strict = _tri(CHUNK, lambda r, c: r < c)

    dtc = _softplus(dt_ref[...][:, 0:2 * SSM_HEADS] + par[0:1, 0:2 * SSM_HEADS])
    ac = dtc * par[1:2, 0:2 * SSM_HEADS]
    cum_c = _dot_hi(lower_incl, ac)
    excl_c = _dot_hi(lower_strict, ac)
    dtr = _softplus(dtt_ref[...][0:2 * SSM_HEADS, :] + part[0:2 * SSM_HEADS, 0:1])
    ar = dtr * part[0:2 * SSM_HEADS, 1:2]
    cum_r = _dot_hi(ar, upper_incl)
    excl_r = _dot_hi(ar, upper_strict)

    cumf_c = cum_c[:, 0:SSM_HEADS]
    exclb_c = excl_c[:, SSM_HEADS:2 * SSM_HEADS]
    totb = cum_c[CHUNK - 1:CHUNK, SSM_HEADS:2 * SSM_HEADS]
    cumf_r = cum_r[0:SSM_HEADS, :]
    exclb_r = excl_r[SSM_HEADS:2 * SSM_HEADS, :]
    dtf_r = dtr[0:SSM_HEADS, :]
    dtb_r = dtr[SSM_HEADS:2 * SSM_HEADS, :]

    row = lax.broadcasted_iota(I32, (CHUNK, CHUNK), 0)
    col = lax.broadcasted_iota(I32, (CHUNK, CHUNK), 1)
    low = col < row
    diag = col == row

    x = x_ref[...]
    bm = b_ref[...]
    cm = c_ref[...]
    gw = D_INNER // SSM_GROUPS
    hpg = SSM_HEADS // SSM_GROUPS
    scale_f = _expand_heads(jnp.exp(cumf_c))
    scale_b = _expand_heads(jnp.exp(totb - exclb_c))
    for g in range(SSM_GROUPS):
        cg = cm[:, g * D_STATE:(g + 1) * D_STATE]
        bg = bm[:, g * D_STATE:(g + 1) * D_STATE]
        cb = lax.dot_general(cg, bg, (((1,), (1,)), ((), ())), preferred_element_type=F32)
        inter_f = jnp.dot(cg, hf_ref[g], preferred_element_type=F32)
        inter_b = jnp.dot(cg, hb_ref[g], preferred_element_type=F32)
        sl = slice(g * gw, (g + 1) * gw)
        y_s[:, sl] = inter_f * scale_f[:, sl] + inter_b * scale_b[:, sl]
        for hh in range(hpg):
            h = g * hpg + hh
            arg = jnp.minimum(cumf_c[:, h:h + 1] - cumf_r[h:h + 1, :],
                              exclb_r[h:h + 1, :] - exclb_c[:, h:h + 1])
            wdt = jnp.where(low, dtf_r[h:h + 1, :], dtb_r[h:h + 1, :])
            wdt = jnp.where(diag, dtf_r[h:h + 1, :] + dtb_r[h:h + 1, :], wdt)
            gmat = (cb * jnp.exp(arg) * wdt).astype(BF16)
            hs = slice(h * SSM_HEAD_DIM, (h + 1) * SSM_HEAD_DIM)
            y_s[:, hs] += jnp.dot(gmat, x[:, hs], preferred_element_type=F32)

    y = y_s[...] + x.astype(F32) * dsk_ref[...]
    yf = y * _silu(z_ref[...].astype(F32))
    nw = nw_ref[...]
    for g in range(SSM_GROUPS):
        sl = slice(g * gw, (g + 1) * gw)
        yg = yf[:, sl]
        o_ref[:, sl] = (yg * lax.rsqrt(jnp.mean(yg * yg, axis=-1, keepdims=True) + EPS)
                        * nw[:, sl]).astype(o_ref.dtype)


def ssd_output(xbc, proj3, dt3, dtt3, par, part, dskip_x, nw, hf, hb):
    b, s, _ = xbc.shape
    nc = s // CHUNK
    gw = D_INNER // SSM_GROUPS
    gs = SSM_GROUPS * D_STATE
    hspec = pl.BlockSpec((None, None, SSM_GROUPS, D_STATE, gw), lambda bi, j: (bi, j, 0, 0, 0))
    return pl.pallas_call(
        _ssd_out_kernel,
        grid=(b, nc),
        in_specs=[
            pl.BlockSpec((None, CHUNK, D_INNER), lambda bi, j: (bi, j, 0)),
            pl.BlockSpec((None, CHUNK, gs), lambda bi, j: (bi, j, D_INNER // gs)),
            pl.BlockSpec((None, CHUNK, gs), lambda bi, j: (bi, j, D_INNER // gs + 1)),
            pl.BlockSpec((None, CHUNK, D_INNER), lambda bi, j: (bi, j, C_Z // D_INNER)),
            pl.BlockSpec((None, CHUNK, LANE), lambda bi, j: (bi, j, 0)),
            pl.BlockSpec((LANE, CHUNK), lambda bi, j: (0, bi * nc + j)),
            pl.BlockSpec((8, LANE), lambda bi, j: (0, 0)),
            pl.BlockSpec((LANE, 8), lambda bi, j: (0, 0)),
            pl.BlockSpec((1, D_INNER), lambda bi, j: (0, 0)),
            pl.BlockSpec((1, D_INNER), lambda bi, j: (0, 0)),
            hspec, hspec,
        ],
        out_specs=pl.BlockSpec((None, CHUNK, D_INNER), lambda bi, j: (bi, j, 0)),
        out_shape=jax.ShapeDtypeStruct((b, s, D_INNER), BF16),
        scratch_shapes=[pltpu.VMEM((CHUNK, D_INNER), F32)],
        compiler_params=_cparams(("parallel", "parallel"), VMEM_LIMIT),
        name="ssd_output",
    )(xbc, xbc, xbc, proj3, dt3, dtt3, par, part, dskip_x, nw, hf, hb)


def _merge_kernel(x_ref, a_ref, s_ref, g0_ref, g1_ref, wa_ref, ws_ref, wo_ref, o_ref):
    ya = jnp.dot(a_ref[...], wa_ref[...], preferred_element_type=F32)
    ys = jnp.dot(s_ref[...], ws_ref[...], preferred_element_type=F32)
    g0 = 1.0 / (1.0 + jnp.exp(-g0_ref[...].astype(F32)))
    g1 = 1.0 / (1.0 + jnp.exp(-g1_ref[...].astype(F32)))
    merged = (g0 * ya + g1 * ys).astype(BF16)
    o_ref[...] = x_ref[...] + jnp.dot(merged, wo_ref[...], preferred_element_type=F32)


def merge_out(x, attn, ssm, proj, wa, ws, wo, tm):
    t, d = x.shape
    gb = C_GATE // d
    const = lambda i: (0, 0)
    return pl.pallas_call(
        _merge_kernel,
        grid=(t // tm,),
        in_specs=[
            pl.BlockSpec((tm, d), lambda i: (i, 0)),
            pl.BlockSpec((tm, d), lambda i: (i, 0)),
            pl.BlockSpec((tm, D_INNER), lambda i: (i, 0)),
            pl.BlockSpec((tm, d), lambda i: (i, gb)),
            pl.BlockSpec((tm, d), lambda i: (i, gb + 1)),
            pl.BlockSpec((d, d), const),
            pl.BlockSpec((D_INNER, d), const),
            pl.BlockSpec((d, d), const),
        ],
        out_specs=pl.BlockSpec((tm, d), lambda i: (i, 0)),
        out_shape=jax.ShapeDtypeStruct((t, d), F32),
        compiler_params=_cparams(("parallel",), VMEM_LIMIT),
        name="merge_out",
    )(x, attn, ssm, proj, proj, wa, ws, wo)


def _cross_kernel(x_ref, nw_ref, wq_ref, kv_ref, wo_ref, o_ref):
    x = x_ref[...]
    h = _rms(x, nw_ref[...], EPS).astype(BF16)
    q = jnp.dot(h, wq_ref[...], preferred_element_type=F32).astype(BF16)
    kv = kv_ref[...]
    outs = []
    for hd in range(X_HEADS):
        qh = q[:, hd * X_HEAD_DIM:(hd + 1) * X_HEAD_DIM]
        kh = kv[:, hd * X_HEAD_DIM:(hd + 1) * X_HEAD_DIM]
        vh = kv[:, X_WIDTH + hd * X_HEAD_DIM:X_WIDTH + (hd + 1) * X_HEAD_DIM]
        s = lax.dot_general(qh, kh, (((1,), (1,)), ((), ())), preferred_element_type=F32) * (X_HEAD_DIM ** -0.5)
        s = s - jnp.max(s, axis=-1, keepdims=True)
        p = jnp.exp(s)
        p = (p / jnp.sum(p, axis=-1, keepdims=True)).astype(BF16)
        outs.append(jnp.dot(p, vh, preferred_element_type=F32).astype(BF16))
    o = jnp.concatenate(outs, axis=1)
    o_ref[...] = x + jnp.dot(o, wo_ref[...], preferred_element_type=F32)


def cross_attention(x, nw, wq, kv, wo, seq, tm):
    t, d = x.shape
    per_seq = seq // tm
    const = lambda i: (0, 0)
    return pl.pallas_call(
        _cross_kernel,
        grid=(t // tm,),
        in_specs=[
            pl.BlockSpec((tm, d), lambda i: (i, 0)),
            pl.BlockSpec((1, d), const),
            pl.BlockSpec((d, X_WIDTH), const),
            pl.BlockSpec((None, MEM_TOKENS, 2 * X_WIDTH), lambda i: (i // per_seq, 0, 0)),
            pl.BlockSpec((X_WIDTH, d), const),
        ],
        out_specs=pl.BlockSpec((tm, d), lambda i: (i, 0)),
        out_shape=jax.ShapeDtypeStruct((t, d), F32),
        compiler_params=_cparams(("parallel",), VMEM_LIMIT),
        name="cross_attention",
    )(x, nw.reshape(1, d), wq, kv, wo)


def _router_kernel(x_ref, nw_ref, wrt_ref, br_ref, hb_ref, aff_ref):
    h = _rms(x_ref[...], nw_ref[...], EPS)
    hb_ref[...] = h.astype(hb_ref.dtype)
    logits = lax.dot_general(wrt_ref[...], h, (((1,), (1,)), ((), ())), preferred_element_type=F32,
                             precision=lax.Precision.HIGHEST) + br_ref[...]
    e = jnp.exp(logits - jnp.max(logits, axis=0, keepdims=True))
    aff_ref[...] = e / jnp.sum(e, axis=0, keepdims=True)


def router(x, nw, wrt, br, tm):
    t, d = x.shape
    return pl.pallas_call(
        _router_kernel,
        grid=(t // tm,),
        in_specs=[
            pl.BlockSpec((tm, d), lambda i: (i, 0)),
            pl.BlockSpec((1, d), lambda i: (0, 0)),
            pl.BlockSpec((N_EXPERTS, d), lambda i: (0, 0)),
            pl.BlockSpec((N_EXPERTS, 1), lambda i: (0, 0)),
        ],
        out_specs=[pl.BlockSpec((tm, d), lambda i: (i, 0)), pl.BlockSpec((N_EXPERTS, tm), lambda i: (0, i))],
        out_shape=[jax.ShapeDtypeStruct((t, d), BF16), jax.ShapeDtypeStruct((N_EXPERTS, t), F32)],
        compiler_params=_cparams(("parallel",), VMEM_LIMIT),
        name="router",
    )(x, nw.reshape(1, d), wrt, br.reshape(N_EXPERTS, 1))


TB = 256


def _topk_kernel(aff_ref, pos_ref, gate_ref, roff_ref, *, cap):
    a = aff_ref[...]
    nr = a.shape[1]
    bits = pltpu.bitcast(a, I32)

    def count(mask):
        c = jnp.sum(jnp.where(mask, 1.0, 0.0), axis=2, keepdims=True)
        return jnp.sum(c, axis=1, keepdims=True)

    def body(i, thr):
        cand = thr | jnp.left_shift(jnp.int32(1), 30 - i)
        return jnp.where(count(bits >= cand) >= float(cap), cand, thr)

    thr = lax.fori_loop(0, 31, body, jnp.zeros((N_EXPERTS, 1, 1), I32))
    gt = bits > thr
    eq = bits == thr
    need = float(cap) - count(gt)

    ustrict = _tri(TB, lambda r, c: r < c).astype(BF16)
    ones = jnp.ones((TB, TB), BF16)
    lstrict = _tri(nr, lambda r, c: c < r).astype(BF16)

    def excl_prefix(m):
        mb = m.astype(BF16)
        within = jnp.dot(mb, ustrict, preferred_element_type=F32)
        rowsum = jnp.dot(mb, ones, preferred_element_type=F32)
        rowoff = jnp.dot(lstrict, rowsum.astype(BF16), preferred_element_type=F32)
        return within + rowoff, rowoff

    for e in range(N_EXPERTS):
        eq_e = jnp.where(eq[e], 1.0, 0.0)
        pe, _ = excl_prefix(eq_e)
        keep = jnp.where(pe < need[e], eq_e, 0.0)
        sel = jnp.where(gt[e], 1.0, keep)
        ps, roff = excl_prefix(sel)
        chosen = sel > 0.5
        pos_ref[e] = jnp.where(chosen, ps.astype(I32), -1)
        gate_ref[e] = jnp.where(chosen, a[e], 0.0)
        roff_ref[e] = roff.astype(I32)


def topk_select(aff3, cap):
    e, r, tb = aff3.shape
    full = lambda: (0, 0, 0)
    return pl.pallas_call(
        functools.partial(_topk_kernel, cap=cap),
        grid=(),
        in_specs=[pl.BlockSpec((e, r, tb), full)],
        out_specs=[pl.BlockSpec((e, r, tb), full)] * 3,
        out_shape=[jax.ShapeDtypeStruct((e, r, tb), I32), jax.ShapeDtypeStruct((e, r, tb), F32),
                   jax.ShapeDtypeStruct((e, r, tb), I32)],
        compiler_params=pltpu.CompilerParams(vmem_limit_bytes=VMEM_LIMIT),
        name="topk_select",
    )(aff3)


GATHER_BUFS = 8


def _ffn_kernel(lo_ref, hi_ref, pos_ref, gate_ref, hb_hbm, wg_ref, wu_ref, wd_ref, y_ref,
                xbuf, sem, xc_s, g_s, *, ts, nj):
    e = pl.program_id(0)
    j = pl.program_id(1)
    lo = lo_ref[e * nj + j]
    hi = hi_ref[e * nj + j]
    xc_s[...] = jnp.zeros(xc_s.shape, F32)
    g_s[...] = jnp.zeros(g_s.shape, F32)

    def fetch(r, slot):
        return pltpu.make_async_copy(hb_hbm.at[pl.ds(r * TB, TB)], xbuf.at[slot], sem.at[slot])

    for i in range(GATHER_BUFS - 1):
        @pl.when(lo + i < hi)
        def _(i=i):
            fetch(lo + i, i).start()

    slot_ids = lax.broadcasted_iota(I32, (ts, TB), 0) + j * ts

    def body(r, carry):
        idx = r - lo
        slot = idx & (GATHER_BUFS - 1)
        fetch(r, slot).wait()

        @pl.when(r + (GATHER_BUFS - 1) < hi)
        def _():
            fetch(r + (GATHER_BUFS - 1), (idx + (GATHER_BUFS - 1)) & (GATHER_BUFS - 1)).start()

        hit = pos_ref[pl.ds(r, 1), :] == slot_ids
        sel = jnp.where(hit, 1.0, 0.0).astype(BF16)
        xc_s[...] += jnp.dot(sel, xbuf[slot], preferred_element_type=F32)
        g_s[...] += jnp.sum(jnp.where(hit, gate_ref[pl.ds(r, 1), :], 0.0), axis=1, keepdims=True)
        return carry

    lax.fori_loop(lo, hi, body, 0)

    xc = xc_s[...].astype(BF16)
    hg = jnp.dot(xc, wg_ref[...], preferred_element_type=F32)
    hu = jnp.dot(xc, wu_ref[...], preferred_element_type=F32)
    hid = (_silu(hg) * hu).astype(BF16)
    y = jnp.dot(hid, wd_ref[...], preferred_element_type=F32) * g_s[...]
    y_ref[...] = y.astype(y_ref.dtype)


def expert_ffn(lo, hi, pos, gate, hb, wg, wu, wd, cap, ts):
    e, r, tb = pos.shape
    d = hb.shape[1]
    nj = cap // ts
    grid_spec = pltpu.PrefetchScalarGridSpec(
        num_scalar_prefetch=2,
        grid=(e, nj),
        in_specs=[
            pl.BlockSpec((None, r, tb), lambda ei, j, lo_r, hi_r: (ei, 0, 0)),
            pl.BlockSpec((None, r, tb), lambda ei, j, lo_r, hi_r: (ei, 0, 0)),
            pl.BlockSpec(memory_space=pl.ANY),
            pl.BlockSpec((None, d, D_FF), lambda ei, j, lo_r, hi_r: (ei, 0, 0), pipeline_mode=pl.Buffered(1)),
            pl.BlockSpec((None, d, D_FF), lambda ei, j, lo_r, hi_r: (ei, 0, 0), pipeline_mode=pl.Buffered(1)),
            pl.BlockSpec((None, D_FF, d), lambda ei, j, lo_r, hi_r: (ei, 0, 0), pipeline_mode=pl.Buffered(1)),
        ],
        out_specs=pl.BlockSpec((None, ts, d), lambda ei, j, lo_r, hi_r: (ei, j, 0)),
        scratch_shapes=[
            pltpu.VMEM((GATHER_BUFS, TB, d), BF16),
            pltpu.SemaphoreType.DMA((GATHER_BUFS,)),
            pltpu.VMEM((ts, d), F32),
            pltpu.VMEM((ts, 1), F32),
        ],
    )
    return pl.pallas_call(
        functools.partial(_ffn_kernel, ts=ts, nj=nj),
        grid_spec=grid_spec,
        out_shape=jax.ShapeDtypeStruct((e, cap, d), BF16),
        compiler_params=_cparams(("arbitrary", "arbitrary"), VMEM_LIMIT),
        name="expert_ffn",
    )(lo, hi, pos, gate, hb, wg, wu, wd)


def _combine_kernel(start_ref, x_ref, post_ref, *rest, win):
    y_refs = rest[:N_EXPERTS]
    o_ref = rest[N_EXPERTS]
    r = pl.program_id(0)
    nr = pl.num_programs(0)
    post = post_ref[...]
    lane = lax.broadcasted_iota(I32, (TB, win), 1)
    acc = x_ref[...]
    for e in range(N_EXPERTS):
        rel = post[:, e:e + 1] - start_ref[e * nr + r]
        sel = jnp.where(rel == lane, 1.0, 0.0).astype(BF16)
        acc = acc + jnp.dot(sel, y_refs[e][...], preferred_element_type=F32)
    o_ref[...] = acc


def combine(x, post, start, y, win):
    t, d = x.shape
    e, cap, _ = y.shape
    nr = t // TB

    def y_spec(ei):
        return pl.BlockSpec((pl.Element(win), pl.Element(d)),
                            lambda r, st: (pl.multiple_of(ei * cap + st[ei * nr + r], BF16_SUBLANE), 0))

    grid_spec = pltpu.PrefetchScalarGridSpec(
        num_scalar_prefetch=1,
        grid=(nr,),
        in_specs=[pl.BlockSpec((TB, d), lambda r, st: (r, 0)),
                  pl.BlockSpec((TB, e), lambda r, st: (r, 0))] + [y_spec(ei) for ei in range(e)],
        out_specs=pl.BlockSpec((TB, d), lambda r, st: (r, 0)),
    )
    return pl.pallas_call(
        functools.partial(_combine_kernel, win=win),
        grid_spec=grid_spec,
        out_shape=jax.ShapeDtypeStruct((t, d), F32),
        compiler_params=_cparams(("arbitrary",), VMEM_LIMIT),
        name="combine",
    )(start, x, post, *([y.reshape(e * cap, d)] * e))


def _norm_kernel(x_ref, w_ref, o_ref):
    o_ref[...] = _rms(x_ref[...], w_ref[...], EPS)


def final_norm(x, w, tm):
    t, d = x.shape
    return pl.pallas_call(
        _norm_kernel,
        grid=(t // tm,),
        in_specs=[pl.BlockSpec((tm, d), lambda i: (i, 0)), pl.BlockSpec((1, d), lambda i: (0, 0))],
        out_specs=pl.BlockSpec((tm, d), lambda i: (i, 0)),
        out_shape=jax.ShapeDtypeStruct((t, d), F32),
        compiler_params=_cparams(("parallel",)),
        name="final_norm",
    )(x, w.reshape(1, d))


def _tiles(t, s):
    def fit(n, pref):
        while n % pref:
            pref //= 2
        return pref

    cap = max(1, CAPACITY_FACTOR * t // N_EXPERTS)
    return dict(
        tm_proj=fit(t, 1024), tn_proj=1024,
        tq=fit(s, 1024), tk=fit(s, 512),
        tm_conv=fit(s, 512), tc_conv=1024,
        tm_tok=fit(s, 512),
        cap=cap, ts=fit(cap, 256),
        win=TB + BF16_SUBLANE,
    )


def _prep_layer(l, p):
    w_in = p['w_in'][l]
    o_q, o_k, o_v, o_z, o_xbc, o_dt = 1024, 2048, 3072, 5120, 8192, 8256
    w_main = jnp.concatenate([w_in[:, o_v:o_z], w_in[:, o_dt:], w_in[:, o_z:o_xbc], w_in[:, :o_k]],
                             axis=1).astype(BF16)
    w_vt = w_in[:, o_k:o_v].T.astype(BF16)
    w_dt = jnp.pad(w_in[:, o_xbc:o_dt], ((0, 0), (0, LANE - 2 * SSM_HEADS))).astype(BF16)
    dt_bias = p['dt_bias'][l].reshape(-1).astype(F32)
    a_neg = -jnp.exp(p['a_log'][l].astype(F32)).reshape(-1)
    par = jnp.zeros((8, LANE), F32).at[0, :2 * SSM_HEADS].set(dt_bias).at[1, :2 * SSM_HEADS].set(a_neg)
    part = jnp.zeros((LANE, 8), F32).at[:2 * SSM_HEADS, 0].set(dt_bias).at[:2 * SSM_HEADS, 1].set(a_neg)
    lam_init = 0.8 - 0.6 * math.exp(-0.3 * l)
    lam = (jnp.exp(jnp.sum(p['lambda_q1'][l].astype(F32) * p['lambda_k1'][l].astype(F32)))
           - jnp.exp(jnp.sum(p['lambda_q2'][l].astype(F32) * p['lambda_k2'][l].astype(F32))) + lam_init)
    slopes = jnp.exp2(-8.0 * (jnp.arange(ATTN_HEADS, dtype=F32) + 1.0) / ATTN_HEADS)
    scal = jnp.concatenate([slopes, lam.reshape(1), jnp.full((1,), 1.0 - lam_init, F32)]).astype(F32)
    return dict(
        norm_mix=p['norm_mix_w'][l], w_main=w_main, w_vt=w_vt, w_dt=w_dt, w_dtt=w_dt.T, par=par, part=part,
        scal=scal,
        subln=p['attn_subln_w'][l].astype(F32),
        conv_w=p['conv_w'][l].astype(F32), conv_b=p['conv_b'][l].astype(F32),
        dskip_x=jnp.repeat(p['d_skip'][l].astype(F32), SSM_HEAD_DIM).reshape(1, D_INNER),
        ssm_nw=p['ssm_norm_w'][l].astype(F32).reshape(1, D_INNER),
        wa=p['w_attn_branch'][l].astype(BF16), ws=p['w_ssm_branch'][l].astype(BF16),
        wo=p['w_out'][l].astype(BF16),
        norm_cross=p['norm_cross_w'][l], norm_mem=p['norm_mem_w'][l],
        wxq=p['w_xq'][l].astype(BF16), wxkv=p['w_xkv'][l].astype(BF16), wxo=p['w_xo'][l].astype(BF16),
        norm_ffn=p['norm_ffn_w'][l], wrt=p['w_router'][l].astype(F32).T, br=p['b_router'][l].astype(F32),
        wg=p['w_gate'][l].astype(BF16), wu=p['w_up'][l].astype(BF16), wd=p['w_down'][l].astype(BF16),
    )


def _layer(x, mem, lw, b, s):
    t = b * s
    tl = _tiles(t, s)
    proj = norm_matmul(x, lw['norm_mix'], lw['w_main'], BF16, tl['tm_proj'], tl['tn_proj'])
    vt, dt, dtt = proj_transposed(x, lw['norm_mix'], lw['w_vt'], lw['w_dt'], lw['w_dtt'], tl['tm_tok'])
    proj3 = proj.reshape(b, s, PROJ_COLS)
    dt3 = dt.reshape(b, s, LANE)
    attn = diff_attention(proj3, vt, lw['scal'], lw['subln'], tl['tq'], tl['tk'])
    xbc = conv_silu(proj3, lw['conv_w'], lw['conv_b'], tl['tm_conv'], tl['tc_conv'])
    hf, hb = ssd_states(xbc, dt3, lw['par'])
    ssm = ssd_output(xbc, proj3, dt3, dtt, lw['par'], lw['part'], lw['dskip_x'], lw['ssm_nw'], hf, hb)
    x = merge_out(x, attn.reshape(t, D_MODEL), ssm.reshape(t, D_INNER), proj, lw['wa'], lw['ws'], lw['wo'],
                  tl['tm_tok'])
    nm = mem.shape[0] * mem.shape[1]
    kv = norm_matmul(mem.reshape(nm, D_MODEL), lw['norm_mem'], lw['wxkv'], BF16, min(nm, 512), 2 * X_WIDTH)
    x = cross_attention(x, lw['norm_cross'], lw['wxq'], kv.reshape(b, MEM_TOKENS, 2 * X_WIDTH), lw['wxo'], s,
                        tl['tm_tok'])
    cap, ts = tl['cap'], tl['ts']
    nr = t // TB
    hbf, aff = router(x, lw['norm_ffn'], lw['wrt'], lw['br'], tl['tm_tok'])
    pos, gate, roff = topk_select(aff.reshape(N_EXPERTS, nr, TB), cap)
    roff = roff[:, :, 0]
    rend = jnp.concatenate([roff[:, 1:], jnp.full((N_EXPERTS, 1), cap, I32)], axis=1)
    edges = jnp.arange(cap // ts, dtype=I32) * ts
    lo = jnp.sum(rend[:, None, :] <= edges[None, :, None], axis=2).astype(I32).reshape(-1)
    hi = jnp.sum(roff[:, None, :] < (edges + ts)[None, :, None], axis=2).astype(I32).reshape(-1)
    y = expert_ffn(lo, hi, pos, gate, hbf, lw['wg'], lw['wu'], lw['wd'], cap, ts)
    win = tl['win']
    start = jnp.minimum((roff // BF16_SUBLANE) * BF16_SUBLANE, cap - win).astype(I32).reshape(-1)
    post = jnp.transpose(pos.reshape(N_EXPERTS, t))
    return combine(x, post, start, y, win)


def _trunk(x, mem, layers, norm_final_w):
    b, s, d = x.shape
    xt = x.reshape(b * s, d)
    for lw in layers:
        xt = _layer(xt, mem, lw, b, s)
    return final_norm(xt, norm_final_w, 512).reshape(b, s, d)


def kernel(x_prompt, x_sample, mem_prompt, mem_sample, norm_mix_w, w_in, lambda_q1, lambda_k1, lambda_q2,
           lambda_k2, attn_subln_w, conv_w, conv_b, a_log, dt_bias, d_skip, ssm_norm_w, w_attn_branch,
           w_ssm_branch, w_out, norm_cross_w, norm_mem_w, w_xq, w_xkv, w_xo, norm_ffn_w, w_router, b_router,
           w_gate, w_up, w_down, norm_final_w):
    p = dict(norm_mix_w=norm_mix_w, w_in=w_in, lambda_q1=lambda_q1, lambda_k1=lambda_k1, lambda_q2=lambda_q2,
             lambda_k2=lambda_k2, attn_subln_w=attn_subln_w, conv_w=conv_w, conv_b=conv_b, a_log=a_log,
             dt_bias=dt_bias, d_skip=d_skip, ssm_norm_w=ssm_norm_w, w_attn_branch=w_attn_branch,
             w_ssm_branch=w_ssm_branch, w_out=w_out, norm_cross_w=norm_cross_w, norm_mem_w=norm_mem_w,
             w_xq=w_xq, w_xkv=w_xkv, w_xo=w_xo, norm_ffn_w=norm_ffn_w, w_router=w_router, b_router=b_router,
             w_gate=w_gate, w_up=w_up, w_down=w_down)
    layers = [_prep_layer(l, p) for l in range(w_in.shape[0])]
    y_prompt = _trunk(x_prompt, mem_prompt, layers, norm_final_w)
    y_sample = _trunk(x_sample, mem_sample, layers, norm_final_w)
    return (y_prompt, y_sample)
```

```python
import functools
import math

import jax
import jax.numpy as jnp
from jax import lax
from jax.experimental import pallas as pl
from jax.experimental.pallas import tpu as pltpu

F32 = jnp.float32
BF16 = jnp.bfloat16
I32 = jnp.int32

D_MODEL = 1024
ATTN_HEADS = 8
ATTN_HEAD_DIM = 64
ATTN_V_DIM = 128
SSM_HEADS = 32
SSM_HEAD_DIM = 64
SSM_GROUPS = 4
D_STATE = 128
D_INNER = 2048
D_CONV = 5
XBC_DIM = 3072
CHUNK = 128
MEM_TOKENS = 256
X_HEADS = 4
X_HEAD_DIM = 128
X_WIDTH = 512
N_EXPERTS = 16
CAPACITY_FACTOR = 2
D_FF = 2816
EPS = 1e-6
SUBLN_EPS = 1e-5

C_Z = 0
C_GATE = 2048
C_XBC = 4096
C_Q = 7168
C_K = 8192
PROJ_COLS = 9216

LANE = 128
BF16_SUBLANE = 16
VMEM_LIMIT = 56 * 1024 * 1024


def _cparams(sem, vmem=None):
    return pltpu.CompilerParams(dimension_semantics=sem, vmem_limit_bytes=vmem)


def _rms(x, w, eps):
    return x * lax.rsqrt(jnp.mean(x * x, axis=-1, keepdims=True) + eps) * w


def _softplus(x):
    return jnp.maximum(x, 0.0) + jnp.log(1.0 + jnp.exp(-jnp.abs(x)))


def _silu(x):
    return x * (1.0 / (1.0 + jnp.exp(-x)))


def _norm_matmul_kernel(x_ref, nw_ref, w_ref, o_ref, xn_ref):
    @pl.when(pl.program_id(1) == 0)
    def _():
        xn_ref[...] = _rms(x_ref[...], nw_ref[...], EPS).astype(xn_ref.dtype)

    o_ref[...] = jnp.dot(xn_ref[...], w_ref[...], preferred_element_type=F32).astype(o_ref.dtype)


def norm_matmul(x, nw, w, out_dtype, tm, tn):
    t, d = x.shape
    n = w.shape[1]
    return pl.pallas_call(
        _norm_matmul_kernel,
        grid=(t // tm, n // tn),
        in_specs=[
            pl.BlockSpec((tm, d), lambda i, j: (i, 0)),
            pl.BlockSpec((1, d), lambda i, j: (0, 0)),
            pl.BlockSpec((d, tn), lambda i, j: (0, j)),
        ],
        out_specs=pl.BlockSpec((tm, tn), lambda i, j: (i, j)),
        out_shape=jax.ShapeDtypeStruct((t, n), out_dtype),
        scratch_shapes=[pltpu.VMEM((tm, d), BF16)],
        compiler_params=_cparams(("parallel", "arbitrary"), VMEM_LIMIT),
        name="norm_matmul",
    )(x, nw.reshape(1, d), w)


ONES_ROWS = BF16_SUBLANE
SKIP_MARGIN = 17.0
POS_SPLIT = 32


FIXED_REF_MAX_RANGE = 60.0


def _attn_kernel(scal_ref, q_ref, k_ref, vt_ref, w_ref, o_ref, qa_s, kf_s, m_s, acc_s, kn_s, ref_s, *, tq, tk, seq):
    h = pl.program_id(1)
    qi = pl.program_id(2)
    slope = scal_ref[h]
    nk = seq // tk
    ndiag = tq // tk
    t0 = qi * tq
    dlo = qi * ndiag
    half = ATTN_HEAD_DIM
    shift = POS_SPLIT.bit_length() - 1
    nt = (((1,), (1,)), ((), ()))

    lane_k = lax.broadcasted_iota(I32, (tk, LANE), 1)

    @pl.when(qi == 0)
    def _():
        def body(i, carry):
            kc = k_ref[pl.ds(pl.multiple_of(i * tk, tk), tk), :].astype(F32)
            sq = kc * kc
            n0 = jnp.max(jnp.sum(jnp.where(lane_k < half, sq, 0.0), axis=1, keepdims=True), axis=0, keepdims=True)
            n1 = jnp.max(jnp.sum(jnp.where(lane_k >= half, sq, 0.0), axis=1, keepdims=True), axis=0, keepdims=True)
            return jnp.maximum(carry[0], n0), jnp.maximum(carry[1], n1)

        z = jnp.zeros((1, 1), F32)
        n0, n1 = lax.fori_loop(0, nk, body, (z, z))
        kn_s[0] = jnp.broadcast_to(n0, (8, LANE))
        kn_s[1] = jnp.broadcast_to(n1, (8, LANE))

    q = q_ref[...].astype(F32) * (ATTN_HEAD_DIM ** -0.5)
    lane_q = lax.broadcasted_iota(I32, (tq, LANE), 1)
    tl = lax.broadcasted_iota(I32, (tq, LANE), 0)
    fa = (tl >> shift).astype(F32) * (-float(POS_SPLIT) * slope)
    fb = (tl & (POS_SPLIT - 1)).astype(F32) * (-slope)
    qsq = q * q
    kd = k_ref[pl.ds(pl.multiple_of(t0, tq), tq), :].astype(F32)
    sl = lax.broadcasted_iota(I32, (tk, LANE), 0)
    ka = (sl >> shift).astype(F32) * (float(POS_SPLIT) * slope)
    kb = (sl & (POS_SPLIT - 1)).astype(F32) * slope
    bounds = []
    for c in range(2):
        base = half if c == 0 else 0
        data = (lane_q < half) if c == 0 else (lane_q >= half)
        feat = jnp.where(lane_q == base, fa, jnp.where(lane_q == base + 1, fb,
                         jnp.where(lane_q == base + 2, 1.0, jnp.where(lane_q == base + 3, 1.0, 0.0))))
        qa_s[c] = jnp.where(data, q, feat).astype(BF16)
        pick = jnp.where(lax.broadcasted_iota(I32, (8, LANE), 1) // half == c, 1.0, 0.0)
        qn_row = lax.dot_general(pick, qsq, nt, preferred_element_type=F32,
                                 precision=lax.Precision.HIGHEST)[0:1, :]
        ref_c = jnp.sqrt(qn_row * kn_s[c][0:1, 0:1]) * 1.02
        ref_s[c] = ref_c
        self_row = lax.dot_general(pick, q * kd, nt, preferred_element_type=F32,
                                   precision=lax.Precision.HIGHEST)[0:1, :]
        bounds.append(jnp.max(ref_c - self_row, axis=1, keepdims=True))
        fl = jnp.where(lane_k == base, 1.0, jnp.where(lane_k == base + 1, 1.0,
                       jnp.where(lane_k == base + 2, ka, jnp.where(lane_k == base + 3, kb, 0.0))))
        kf_s[2 * c] = fl.astype(BF16)
        kf_s[2 * c + 1] = (-fl).astype(BF16)

    span = jnp.maximum(bounds[0], bounds[1])
    dskip = jnp.minimum((span + (math.log(seq) + SKIP_MARGIN)) / slope, 2.0 * seq)
    t0f = jnp.full((1, 1), t0, I32).astype(F32)
    dlof = jnp.full((1, 1), dlo, I32).astype(F32)
    klo_f = jnp.clip(jnp.floor((t0f + 1.0 - dskip) / tk), 0.0, dlof)
    khi_f = jnp.clip(jnp.ceil((dskip + t0f + (tq - 1.0)) / tk), dlof + ndiag, float(nk))
    klo = jnp.max(klo_f).astype(I32)
    khi = jnp.max(khi_f).astype(I32)
    fixed_ref = jnp.max(jnp.where(span <= FIXED_REF_MAX_RANGE, 1.0, 0.0)).astype(I32)

    m_s[...] = jnp.full(m_s.shape, -jnp.inf, F32)
    acc_s[...] = jnp.zeros(acc_s.shape, F32)
    ones = jnp.ones((ONES_ROWS, tk), BF16)

    def block(ki, side, fixed):
        s0 = pl.multiple_of(ki * tk, tk)
        kblk = k_ref[pl.ds(s0, tk), :]
        vaug = jnp.concatenate([vt_ref[:, pl.ds(s0, tk)], ones], axis=0)
        off = jnp.full((1, tq), t0 - s0, I32).astype(F32) * slope
        cblk = -off if side == 'R' else off
        if side == 'D':
            rel = (lax.broadcasted_iota(I32, (tk, tq), 0) - lax.broadcasted_iota(I32, (tk, tq), 1)
                   + (s0 - t0)).astype(F32)
            corr = jnp.maximum(rel, 0.0) * (-2.0 * slope)
        for c in range(2):
            data = (lane_k < half) if c == 0 else (lane_k >= half)
            kaug = jnp.where(data, kblk, kf_s[2 * c + (1 if side == 'R' else 0)])
            st = lax.dot_general(kaug, qa_s[c], nt, preferred_element_type=F32)
            if side == 'D':
                st = st + corr
            if fixed:
                p = jnp.exp(st - (ref_s[c] + cblk)).astype(BF16)
                acc_s[c] += jnp.dot(vaug, p, preferred_element_type=F32)
            else:
                m_prev = m_s[c]
                m_new = jnp.maximum(m_prev, jnp.max(st, axis=0, keepdims=True) - cblk)
                p = jnp.exp(st - (m_new + cblk)).astype(BF16)
                alpha = jnp.exp(m_prev - m_new)
                acc_s[c] = acc_s[c] * alpha + jnp.dot(vaug, p, preferred_element_type=F32)
                m_s[c] = m_new

    def sweep(fixed):
        def run(side):
            def body(ki, carry):
                block(ki, side, fixed)
                return carry
            return body

        lax.fori_loop(klo, dlo, run('L'), 0)
        for d in range(ndiag):
            block(dlo + d, 'D', fixed)
        lax.fori_loop(dlo + ndiag, khi, run('R'), 0)

    @pl.when(fixed_ref == 1)
    def _():
        sweep(True)

    @pl.when(fixed_ref == 0)
    def _():
        sweep(False)

    lam = scal_ref[ATTN_HEADS]
    post = scal_ref[ATTN_HEADS + 1]
    a0 = acc_s[0]
    a1 = acc_s[1]
    o = a0[0:ATTN_V_DIM] / a0[ATTN_V_DIM:ATTN_V_DIM + 1] - lam * (a1[0:ATTN_V_DIM] / a1[ATTN_V_DIM:ATTN_V_DIM + 1])
    o = o * lax.rsqrt(jnp.mean(o * o, axis=0, keepdims=True) + SUBLN_EPS) * (w_ref[...] * post)
    o_ref[...] = o.T.astype(o_ref.dtype)


def diff_attention(proj3, vt, scal, subln_w, tq, tk):
    b, s, _ = proj3.shape
    kq, kk = C_Q // LANE, C_K // LANE
    return pl.pallas_call(
        functools.partial(_attn_kernel, tq=tq, tk=tk, seq=s),
        grid=(b, ATTN_HEADS, s // tq),
        in_specs=[
            pl.BlockSpec(memory_space=pltpu.SMEM),
            pl.BlockSpec((None, tq, LANE), lambda bi, h, qi: (bi, qi, kq + h)),
            pl.BlockSpec((None, s, LANE), lambda bi, h, qi: (bi, 0, kk + h)),
            pl.BlockSpec((ATTN_V_DIM, s), lambda bi, h, qi: (h, bi)),
            pl.BlockSpec((ATTN_V_DIM, 1), lambda bi, h, qi: (0, 0)),
        ],
        out_specs=pl.BlockSpec((None, tq, LANE), lambda bi, h, qi: (bi, qi, h)),
        out_shape=jax.ShapeDtypeStruct((b, s, ATTN_HEADS * ATTN_V_DIM), BF16),
        scratch_shapes=[
            pltpu.VMEM((2, tq, LANE), BF16),
            pltpu.VMEM((4, tk, LANE), BF16),
            pltpu.VMEM((2, 1, tq), F32),
            pltpu.VMEM((2, ATTN_V_DIM + ONES_ROWS, tq), F32),
            pltpu.VMEM((2, 8, LANE), F32),
            pltpu.VMEM((2, 1, tq), F32),
        ],
        compiler_params=_cparams(("parallel", "parallel", "arbitrary"), VMEM_LIMIT),
        name="diff_attention",
    )(scal, proj3, proj3, vt, subln_w.reshape(ATTN_V_DIM, 1))


def _proj_t_kernel(x_ref, nw_ref, wvt_ref, wdt_ref, wdtt_ref, vt_ref, dt_ref, dtt_ref):
    xn = _rms(x_ref[...], nw_ref[...], EPS).astype(BF16)
    nt = (((1,), (1,)), ((), ()))
    vt_ref[...] = lax.dot_general(wvt_ref[...], xn, nt, preferred_element_type=F32).astype(vt_ref.dtype)
    dt_ref[...] = jnp.dot(xn, wdt_ref[...], preferred_element_type=F32)
    dtt_ref[...] = lax.dot_general(wdtt_ref[...], xn, nt, preferred_element_type=F32)


def proj_transposed(x, nw, wvt, wdt, wdtt, tm):
    t, d = x.shape
    nv = wvt.shape[0]
    const = lambda i: (0, 0)
    return pl.pallas_call(
        _proj_t_kernel,
        grid=(t // tm,),
        in_specs=[
            pl.BlockSpec((tm, d), lambda i: (i, 0)),
            pl.BlockSpec((1, d), const),
            pl.BlockSpec((nv, d), const),
            pl.BlockSpec((d, LANE), const),
            pl.BlockSpec((LANE, d), const),
        ],
        out_specs=[pl.BlockSpec((nv, tm), lambda i: (0, i)), pl.BlockSpec((tm, LANE), lambda i: (i, 0)),
                   pl.BlockSpec((LANE, tm), lambda i: (0, i))],
        out_shape=[jax.ShapeDtypeStruct((nv, t), BF16), jax.ShapeDtypeStruct((t, LANE), F32),
                   jax.ShapeDtypeStruct((LANE, t), F32)],
        compiler_params=_cparams(("parallel",), VMEM_LIMIT),
        name="proj_transposed",
    )(x, nw.reshape(1, d), wvt, wdt, wdtt)


HALO = 16


def _conv_kernel(prev_ref, cur_ref, next_ref, w_ref, b_ref, o_ref, ext_ref, *, tm):
    i = pl.program_id(1)
    last = pl.num_programs(1) - 1
    prev = prev_ref[...].astype(F32)
    nxt = next_ref[...].astype(F32)
    ext_ref[pl.ds(0, HALO), :] = jnp.where(i > 0, prev, 0.0)
    ext_ref[pl.ds(HALO, tm), :] = cur_ref[...].astype(F32)
    ext_ref[pl.ds(HALO + tm, HALO), :] = jnp.where(i < last, nxt, 0.0)
    w = w_ref[...]
    acc = jnp.zeros(o_ref.shape, F32) + b_ref[...]
    for j in range(D_CONV):
        acc = acc + ext_ref[pl.ds(HALO - D_CONV // 2 + j, tm), :] * w[j:j + 1, :]
    o_ref[...] = _silu(acc).astype(o_ref.dtype)


def conv_silu(proj3, conv_w, conv_b, tm, tc):
    b, s, _ = proj3.shape
    c0 = C_XBC // tc
    hb = tm // HALO
    nh = s // HALO
    return pl.pallas_call(
        functools.partial(_conv_kernel, tm=tm),
        grid=(b, s // tm, XBC_DIM // tc),
        in_specs=[
            pl.BlockSpec((None, HALO, tc), lambda bi, i, j: (bi, jnp.maximum(i * hb - 1, 0), c0 + j)),
            pl.BlockSpec((None, tm, tc), lambda bi, i, j: (bi, i, c0 + j)),
            pl.BlockSpec((None, HALO, tc), lambda bi, i, j: (bi, jnp.minimum((i + 1) * hb, nh - 1), c0 + j)),
            pl.BlockSpec((D_CONV, tc), lambda bi, i, j: (0, j)),
            pl.BlockSpec((1, tc), lambda bi, i, j: (0, j)),
        ],
        out_specs=pl.BlockSpec((None, tm, tc), lambda bi, i, j: (bi, i, j)),
        out_shape=jax.ShapeDtypeStruct((b, s, XBC_DIM), BF16),
        scratch_shapes=[pltpu.VMEM((tm + 2 * HALO, tc), F32)],
        compiler_params=_cparams(("parallel", "parallel", "parallel"), VMEM_LIMIT),
        name="conv_silu",
    )(proj3, proj3, proj3, conv_w, conv_b.reshape(1, XBC_DIM))


def _expand_heads(a):
    rows = a.shape[0]
    lane = lax.broadcasted_iota(I32, (rows, LANE), 1)
    parts = []
    for hp in range(SSM_HEADS // 2):
        lo = a[:, 2 * hp:2 * hp + 1]
        hi = a[:, 2 * hp + 1:2 * hp + 2]
        parts.append(jnp.where(lane < SSM_HEAD_DIM, lo, hi))
    return jnp.concatenate(parts, axis=1)


def _tri(n, fn):
    r = lax.broadcasted_iota(I32, (n, n), 0)
    c = lax.broadcasted_iota(I32, (n, n), 1)
    return jnp.where(fn(r, c), 1.0, 0.0).astype(F32)


def _dot_hi(a, b):
    return jnp.dot(a, b, preferred_element_type=F32, precision=lax.Precision.HIGHEST)


def _ssd_state_kernel(xf_ref, bf_ref, dtf_ref, xb_ref, bb_ref, dtb_ref, par_ref, hf_ref, hb_ref, sf_s, sb_s):
    j = pl.program_id(1)

    @pl.when(j == 0)
    def _():
        sf_s[...] = jnp.zeros(sf_s.shape, F32)
        sb_s[...] = jnp.zeros(sb_s.shape, F32)

    hf_ref[...] = sf_s[...].astype(hf_ref.dtype)
    hb_ref[...] = sb_s[...].astype(hb_ref.dtype)

    par = par_ref[...]
    lower_incl = _tri(CHUNK, lambda r, c: c <= r)
    lower_strict = _tri(CHUNK, lambda r, c: c < r)

    def one_direction(x_ref, b_ref, dt_ref, s_ref, col0, backward):
        raw = dt_ref[...][:, col0:col0 + SSM_HEADS]
        dt = _softplus(raw + par[0:1, col0:col0 + SSM_HEADS])
        a = dt * par[1:2, col0:col0 + SSM_HEADS]
        if backward:
            excl = _dot_hi(lower_strict, a)
            wgt = dt * jnp.exp(excl)
            total = excl[CHUNK - 1:CHUNK, :] + a[CHUNK - 1:CHUNK, :]
        else:
            cum = _dot_hi(lower_incl, a)
            total = cum[CHUNK - 1:CHUNK, :]
            wgt = dt * jnp.exp(total - cum)
        xw = (x_ref[...].astype(F32) * _expand_heads(wgt)).astype(BF16)
        dec = _expand_heads(jnp.exp(jnp.broadcast_to(total, (8, SSM_HEADS))))[0:1, :]
        bmat = b_ref[...]
        gw = D_INNER // SSM_GROUPS
        for g in range(SSM_GROUPS):
            contrib = lax.dot_general(bmat[:, g * D_STATE:(g + 1) * D_STATE], xw[:, g * gw:(g + 1) * gw],
                                      (((0,), (0,)), ((), ())), preferred_element_type=F32)
            s_ref[g] = s_ref[g] * dec[:, g * gw:(g + 1) * gw] + contrib

    one_direction(xf_ref, bf_ref, dtf_ref, sf_s, 0, False)
    one_direction(xb_ref, bb_ref, dtb_ref, sb_s, SSM_HEADS, True)


def ssd_states(xbc, dt3, par):
    b, s, _ = xbc.shape
    nc = s // CHUNK
    gw = D_INNER // SSM_GROUPS
    xblk = D_INNER // D_INNER
    bcol = D_INNER // (SSM_GROUPS * D_STATE)
    hshape = jax.ShapeDtypeStruct((b, nc, SSM_GROUPS, D_STATE, gw), BF16)
    hspec_f = pl.BlockSpec((None, None, SSM_GROUPS, D_STATE, gw), lambda bi, j: (bi, j, 0, 0, 0))
    hspec_b = pl.BlockSpec((None, None, SSM_GROUPS, D_STATE, gw), lambda bi, j: (bi, nc - 1 - j, 0, 0, 0))
    del xblk
    return pl.pallas_call(
        _ssd_state_kernel,
        grid=(b, nc),
        in_specs=[
            pl.BlockSpec((None, CHUNK, D_INNER), lambda bi, j: (bi, j, 0)),
            pl.BlockSpec((None, CHUNK, SSM_GROUPS * D_STATE), lambda bi, j: (bi, j, bcol)),
            pl.BlockSpec((None, CHUNK, LANE), lambda bi, j: (bi, j, 0)),
            pl.BlockSpec((None, CHUNK, D_INNER), lambda bi, j: (bi, nc - 1 - j, 0)),
            pl.BlockSpec((None, CHUNK, SSM_GROUPS * D_STATE), lambda bi, j: (bi, nc - 1 - j, bcol)),
            pl.BlockSpec((None, CHUNK, LANE), lambda bi, j: (bi, nc - 1 - j, 0)),
            pl.BlockSpec((8, LANE), lambda bi, j: (0, 0)),
        ],
        out_specs=[hspec_f, hspec_b],
        out_shape=[hshape, hshape],
        scratch_shapes=[pltpu.VMEM((SSM_GROUPS, D_STATE, gw), F32), pltpu.VMEM((SSM_GROUPS, D_STATE, gw), F32)],
        compiler_params=_cparams(("parallel", "arbitrary"), VMEM_LIMIT),
        name="ssd_states",
    )(xbc, xbc, dt3, xbc, xbc, dt3, par)


def _ssd_out_kernel(x_ref, b_ref, c_ref, z_ref, dt_ref, dtt_ref, par_ref, part_ref, dsk_ref, nw_ref,
                    hf_ref, hb_ref, o_ref):
    par = par_ref[...]
    part = part_ref[...]
    lower_incl = _tri(CHUNK, lambda r, c: c <= r)
    lower_strict = _tri(CHUNK, lambda r, c: c < r)
    upper_incl = _tri(CHUNK, lambda r, c: r <= c)
    upper_strict = _tri(CHUNK, lambda r, c: r < c)

    dtc = _softplus(dt_ref[...][:, 0:2 * SSM_HEADS] + par[0:1, 0:2 * SSM_HEADS])
    ac = dtc * par[1:2, 0:2 * SSM_HEADS]
    cum_c = _dot_hi(lower_incl, ac)
    excl_c = _dot_hi(lower_strict, ac)
    dtr = _softplus(dtt_ref[...][0:2 * SSM_HEADS, :] + part[0:2 * SSM_HEADS, 0:1])
    ar = dtr * part[0:2 * SSM_HEADS, 1:2]
    cum_r = _dot_hi(ar, upper_incl)
    excl_r = _dot_hi(ar, upper_strict)

    cumf_c = cum_c[:, 0:SSM_HEADS]
    exclb_c = excl_c[:, SSM_HEADS:2 * SSM_HEADS]
    totb = cum_c[CHUNK - 1:CHUNK, SSM_HEADS:2 * SSM_HEADS]
    dtf_r = dtr[0:SSM_HEADS, :]
    dtb_r = dtr[SSM_HEADS:2 * SSM_HEADS, :]
    fwd_r = cum_r[0:SSM_HEADS, :] - jnp.log(dtf_r)
    bwd_r = excl_r[SSM_HEADS:2 * SSM_HEADS, :] + jnp.log(dtb_r)
    diag_r = jnp.log(dtf_r + dtb_r)

    row = lax.broadcasted_iota(I32, (CHUNK, CHUNK), 0)
    col = lax.broadcasted_iota(I32, (CHUNK, CHUNK), 1)
    low = col < row
    diag = col == row
    lane = lax.broadcasted_iota(I32, (CHUNK, LANE), 1)
    first = lane < SSM_HEAD_DIM

    gw = D_INNER // SSM_GROUPS
    hpg = SSM_HEADS // SSM_GROUPS
    scale_f = _expand_heads(jnp.exp(cumf_c))
    scale_b = _expand_heads(jnp.exp(totb - exclb_c))
    for g in range(SSM_GROUPS):
        cg = c_ref[:, g * D_STATE:(g + 1) * D_STATE]
        bg = b_ref[:, g * D_STATE:(g + 1) * D_STATE]
        cb = lax.dot_general(cg, bg, (((1,), (1,)), ((), ())), preferred_element_type=F32)
        inter_f = jnp.dot(cg, hf_ref[g], preferred_element_type=F32)
        inter_b = jnp.dot(cg, hb_ref[g], preferred_element_type=F32)
        parts = []
        for pp in range(hpg // 2):
            gms = []
            for h in (g * hpg + 2 * pp, g * hpg + 2 * pp + 1):
                arg = jnp.where(low, cumf_c[:, h:h + 1] - fwd_r[h:h + 1, :],
                                bwd_r[h:h + 1, :] - exclb_c[:, h:h + 1])
                arg = jnp.where(diag, diag_r[h:h + 1, :], arg)
                gms.append((cb * jnp.exp(arg)).astype(BF16))
            ls = slice(g * gw + pp * LANE, g * gw + (pp + 1) * LANE)
            il = slice(pp * LANE, (pp + 1) * LANE)
            xp = x_ref[:, ls]
            rhs = jnp.concatenate([jnp.where(first, xp, 0), jnp.where(first, 0, xp)], axis=0)
            yp = jnp.dot(jnp.concatenate(gms, axis=1), rhs, preferred_element_type=F32)
            yp = (yp + inter_f[:, il] * scale_f[:, ls] + inter_b[:, il] * scale_b[:, ls]
                  + xp.astype(F32) * dsk_ref[:, ls])
            parts.append(yp * _silu(z_ref[:, ls].astype(F32)))
        yg = jnp.concatenate(parts, axis=1)
        sl = slice(g * gw, (g + 1) * gw)
        o_ref[:, sl] = (yg * lax.rsqrt(jnp.mean(yg * yg, axis=-1, keepdims=True) + EPS)
                        * nw_ref[:, sl]).astype(o_ref.dtype)


def ssd_output(xbc, proj3, dt3, dtt3, par, part, dskip_x, nw, hf, hb):
    b, s, _ = xbc.shape
    nc = s // CHUNK
    gw = D_INNER // SSM_GROUPS
    gs = SSM_GROUPS * D_STATE
    hspec = pl.BlockSpec((None, None, SSM_GROUPS, D_STATE, gw), lambda bi, j: (bi, j, 0, 0, 0))
    return pl.pallas_call(
        _ssd_out_kernel,
        grid=(b, nc),
        in_specs=[
            pl.BlockSpec((None, CHUNK, D_INNER), lambda bi, j: (bi, j, 0)),
            pl.BlockSpec((None, CHUNK, gs), lambda bi, j: (bi, j, D_INNER // gs)),
            pl.BlockSpec((None, CHUNK, gs), lambda bi, j: (bi, j, D_INNER // gs + 1)),
            pl.BlockSpec((None, CHUNK, D_INNER), lambda bi, j: (bi, j, C_Z // D_INNER)),
            pl.BlockSpec((None, CHUNK, LANE), lambda bi, j: (bi, j, 0)),
            pl.BlockSpec((LANE, CHUNK), lambda bi, j: (0, bi * nc + j)),
            pl.BlockSpec((8, LANE), lambda bi, j: (0, 0)),
            pl.BlockSpec((LANE, 8), lambda bi, j: (0, 0)),
            pl.BlockSpec((1, D_INNER), lambda bi, j: (0, 0)),
            pl.BlockSpec((1, D_INNER), lambda bi, j: (0, 0)),
            hspec, hspec,
        ],
        out_specs=pl.BlockSpec((None, CHUNK, D_INNER), lambda bi, j: (bi, j, 0)),
        out_shape=jax.ShapeDtypeStruct((b, s, D_INNER), BF16),
        compiler_params=_cparams(("parallel", "parallel"), VMEM_LIMIT),
        name="ssd_output",
    )(xbc, xbc, xbc, proj3, dt3, dtt3, par, part, dskip_x, nw, hf, hb)


def _merge_kernel(x_ref, a_ref, s_ref, g0_ref, g1_ref, wa_ref, ws_ref, wo_ref, o_ref):
    ya = jnp.dot(a_ref[...], wa_ref[...], preferred_element_type=F32)
    ys = jnp.dot(s_ref[...], ws_ref[...], preferred_element_type=F32)
    g0 = 1.0 / (1.0 + jnp.exp(-g0_ref[...].astype(F32)))
    g1 = 1.0 / (1.0 + jnp.exp(-g1_ref[...].astype(F32)))
    merged = (g0 * ya + g1 * ys).astype(BF16)
    o_ref[...] = x_ref[...] + jnp.dot(merged, wo_ref[...], preferred_element_type=F32)


def merge_out(x, attn, ssm, proj, wa, ws, wo, tm):
    t, d = x.shape
    gb = C_GATE // d
    const = lambda i: (0, 0)
    return pl.pallas_call(
        _merge_kernel,
        grid=(t // tm,),
        in_specs=[
            pl.BlockSpec((tm, d), lambda i: (i, 0)),
            pl.BlockSpec((tm, d), lambda i: (i, 0)),
            pl.BlockSpec((tm, D_INNER), lambda i: (i, 0)),
            pl.BlockSpec((tm, d), lambda i: (i, gb)),
            pl.BlockSpec((tm, d), lambda i: (i, gb + 1)),
            pl.BlockSpec((d, d), const),
            pl.BlockSpec((D_INNER, d), const),
            pl.BlockSpec((d, d), const),
        ],
        out_specs=pl.BlockSpec((tm, d), lambda i: (i, 0)),
        out_shape=jax.ShapeDtypeStruct((t, d), F32),
        compiler_params=_cparams(("parallel",), VMEM_LIMIT),
        name="merge_out",
    )(x, attn, ssm, proj, proj, wa, ws, wo)


def _cross_kernel(x_ref, nw_ref, wq_ref, kv_ref, wo_ref, o_ref):
    x = x_ref[...]
    h = _rms(x, nw_ref[...], EPS).astype(BF16)
    q = jnp.dot(h, wq_ref[...], preferred_element_type=F32).astype(BF16)
    kv = kv_ref[...]
    outs = []
    for hd in range(X_HEADS):
        qh = q[:, hd * X_HEAD_DIM:(hd + 1) * X_HEAD_DIM]
        kh = kv[:, hd * X_HEAD_DIM:(hd + 1) * X_HEAD_DIM]
        vh = kv[:, X_WIDTH + hd * X_HEAD_DIM:X_WIDTH + (hd + 1) * X_HEAD_DIM]
        s = lax.dot_general(qh, kh, (((1,), (1,)), ((), ())), preferred_element_type=F32) * (X_HEAD_DIM ** -0.5)
        s = s - jnp.max(s, axis=-1, keepdims=True)
        p = jnp.exp(s)
        p = (p / jnp.sum(p, axis=-1, keepdims=True)).astype(BF16)
        outs.append(jnp.dot(p, vh, preferred_element_type=F32).astype(BF16))
    o = jnp.concatenate(outs, axis=1)
    o_ref[...] = x + jnp.dot(o, wo_ref[...], preferred_element_type=F32)


def cross_attention(x, nw, wq, kv, wo, seq, tm):
    t, d = x.shape
    per_seq = seq // tm
    const = lambda i: (0, 0)
    return pl.pallas_call(
        _cross_kernel,
        grid=(t // tm,),
        in_specs=[
            pl.BlockSpec((tm, d), lambda i: (i, 0)),
            pl.BlockSpec((1, d), const),
            pl.BlockSpec((d, X_WIDTH), const),
            pl.BlockSpec((None, MEM_TOKENS, 2 * X_WIDTH), lambda i: (i // per_seq, 0, 0)),
            pl.BlockSpec((X_WIDTH, d), const),
        ],
        out_specs=pl.BlockSpec((tm, d), lambda i: (i, 0)),
        out_shape=jax.ShapeDtypeStruct((t, d), F32),
        compiler_params=_cparams(("parallel",), VMEM_LIMIT),
        name="cross_attention",
    )(x, nw.reshape(1, d), wq, kv, wo)


def _router_kernel(x_ref, nw_ref, wrt_ref, br_ref, hb_ref, aff_ref):
    h = _rms(x_ref[...], nw_ref[...], EPS)
    hb_ref[...] = h.astype(hb_ref.dtype)
    logits = lax.dot_general(wrt_ref[...], h, (((1,), (1,)), ((), ())), preferred_element_type=F32,
                             precision=lax.Precision.HIGHEST) + br_ref[...]
    e = jnp.exp(logits - jnp.max(logits, axis=0, keepdims=True))
    aff_ref[...] = e / jnp.sum(e, axis=0, keepdims=True)


def router(x, nw, wrt, br, tm):
    t, d = x.shape
    return pl.pallas_call(
        _router_kernel,
        grid=(t // tm,),
        in_specs=[
            pl.BlockSpec((tm, d), lambda i: (i, 0)),
            pl.BlockSpec((1, d), lambda i: (0, 0)),
            pl.BlockSpec((N_EXPERTS, d), lambda i: (0, 0)),
            pl.BlockSpec((N_EXPERTS, 1), lambda i: (0, 0)),
        ],
        out_specs=[pl.BlockSpec((tm, d), lambda i: (i, 0)), pl.BlockSpec((N_EXPERTS, tm), lambda i: (0, i))],
        out_shape=[jax.ShapeDtypeStruct((t, d), BF16), jax.ShapeDtypeStruct((N_EXPERTS, t), F32)],
        compiler_params=_cparams(("parallel",), VMEM_LIMIT),
        name="router",
    )(x, nw.reshape(1, d), wrt, br.reshape(N_EXPERTS, 1))


TB = 256


def _topk_kernel(aff_ref, pos_ref, gate_ref, roff_ref, *, cap):
    a = aff_ref[...]
    nr = a.shape[1]
    bits = pltpu.bitcast(a, I32)

    def count(mask):
        c = jnp.sum(jnp.where(mask, 1.0, 0.0), axis=2, keepdims=True)
        return jnp.sum(c, axis=1, keepdims=True)

    def body(i, thr):
        cand = thr | jnp.left_shift(jnp.int32(1), 30 - i)
        return jnp.where(count(bits >= cand) >= float(cap), cand, thr)

    thr = lax.fori_loop(0, 31, body, jnp.zeros((N_EXPERTS, 1, 1), I32))
    gt = bits > thr
    eq = bits == thr
    need = float(cap) - count(gt)

    ustrict = _tri(TB, lambda r, c: r < c).astype(BF16)
    ones = jnp.ones((TB, TB), BF16)
    lstrict = _tri(nr, lambda r, c: c < r).astype(BF16)

    def excl_prefix(m):
        mb = m.astype(BF16)
        within = jnp.dot(mb, ustrict, preferred_element_type=F32)
        rowsum = jnp.dot(mb, ones, preferred_element_type=F32)
        rowoff = jnp.dot(lstrict, rowsum.astype(BF16), preferred_element_type=F32)
        return within + rowoff, rowoff

    for e in range(N_EXPERTS):
        eq_e = jnp.where(eq[e], 1.0, 0.0)
        pe, _ = excl_prefix(eq_e)
        keep = jnp.where(pe < need[e], eq_e, 0.0)
        sel = jnp.where(gt[e], 1.0, keep)
        ps, roff = excl_prefix(sel)
        chosen = sel > 0.5
        pos_ref[e] = jnp.where(chosen, ps.astype(I32), -1)
        gate_ref[e] = jnp.where(chosen, a[e], 0.0)
        roff_ref[e] = roff.astype(I32)


def topk_select(aff3, cap):
    e, r, tb = aff3.shape
    full = lambda: (0, 0, 0)
    return pl.pallas_call(
        functools.partial(_topk_kernel, cap=cap),
        grid=(),
        in_specs=[pl.BlockSpec((e, r, tb), full)],
        out_specs=[pl.BlockSpec((e, r, tb), full)] * 3,
        out_shape=[jax.ShapeDtypeStruct((e, r, tb), I32), jax.ShapeDtypeStruct((e, r, tb), F32),
                   jax.ShapeDtypeStruct((e, r, tb), I32)],
        compiler_params=pltpu.CompilerParams(vmem_limit_bytes=VMEM_LIMIT),
        name="topk_select",
    )(aff3)


GATHER_BUFS = 8


GATHER_PARTS = 2


def _ffn_kernel(lo_ref, hi_ref, roff_ref, rend_ref, pos_ref, gate_ref, hb_hbm, wg_ref, wu_ref, wd_ref, y_ref,
                xbuf, sem, xc_s, g_s, *, ts, nj):
    e = pl.program_id(0)
    j = pl.program_id(1)
    nr = pos_ref.shape[0]
    step = e * nj + j
    nsteps = pl.num_programs(0) * nj
    lo = lo_ref[step]
    hi = hi_ref[step]
    xc_s[...] = jnp.zeros(xc_s.shape, F32)
    g_s[...] = jnp.zeros(g_s.shape, F32)

    def fetch(r, slot):
        return pltpu.make_async_copy(hb_hbm.at[pl.ds(r * TB, TB)], xbuf.at[slot], sem.at[slot])

    def prime(first, end):
        for i in range(GATHER_BUFS - 1):
            @pl.when(first + i < end)
            def _(i=i):
                fetch(first + i, i).start()

    @pl.when(step == 0)
    def _():
        prime(lo, hi)

    part = ts // GATHER_PARTS
    slot_ids = lax.broadcasted_iota(I32, (part, TB), 0)

    def body(r, carry):
        idx = r - lo
        slot = idx & (GATHER_BUFS - 1)
        fetch(r, slot).wait()

        @pl.when(r + (GATHER_BUFS - 1) < hi)
        def _():
            fetch(r + (GATHER_BUFS - 1), (idx + (GATHER_BUFS - 1)) & (GATHER_BUFS - 1)).start()

        prow = pos_ref[pl.ds(r, 1), :]
        grow = gate_ref[pl.ds(r, 1), :]
        first = roff_ref[e * nr + r]
        last = rend_ref[e * nr + r]
        for p in range(GATHER_PARTS):
            base = j * ts + p * part

            @pl.when(jnp.logical_and(first < base + part, last > base))
            def _(p=p, base=base):
                hit = prow == slot_ids + base
                sel = jnp.where(hit, 1.0, 0.0).astype(BF16)
                rows = slice(p * part, (p + 1) * part)
                xc_s[rows, :] += jnp.dot(sel, xbuf[slot], preferred_element_type=F32)
                g_s[rows, :] += jnp.sum(jnp.where(hit, grow, 0.0), axis=1, keepdims=True)
        return carry

    lax.fori_loop(lo, hi, body, 0)

    @pl.when(step + 1 < nsteps)
    def _():
        prime(lo_ref[step + 1], hi_ref[step + 1])

    xc = xc_s[...].astype(BF16)
    hg = jnp.dot(xc, wg_ref[...], preferred_element_type=F32)
    hu = jnp.dot(xc, wu_ref[...], preferred_element_type=F32)
    hid = (_silu(hg) * hu).astype(BF16)
    y = jnp.dot(hid, wd_ref[...], preferred_element_type=F32) * g_s[...]
    y_ref[...] = y.astype(y_ref.dtype)


def expert_ffn(lo, hi, roff, rend, pos, gate, hb, wg, wu, wd, cap, ts):
    e, r, tb = pos.shape
    d = hb.shape[1]
    nj = cap // ts
    grid_spec = pltpu.PrefetchScalarGridSpec(
        num_scalar_prefetch=4,
        grid=(e, nj),
        in_specs=[
            pl.BlockSpec((None, r, tb), lambda ei, j, *_: (ei, 0, 0)),
            pl.BlockSpec((None, r, tb), lambda ei, j, *_: (ei, 0, 0)),
            pl.BlockSpec(memory_space=pl.ANY),
            pl.BlockSpec((None, d, D_FF), lambda ei, j, *_: (ei, 0, 0)),
            pl.BlockSpec((None, d, D_FF), lambda ei, j, *_: (ei, 0, 0)),
            pl.BlockSpec((None, D_FF, d), lambda ei, j, *_: (ei, 0, 0)),
        ],
        out_specs=pl.BlockSpec((None, ts, d), lambda ei, j, *_: (ei, j, 0)),
        scratch_shapes=[
            pltpu.VMEM((GATHER_BUFS, TB, d), BF16),
            pltpu.SemaphoreType.DMA((GATHER_BUFS,)),
            pltpu.VMEM((ts, d), F32),
            pltpu.VMEM((ts, 1), F32),
        ],
    )
    return pl.pallas_call(
        functools.partial(_ffn_kernel, ts=ts, nj=nj),
        grid_spec=grid_spec,
        out_shape=jax.ShapeDtypeStruct((e, cap, d), BF16),
        compiler_params=_cparams(("arbitrary", "arbitrary"), VMEM_LIMIT),
        name="expert_ffn",
    )(lo, hi, roff.reshape(-1), rend.reshape(-1), pos, gate, hb, wg, wu, wd)


WIN_MAIN = 128
WIN_OVER = TB + BF16_SUBLANE - WIN_MAIN


def _combine_kernel(start_ref, ostart_ref, ovf_ref, x_ref, post_ref, *rest):
    y_main = rest[:N_EXPERTS]
    y_over = rest[N_EXPERTS:2 * N_EXPERTS]
    o_ref = rest[2 * N_EXPERTS]
    r = pl.program_id(0)
    nr = pl.num_programs(0)
    post = post_ref[...]
    rels = [post[:, e:e + 1] - start_ref[e * nr + r] for e in range(N_EXPERTS)]
    lane = lax.broadcasted_iota(I32, (TB, WIN_MAIN), 1)
    acc = x_ref[...]
    for e in range(0, N_EXPERTS, 2):
        sel = jnp.concatenate([jnp.where(rels[e] == lane, 1.0, 0.0),
                               jnp.where(rels[e + 1] == lane, 1.0, 0.0)], axis=1).astype(BF16)
        rows = jnp.concatenate([y_main[e][...], y_main[e + 1][...]], axis=0)
        acc = acc + jnp.dot(sel, rows, preferred_element_type=F32)
    o_ref[...] = acc

    @pl.when(ovf_ref[r] != 0)
    def _():
        lane_o = lax.broadcasted_iota(I32, (TB, WIN_OVER), 1) + WIN_MAIN
        more = o_ref[...]
        for e in range(N_EXPERTS):
            sel = jnp.where(rels[e] == lane_o, 1.0, 0.0).astype(BF16)
            more = more + jnp.dot(sel, y_over[e][...], preferred_element_type=F32)
        o_ref[...] = more


def combine(x, post, start, ostart, ovf, y):
    t, d = x.shape
    e, cap, _ = y.shape
    nr = t // TB

    def y_spec(ei, rows, which):
        return pl.BlockSpec((pl.Element(rows), pl.Element(d)),
                            lambda r, st, ost, ov: (pl.multiple_of(ei * cap + (st, ost)[which][ei * nr + r],
                                                                   BF16_SUBLANE), 0))

    grid_spec = pltpu.PrefetchScalarGridSpec(
        num_scalar_prefetch=3,
        grid=(nr,),
        in_specs=([pl.BlockSpec((TB, d), lambda r, st, ost, ov: (r, 0)),
                   pl.BlockSpec((TB, e), lambda r, st, ost, ov: (r, 0))]
                  + [y_spec(ei, WIN_MAIN, 0) for ei in range(e)]
                  + [y_spec(ei, WIN_OVER, 1) for ei in range(e)]),
        out_specs=pl.BlockSpec((TB, d), lambda r, st, ost, ov: (r, 0)),
    )
    y2 = y.reshape(e * cap, d)
    return pl.pallas_call(
        _combine_kernel,
        grid_spec=grid_spec,
        out_shape=jax.ShapeDtypeStruct((t, d), F32),
        compiler_params=_cparams(("arbitrary",), VMEM_LIMIT),
        name="combine",
    )(start, ostart, ovf, x, post, *([y2] * (2 * e)))


def _norm_kernel(x_ref, w_ref, o_ref):
    o_ref[...] = _rms(x_ref[...], w_ref[...], EPS)


def final_norm(x, w, tm):
    t, d = x.shape
    return pl.pallas_call(
        _norm_kernel,
        grid=(t // tm,),
        in_specs=[pl.BlockSpec((tm, d), lambda i: (i, 0)), pl.BlockSpec((1, d), lambda i: (0, 0))],
        out_specs=pl.BlockSpec((tm, d), lambda i: (i, 0)),
        out_shape=jax.ShapeDtypeStruct((t, d), F32),
        compiler_params=_cparams(("parallel",)),
        name="final_norm",
    )(x, w.reshape(1, d))


def _tiles(t, s):
    def fit(n, pref):
        while n % pref:
            pref //= 2
        return pref

    cap = max(1, CAPACITY_FACTOR * t // N_EXPERTS)
    return dict(
        tm_proj=fit(t, 1024), tn_proj=1024,
        tq=fit(s, 1024), tk=fit(s, 512),
        tm_conv=fit(s, 512), tc_conv=1024,
        tm_tok=fit(s, 512),
        cap=cap, ts=fit(cap, 256),
    )


def _prep_layer(l, p):
    w_in = p['w_in'][l]
    o_q, o_k, o_v, o_z, o_xbc, o_dt = 1024, 2048, 3072, 5120, 8192, 8256
    w_main = jnp.concatenate([w_in[:, o_v:o_z], w_in[:, o_dt:], w_in[:, o_z:o_xbc], w_in[:, :o_k]],
                             axis=1).astype(BF16)
    w_vt = w_in[:, o_k:o_v].T.astype(BF16)
    w_dt = jnp.pad(w_in[:, o_xbc:o_dt], ((0, 0), (0, LANE - 2 * SSM_HEADS))).astype(BF16)
    dt_bias = p['dt_bias'][l].reshape(-1).astype(F32)
    a_neg = -jnp.exp(p['a_log'][l].astype(F32)).reshape(-1)
    par = jnp.zeros((8, LANE), F32).at[0, :2 * SSM_HEADS].set(dt_bias).at[1, :2 * SSM_HEADS].set(a_neg)
    part = jnp.zeros((LANE, 8), F32).at[:2 * SSM_HEADS, 0].set(dt_bias).at[:2 * SSM_HEADS, 1].set(a_neg)
    lam_init = 0.8 - 0.6 * math.exp(-0.3 * l)
    lam = (jnp.exp(jnp.sum(p['lambda_q1'][l].astype(F32) * p['lambda_k1'][l].astype(F32)))
           - jnp.exp(jnp.sum(p['lambda_q2'][l].astype(F32) * p['lambda_k2'][l].astype(F32))) + lam_init)
    slopes = jnp.exp2(-8.0 * (jnp.arange(ATTN_HEADS, dtype=F32) + 1.0) / ATTN_HEADS)
    scal = jnp.concatenate([slopes, lam.reshape(1), jnp.full((1,), 1.0 - lam_init, F32)]).astype(F32)
    return dict(
        norm_mix=p['norm_mix_w'][l], w_main=w_main, w_vt=w_vt, w_dt=w_dt, w_dtt=w_dt.T, par=par, part=part,
        scal=scal,
        subln=p['attn_subln_w'][l].astype(F32),
        conv_w=p['conv_w'][l].astype(F32), conv_b=p['conv_b'][l].astype(F32),
        dskip_x=jnp.repeat(p['d_skip'][l].astype(F32), SSM_HEAD_DIM).reshape(1, D_INNER),
        ssm_nw=p['ssm_norm_w'][l].astype(F32).reshape(1, D_INNER),
        wa=p['w_attn_branch'][l].astype(BF16), ws=p['w_ssm_branch'][l].astype(BF16),
        wo=p['w_out'][l].astype(BF16),
        norm_cross=p['norm_cross_w'][l], norm_mem=p['norm_mem_w'][l],
        wxq=p['w_xq'][l].astype(BF16), wxkv=p['w_xkv'][l].astype(BF16), wxo=p['w_xo'][l].astype(BF16),
        norm_ffn=p['norm_ffn_w'][l], wrt=p['w_router'][l].astype(F32).T, br=p['b_router'][l].astype(F32),
        wg=p['w_gate'][l].astype(BF16), wu=p['w_up'][l].astype(BF16), wd=p['w_down'][l].astype(BF16),
    )


def _layer(x, mem, lw, b, s):
    t = b * s
    tl = _tiles(t, s)
    proj = norm_matmul(x, lw['norm_mix'], lw['w_main'], BF16, tl['tm_proj'], tl['tn_proj'])
    vt, dt, dtt = proj_transposed(x, lw['norm_mix'], lw['w_vt'], lw['w_dt'], lw['w_dtt'], tl['tm_tok'])
    proj3 = proj.reshape(b, s, PROJ_COLS)
    dt3 = dt.reshape(b, s, LANE)
    attn = diff_attention(proj3, vt, lw['scal'], lw['subln'], tl['tq'], tl['tk'])
    xbc = conv_silu(proj3, lw['conv_w'], lw['conv_b'], tl['tm_conv'], tl['tc_conv'])
    hf, hb = ssd_states(xbc, dt3, lw['par'])
    ssm = ssd_output(xbc, proj3, dt3, dtt, lw['par'], lw['part'], lw['dskip_x'], lw['ssm_nw'], hf, hb)
    x = merge_out(x, attn.reshape(t, D_MODEL), ssm.reshape(t, D_INNER), proj, lw['wa'], lw['ws'], lw['wo'],
                  tl['tm_tok'])
    nm = mem.shape[0] * mem.shape[1]
    kv = norm_matmul(mem.reshape(nm, D_MODEL), lw['norm_mem'], lw['wxkv'], BF16, min(nm, 512), 2 * X_WIDTH)
    x = cross_attention(x, lw['norm_cross'], lw['wxq'], kv.reshape(b, MEM_TOKENS, 2 * X_WIDTH), lw['wxo'], s,
                        tl['tm_tok'])
    cap, ts = tl['cap'], tl['ts']
    nr = t // TB
    hbf, aff = router(x, lw['norm_ffn'], lw['wrt'], lw['br'], tl['tm_tok'])
    pos, gate, roff = topk_select(aff.reshape(N_EXPERTS, nr, TB), cap)
    roff = roff[:, :, 0]
    rend = jnp.concatenate([roff[:, 1:], jnp.full((N_EXPERTS, 1), cap, I32)], axis=1)
    edges = jnp.arange(cap // ts, dtype=I32) * ts
    lo = jnp.sum(rend[:, None, :] <= edges[None, :, None], axis=2).astype(I32).reshape(-1)
    hi = jnp.sum(roff[:, None, :] < (edges + ts)[None, :, None], axis=2).astype(I32).reshape(-1)
    y = expert_ffn(lo, hi, roff, rend, pos, gate, hbf, lw['wg'], lw['wu'], lw['wd'], cap, ts)
    start = jnp.minimum((roff // BF16_SUBLANE) * BF16_SUBLANE, cap - (WIN_MAIN + WIN_OVER)).astype(I32)
    ovf = jnp.any(rend - start > WIN_MAIN, axis=0).astype(I32)
    ostart = jnp.where(ovf[None, :] != 0, start + WIN_MAIN, 0).astype(I32)
    post = jnp.transpose(pos.reshape(N_EXPERTS, t))
    return combine(x, post, start.reshape(-1), ostart.reshape(-1), ovf, y)


def _trunk(x, mem, layers, norm_final_w):
    b, s, d = x.shape
    xt = x.reshape(b * s, d)
    for lw in layers:
        xt = _layer(xt, mem, lw, b, s)
    return final_norm(xt, norm_final_w, 512).reshape(b, s, d)


def kernel(x_prompt, x_sample, mem_prompt, mem_sample, norm_mix_w, w_in, lambda_q1, lambda_k1, lambda_q2,
           lambda_k2, attn_subln_w, conv_w, conv_b, a_log, dt_bias, d_skip, ssm_norm_w, w_attn_branch,
           w_ssm_branch, w_out, norm_cross_w, norm_mem_w, w_xq, w_xkv, w_xo, norm_ffn_w, w_router, b_router,
           w_gate, w_up, w_down, norm_final_w):
    p = dict(norm_mix_w=norm_mix_w, w_in=w_in, lambda_q1=lambda_q1, lambda_k1=lambda_k1, lambda_q2=lambda_q2,
             lambda_k2=lambda_k2, attn_subln_w=attn_subln_w, conv_w=conv_w, conv_b=conv_b, a_log=a_log,
             dt_bias=dt_bias, d_skip=d_skip, ssm_norm_w=ssm_norm_w, w_attn_branch=w_attn_branch,
             w_ssm_branch=w_ssm_branch, w_out=w_out, norm_cross_w=norm_cross_w, norm_mem_w=norm_mem_w,
             w_xq=w_xq, w_xkv=w_xkv, w_xo=w_xo, norm_ffn_w=norm_ffn_w, w_router=w_router, b_router=b_router,
             w_gate=w_gate, w_up=w_up, w_down=w_down)
    layers = [_prep_layer(l, p) for l in range(w_in.shape[0])]
    y_prompt = _trunk(x_prompt, mem_prompt, layers, norm_final_w)
    y_sample = _trunk(x_sample, mem_sample, layers, norm_final_w)
    return (y_prompt, y_sample)
```

```python
import functools
import math

import jax
import jax.numpy as jnp
from jax import lax
from jax.experimental import pallas as pl
from jax.experimental.pallas import tpu as pltpu

F32 = jnp.float32
BF16 = jnp.bfloat16
I32 = jnp.int32

D_MODEL = 1024
ATTN_HEADS = 8
ATTN_HEAD_DIM = 64
ATTN_V_DIM = 128
SSM_HEADS = 32
SSM_HEAD_DIM = 64
SSM_GROUPS = 4
D_STATE = 128
D_INNER = 2048
D_CONV = 5
XBC_DIM = 3072
CHUNK = 128
MEM_TOKENS = 256
X_HEADS = 4
X_HEAD_DIM = 128
X_WIDTH = 512
N_EXPERTS = 16
CAPACITY_FACTOR = 2
D_FF = 2816
EPS = 1e-6
SUBLN_EPS = 1e-5

C_Z = 0
C_GATE = 2048
C_XBC = 4096
C_Q = 7168
C_K = 8192
PROJ_COLS = 9216

LANE = 128
BF16_SUBLANE = 16
VMEM_LIMIT = 56 * 1024 * 1024


def _cparams(sem, vmem=None):
    return pltpu.CompilerParams(dimension_semantics=sem, vmem_limit_bytes=vmem)


def _rms(x, w, eps):
    return x * lax.rsqrt(jnp.mean(x * x, axis=-1, keepdims=True) + eps) * w


def _softplus(x):
    return jnp.maximum(x, 0.0) + jnp.log(1.0 + jnp.exp(-jnp.abs(x)))


def _silu(x):
    return x * (1.0 / (1.0 + jnp.exp(-x)))


def _norm_matmul_kernel(x_ref, nw_ref, w_ref, o_ref, xn_ref):
    @pl.when(pl.program_id(1) == 0)
    def _():
        xn_ref[...] = _rms(x_ref[...], nw_ref[...], EPS).astype(xn_ref.dtype)

    o_ref[...] = jnp.dot(xn_ref[...], w_ref[...], preferred_element_type=F32).astype(o_ref.dtype)


def norm_matmul(x, nw, w, out_dtype, tm, tn):
    t, d = x.shape
    n = w.shape[1]
    return pl.pallas_call(
        _norm_matmul_kernel,
        grid=(t // tm, n // tn),
        in_specs=[
            pl.BlockSpec((tm, d), lambda i, j: (i, 0)),
            pl.BlockSpec((1, d), lambda i, j: (0, 0)),
            pl.BlockSpec((d, tn), lambda i, j: (0, j)),
        ],
        out_specs=pl.BlockSpec((tm, tn), lambda i, j: (i, j)),
        out_shape=jax.ShapeDtypeStruct((t, n), out_dtype),
        scratch_shapes=[pltpu.VMEM((tm, d), BF16)],
        compiler_params=_cparams(("parallel", "arbitrary"), VMEM_LIMIT),
        name="norm_matmul",
    )(x, nw.reshape(1, d), w)


ONES_ROWS = BF16_SUBLANE
SKIP_MARGIN = 17.0
POS_SPLIT = 32


FIXED_REF_MAX_RANGE = 60.0


def _attn_kernel(scal_ref, q_ref, k_ref, vt_ref, w_ref, qf_ref, kf_ref, o_ref, qa_s, m_s, acc_s, kn_s, ref_s,
                 *, tq, tk, seq):
    h = pl.program_id(1)
    qi = pl.program_id(2)
    slope = scal_ref[h]
    nk = seq // tk
    ndiag = tq // tk
    t0 = qi * tq
    dlo = qi * ndiag
    half = ATTN_HEAD_DIM
    nt = (((1,), (1,)), ((), ()))

    lane_k = lax.broadcasted_iota(I32, (tk, LANE), 1)

    @pl.when(qi == 0)
    def _():
        def body(i, carry):
            kc = k_ref[pl.ds(pl.multiple_of(i * tk, tk), tk), :].astype(F32)
            sq = kc * kc
            n0 = jnp.max(jnp.sum(jnp.where(lane_k < half, sq, 0.0), axis=1, keepdims=True), axis=0, keepdims=True)
            n1 = jnp.max(jnp.sum(jnp.where(lane_k >= half, sq, 0.0), axis=1, keepdims=True), axis=0, keepdims=True)
            return jnp.maximum(carry[0], n0), jnp.maximum(carry[1], n1)

        z = jnp.zeros((1, 1), F32)
        n0, n1 = lax.fori_loop(0, nk, body, (z, z))
        kn_s[0] = jnp.broadcast_to(n0, (8, LANE))
        kn_s[1] = jnp.broadcast_to(n1, (8, LANE))

    qb = q_ref[...] * (ATTN_HEAD_DIM ** -0.5)
    lane_q = lax.broadcasted_iota(I32, (tq, LANE), 1)
    q = qb.astype(F32)
    kd = k_ref[pl.ds(pl.multiple_of(t0, tq), tq), :].astype(F32)
    pick_r = lax.broadcasted_iota(I32, (8, LANE), 0)
    pick = jnp.where(lax.broadcasted_iota(I32, (8, LANE), 1) // half == pick_r, 1.0, 0.0)
    qn_rows = lax.dot_general(pick, q * q, nt, preferred_element_type=F32, precision=lax.Precision.HIGHEST)
    self_rows = lax.dot_general(pick, q * kd, nt, preferred_element_type=F32, precision=lax.Precision.HIGHEST)
    bounds = []
    for c in range(2):
        data = (lane_q < half) if c == 0 else (lane_q >= half)
        qa_s[c] = jnp.where(data, qb, qf_ref[c])
        ref_c = jnp.sqrt(qn_rows[c:c + 1, :] * kn_s[c][0:1, 0:1]) * 1.02
        ref_s[c] = ref_c
        bounds.append(jnp.max(ref_c - self_rows[c:c + 1, :], axis=1, keepdims=True))

    span = jnp.maximum(bounds[0], bounds[1])
    dskip = jnp.minimum((span + (math.log(seq) + SKIP_MARGIN)) / slope, 2.0 * seq)
    t0f = jnp.full((1, 1), t0, I32).astype(F32)
    dlof = jnp.full((1, 1), dlo, I32).astype(F32)
    klo_f = jnp.clip(jnp.floor((t0f + 1.0 - dskip) / tk), 0.0, dlof)
    khi_f = jnp.clip(jnp.ceil((dskip + t0f + (tq - 1.0)) / tk), dlof + ndiag, float(nk))
    klo = jnp.max(klo_f).astype(I32)
    khi = jnp.max(khi_f).astype(I32)
    fixed_ref = jnp.max(jnp.where(span <= FIXED_REF_MAX_RANGE, 1.0, 0.0)).astype(I32)

    m_s[...] = jnp.full(m_s.shape, -jnp.inf, F32)
    acc_s[...] = jnp.zeros(acc_s.shape, F32)
    ones = jnp.ones((ONES_ROWS, tk), BF16)

    def block(ki, side, fixed):
        s0 = pl.multiple_of(ki * tk, tk)
        kblk = k_ref[pl.ds(s0, tk), :]
        vaug = jnp.concatenate([vt_ref[:, pl.ds(s0, tk)], ones], axis=0)
        off = jnp.full((1, tq), t0 - s0, I32).astype(F32) * slope
        cblk = -off if side == 'R' else off
        if side == 'D':
            rel = (lax.broadcasted_iota(I32, (tk, tq), 0) - lax.broadcasted_iota(I32, (tk, tq), 1)
                   + (s0 - t0)).astype(F32)
            corr = jnp.maximum(rel, 0.0) * (-2.0 * slope)
        for c in range(2):
            data = (lane_k < half) if c == 0 else (lane_k >= half)
            kaug = jnp.where(data, kblk, kf_ref[2 * c + (1 if side == 'R' else 0)])
            st = lax.dot_general(kaug, qa_s[c], nt, preferred_element_type=F32)
            if side == 'D':
                st = st + corr
            if fixed:
                p = jnp.exp(st - (ref_s[c] + cblk)).astype(BF16)
                acc_s[c] += jnp.dot(vaug, p, preferred_element_type=F32)
            else:
                m_prev = m_s[c]
                m_new = jnp.maximum(m_prev, jnp.max(st, axis=0, keepdims=True) - cblk)
                p = jnp.exp(st - (m_new + cblk)).astype(BF16)
                alpha = jnp.exp(m_prev - m_new)
                acc_s[c] = acc_s[c] * alpha + jnp.dot(vaug, p, preferred_element_type=F32)
                m_s[c] = m_new

    def sweep(fixed):
        def run(side):
            def body(ki, carry):
                block(ki, side, fixed)
                return carry
            return body

        lax.fori_loop(klo, dlo, run('L'), 0)
        for d in range(ndiag):
            block(dlo + d, 'D', fixed)
        lax.fori_loop(dlo + ndiag, khi, run('R'), 0)

    @pl.when(fixed_ref == 1)
    def _():
        sweep(True)

    @pl.when(fixed_ref == 0)
    def _():
        sweep(False)

    lam = scal_ref[ATTN_HEADS]
    post = scal_ref[ATTN_HEADS + 1]
    a0 = acc_s[0]
    a1 = acc_s[1]
    o = a0[0:ATTN_V_DIM] / a0[ATTN_V_DIM:ATTN_V_DIM + 1] - lam * (a1[0:ATTN_V_DIM] / a1[ATTN_V_DIM:ATTN_V_DIM + 1])
    o = o * lax.rsqrt(jnp.mean(o * o, axis=0, keepdims=True) + SUBLN_EPS) * (w_ref[...] * post)
    o_ref[...] = o.T.astype(o_ref.dtype)


def _position_features(slopes, tq, tk):
    half = ATTN_HEAD_DIM
    sl = slopes.reshape(-1, 1, 1).astype(F32)
    lane = jnp.arange(LANE)[None, None, :]
    one = jnp.ones((1, 1, 1), F32)

    def tile(n, vals, base):
        out = jnp.zeros((slopes.shape[0], n, LANE), F32)
        for i, v in enumerate(vals):
            out = jnp.where(lane == base + i, v, out)
        return out

    def split(n):
        pos = jnp.arange(n, dtype=F32)[None, :, None]
        lo = jnp.mod(pos, float(POS_SPLIT))
        return pos - lo, lo

    qhi, qlo = split(tq)
    khi, klo = split(tk)
    qf, kf = [], []
    for c in range(2):
        base = half if c == 0 else 0
        qf.append(tile(tq, (-sl * qhi, -sl * qlo, one, one), base))
        left = tile(tk, (one, one, sl * khi, sl * klo), base)
        kf += [left, -left]
    return jnp.stack(qf, axis=1).astype(BF16), jnp.stack(kf, axis=1).astype(BF16)


def diff_attention(proj3, vt, scal, subln_w, tq, tk):
    b, s, _ = proj3.shape
    kq, kk = C_Q // LANE, C_K // LANE
    qfeat, kfeat = _position_features(scal[:ATTN_HEADS], tq, tk)
    return pl.pallas_call(
        functools.partial(_attn_kernel, tq=tq, tk=tk, seq=s),
        grid=(b, ATTN_HEADS, s // tq),
        in_specs=[
            pl.BlockSpec(memory_space=pltpu.SMEM),
            pl.BlockSpec((None, tq, LANE), lambda bi, h, qi: (bi, qi, kq + h)),
            pl.BlockSpec((None, s, LANE), lambda bi, h, qi: (bi, 0, kk + h)),
            pl.BlockSpec((ATTN_V_DIM, s), lambda bi, h, qi: (h, bi)),
            pl.BlockSpec((ATTN_V_DIM, 1), lambda bi, h, qi: (0, 0)),
            pl.BlockSpec((None, 2, tq, LANE), lambda bi, h, qi: (h, 0, 0, 0)),
            pl.BlockSpec((None, 4, tk, LANE), lambda bi, h, qi: (h, 0, 0, 0)),
        ],
        out_specs=pl.BlockSpec((None, tq, LANE), lambda bi, h, qi: (bi, qi, h)),
        out_shape=jax.ShapeDtypeStruct((b, s, ATTN_HEADS * ATTN_V_DIM), BF16),
        scratch_shapes=[
            pltpu.VMEM((2, tq, LANE), BF16),
            pltpu.VMEM((2, 1, tq), F32),
            pltpu.VMEM((2, ATTN_V_DIM + ONES_ROWS, tq), F32),
            pltpu.VMEM((2, 8, LANE), F32),
            pltpu.VMEM((2, 1, tq), F32),
        ],
        compiler_params=_cparams(("parallel", "parallel", "arbitrary"), VMEM_LIMIT),
        name="diff_attention",
    )(scal, proj3, proj3, vt, subln_w.reshape(ATTN_V_DIM, 1), qfeat, kfeat)


def _proj_t_kernel(x_ref, nw_ref, wvt_ref, wdt_ref, wdtt_ref, vt_ref, dt_ref, dtt_ref):
    xn = _rms(x_ref[...], nw_ref[...], EPS).astype(BF16)
    nt = (((1,), (1,)), ((), ()))
    vt_ref[...] = lax.dot_general(wvt_ref[...], xn, nt, preferred_element_type=F32).astype(vt_ref.dtype)
    dt_ref[...] = jnp.dot(xn, wdt_ref[...], preferred_element_type=F32)
    dtt_ref[...] = lax.dot_general(wdtt_ref[...], xn, nt, preferred_element_type=F32)


def proj_transposed(x, nw, wvt, wdt, wdtt, tm):
    t, d = x.shape
    nv = wvt.shape[0]
    const = lambda i: (0, 0)
    return pl.pallas_call(
        _proj_t_kernel,
        grid=(t // tm,),
        in_specs=[
            pl.BlockSpec((tm, d), lambda i: (i, 0)),
            pl.BlockSpec((1, d), const),
            pl.BlockSpec((nv, d), const),
            pl.BlockSpec((d, LANE), const),
            pl.BlockSpec((LANE, d), const),
        ],
        out_specs=[pl.BlockSpec((nv, tm), lambda i: (0, i)), pl.BlockSpec((tm, LANE), lambda i: (i, 0)),
                   pl.BlockSpec((LANE, tm), lambda i: (0, i))],
        out_shape=[jax.ShapeDtypeStruct((nv, t), BF16), jax.ShapeDtypeStruct((t, LANE), F32),
                   jax.ShapeDtypeStruct((LANE, t), F32)],
        compiler_params=_cparams(("parallel",), VMEM_LIMIT),
        name="proj_transposed",
    )(x, nw.reshape(1, d), wvt, wdt, wdtt)


HALO = 16


def _conv_kernel(prev_ref, cur_ref, next_ref, w_ref, b_ref, o_ref, ext_ref, *, tm):
    i = pl.program_id(1)
    last = pl.num_programs(1) - 1
    prev = prev_ref[...].astype(F32)
    nxt = next_ref[...].astype(F32)
    ext_ref[pl.ds(0, HALO), :] = jnp.where(i > 0, prev, 0.0)
    ext_ref[pl.ds(HALO, tm), :] = cur_ref[...].astype(F32)
    ext_ref[pl.ds(HALO + tm, HALO), :] = jnp.where(i < last, nxt, 0.0)
    w = w_ref[...]
    acc = jnp.zeros(o_ref.shape, F32) + b_ref[...]
    for j in range(D_CONV):
        acc = acc + ext_ref[pl.ds(HALO - D_CONV // 2 + j, tm), :] * w[j:j + 1, :]
    o_ref[...] = _silu(acc).astype(o_ref.dtype)


def conv_silu(proj3, conv_w, conv_b, tm, tc):
    b, s, _ = proj3.shape
    c0 = C_XBC // tc
    hb = tm // HALO
    nh = s // HALO
    return pl.pallas_call(
        functools.partial(_conv_kernel, tm=tm),
        grid=(b, s // tm, XBC_DIM // tc),
        in_specs=[
            pl.BlockSpec((None, HALO, tc), lambda bi, i, j: (bi, jnp.maximum(i * hb - 1, 0), c0 + j)),
            pl.BlockSpec((None, tm, tc), lambda bi, i, j: (bi, i, c0 + j)),
            pl.BlockSpec((None, HALO, tc), lambda bi, i, j: (bi, jnp.minimum((i + 1) * hb, nh - 1), c0 + j)),
            pl.BlockSpec((D_CONV, tc), lambda bi, i, j: (0, j)),
            pl.BlockSpec((1, tc), lambda bi, i, j: (0, j)),
        ],
        out_specs=pl.BlockSpec((None, tm, tc), lambda bi, i, j: (bi, i, j)),
        out_shape=jax.ShapeDtypeStruct((b, s, XBC_DIM), BF16),
        scratch_shapes=[pltpu.VMEM((tm + 2 * HALO, tc), F32)],
        compiler_params=_cparams(("parallel", "parallel", "parallel"), VMEM_LIMIT),
        name="conv_silu",
    )(proj3, proj3, proj3, conv_w, conv_b.reshape(1, XBC_DIM))


def _expand_heads(a):
    hi = a.astype(BF16)
    lo = (a - hi.astype(F32)).astype(BF16)
    head_of = lax.broadcasted_iota(I32, (SSM_HEADS, D_INNER), 1) // SSM_HEAD_DIM
    e = jnp.where(head_of == lax.broadcasted_iota(I32, (SSM_HEADS, D_INNER), 0), 1.0, 0.0).astype(BF16)
    return jnp.dot(hi, e, preferred_element_type=F32) + jnp.dot(lo, e, preferred_element_type=F32)


def _tri(n, fn):
    r = lax.broadcasted_iota(I32, (n, n), 0)
    c = lax.broadcasted_iota(I32, (n, n), 1)
    return jnp.where(fn(r, c), 1.0, 0.0).astype(F32)


def _dot_hi(a, b):
    return jnp.dot(a, b, preferred_element_type=F32, precision=lax.Precision.HIGHEST)


def _ssd_state_kernel(xf_ref, bf_ref, dtf_ref, xb_ref, bb_ref, dtb_ref, par_ref, hf_ref, hb_ref, sf_s, sb_s):
    j = pl.program_id(1)

    @pl.when(j == 0)
    def _():
        sf_s[...] = jnp.zeros(sf_s.shape, F32)
        sb_s[...] = jnp.zeros(sb_s.shape, F32)

    hf_ref[...] = sf_s[...].astype(hf_ref.dtype)
    hb_ref[...] = sb_s[...].astype(hb_ref.dtype)

    par = par_ref[...]
    lower_incl = _tri(CHUNK, lambda r, c: c <= r)
    lower_strict = _tri(CHUNK, lambda r, c: c < r)

    def one_direction(x_ref, b_ref, dt_ref, s_ref, col0, backward):
        raw = dt_ref[...][:, col0:col0 + SSM_HEADS]
        dt = _softplus(raw + par[0:1, col0:col0 + SSM_HEADS])
        a = dt * par[1:2, col0:col0 + SSM_HEADS]
        if backward:
            excl = _dot_hi(lower_strict, a)
            wgt = dt * jnp.exp(excl)
            total = excl[CHUNK - 1:CHUNK, :] + a[CHUNK - 1:CHUNK, :]
        else:
            cum = _dot_hi(lower_incl, a)
            total = cum[CHUNK - 1:CHUNK, :]
            wgt = dt * jnp.exp(total - cum)
        xw = (x_ref[...].astype(F32) * _expand_heads(wgt)).astype(BF16)
        dec = _expand_heads(jnp.exp(jnp.broadcast_to(total, (8, SSM_HEADS))))[0:1, :]
        bmat = b_ref[...]
        gw = D_INNER // SSM_GROUPS
        for g in range(SSM_GROUPS):
            contrib = lax.dot_general(bmat[:, g * D_STATE:(g + 1) * D_STATE], xw[:, g * gw:(g + 1) * gw],
                                      (((0,), (0,)), ((), ())), preferred_element_type=F32)
            s_ref[g] = s_ref[g] * dec[:, g * gw:(g + 1) * gw] + contrib

    one_direction(xf_ref, bf_ref, dtf_ref, sf_s, 0, False)
    one_direction(xb_ref, bb_ref, dtb_ref, sb_s, SSM_HEADS, True)


def ssd_states(xbc, dt3, par):
    b, s, _ = xbc.shape
    nc = s // CHUNK
    gw = D_INNER // SSM_GROUPS
    xblk = D_INNER // D_INNER
    bcol = D_INNER // (SSM_GROUPS * D_STATE)
    hshape = jax.ShapeDtypeStruct((b, nc, SSM_GROUPS, D_STATE, gw), BF16)
    hspec_f = pl.BlockSpec((None, None, SSM_GROUPS, D_STATE, gw), lambda bi, j: (bi, j, 0, 0, 0))
    hspec_b = pl.BlockSpec((None, None, SSM_GROUPS, D_STATE, gw), lambda bi, j: (bi, nc - 1 - j, 0, 0, 0))
    del xblk
    return pl.pallas_call(
        _ssd_state_kernel,
        grid=(b, nc),
        in_specs=[
            pl.BlockSpec((None, CHUNK, D_INNER), lambda bi, j: (bi, j, 0)),
            pl.BlockSpec((None, CHUNK, SSM_GROUPS * D_STATE), lambda bi, j: (bi, j, bcol)),
            pl.BlockSpec((None, CHUNK, LANE), lambda bi, j: (bi, j, 0)),
            pl.BlockSpec((None, CHUNK, D_INNER), lambda bi, j: (bi, nc - 1 - j, 0)),
            pl.BlockSpec((None, CHUNK, SSM_GROUPS * D_STATE), lambda bi, j: (bi, nc - 1 - j, bcol)),
            pl.BlockSpec((None, CHUNK, LANE), lambda bi, j: (bi, nc - 1 - j, 0)),
            pl.BlockSpec((8, LANE), lambda bi, j: (0, 0)),
        ],
        out_specs=[hspec_f, hspec_b],
        out_shape=[hshape, hshape],
        scratch_shapes=[pltpu.VMEM((SSM_GROUPS, D_STATE, gw), F32), pltpu.VMEM((SSM_GROUPS, D_STATE, gw), F32)],
        compiler_params=_cparams(("parallel", "arbitrary"), VMEM_LIMIT),
        name="ssd_states",
    )(xbc, xbc, dt3, xbc, xbc, dt3, par)


def _ssd_out_kernel(x_ref, b_ref, c_ref, z_ref, dt_ref, dtt_ref, par_ref, part_ref, dsk_ref, nw_ref,
                    hf_ref, hb_ref, o_ref):
    par = par_ref[...]
    part = part_ref[...]
    lower_incl = _tri(CHUNK, lambda r, c: c <= r)
    lower_strict = _tri(CHUNK, lambda r, c: c < r)
    upper_incl = _tri(CHUNK, lambda r, c: r <= c)
    upper_strict = _tri(CHUNK, lambda r, c: r < c)

    dtc = _softplus(dt_ref[...][:, 0:2 * SSM_HEADS] + par[0:1, 0:2 * SSM_HEADS])
    ac = dtc * par[1:2, 0:2 * SSM_HEADS]
    cum_c = _dot_hi(lower_incl, ac)
    excl_c = _dot_hi(lower_strict, ac)
    dtr = _softplus(dtt_ref[...][0:2 * SSM_HEADS, :] + part[0:2 * SSM_HEADS, 0:1])
    ar = dtr * part[0:2 * SSM_HEADS, 1:2]
    cum_r = _dot_hi(ar, upper_incl)
    excl_r = _dot_hi(ar, upper_strict)

    cumf_c = cum_c[:, 0:SSM_HEADS]
    exclb_c = excl_c[:, SSM_HEADS:2 * SSM_HEADS]
    totb = cum_c[CHUNK - 1:CHUNK, SSM_HEADS:2 * SSM_HEADS]
    dtf_r = dtr[0:SSM_HEADS, :]
    dtb_r = dtr[SSM_HEADS:2 * SSM_HEADS, :]
    fwd_r = cum_r[0:SSM_HEADS, :] - jnp.log(dtf_r)
    bwd_r = excl_r[SSM_HEADS:2 * SSM_HEADS, :] + jnp.log(dtb_r)
    diag_r = jnp.log(dtf_r + dtb_r)

    row = lax.broadcasted_iota(I32, (CHUNK, CHUNK), 0)
    col = lax.broadcasted_iota(I32, (CHUNK, CHUNK), 1)
    low = col < row
    diag = col == row
    lane = lax.broadcasted_iota(I32, (CHUNK, LANE), 1)
    first = lane < SSM_HEAD_DIM

    gw = D_INNER // SSM_GROUPS
    hpg = SSM_HEADS // SSM_GROUPS
    scale_f = _expand_heads(jnp.exp(cumf_c))
    scale_b = _expand_heads(jnp.exp(totb - exclb_c))
    for g in range(SSM_GROUPS):
        cg = c_ref[:, g * D_STATE:(g + 1) * D_STATE]
        bg = b_ref[:, g * D_STATE:(g + 1) * D_STATE]
        cb = lax.dot_general(cg, bg, (((1,), (1,)), ((), ())), preferred_element_type=F32)
        inter_f = jnp.dot(cg, hf_ref[g], preferred_element_type=F32)
        inter_b = jnp.dot(cg, hb_ref[g], preferred_element_type=F32)
        parts = []
        for pp in range(hpg // 2):
            gms = []
            for h in (g * hpg + 2 * pp, g * hpg + 2 * pp + 1):
                arg = jnp.where(low, cumf_c[:, h:h + 1] - fwd_r[h:h + 1, :],
                                bwd_r[h:h + 1, :] - exclb_c[:, h:h + 1])
                arg = jnp.where(diag, diag_r[h:h + 1, :], arg)
                gms.append((cb * jnp.exp(arg)).astype(BF16))
            ls = slice(g * gw + pp * LANE, g * gw + (pp + 1) * LANE)
            il = slice(pp * LANE, (pp + 1) * LANE)
            xp = x_ref[:, ls]
            rhs = jnp.concatenate([jnp.where(first, xp, 0), jnp.where(first, 0, xp)], axis=0)
            yp = jnp.dot(jnp.concatenate(gms, axis=1), rhs, preferred_element_type=F32)
            yp = (yp + inter_f[:, il] * scale_f[:, ls] + inter_b[:, il] * scale_b[:, ls]
                  + xp.astype(F32) * dsk_ref[:, ls])
            parts.append(yp * _silu(z_ref[:, ls].astype(F32)))
        yg = jnp.concatenate(parts, axis=1)
        sl = slice(g * gw, (g + 1) * gw)
        o_ref[:, sl] = (yg * lax.rsqrt(jnp.mean(yg * yg, axis=-1, keepdims=True) + EPS)
                        * nw_ref[:, sl]).astype(o_ref.dtype)


def ssd_output(xbc, proj3, dt3, dtt3, par, part, dskip_x, nw, hf, hb):
    b, s, _ = xbc.shape
    nc = s // CHUNK
    gw = D_INNER // SSM_GROUPS
    gs = SSM_GROUPS * D_STATE
    hspec = pl.BlockSpec((None, None, SSM_GROUPS, D_STATE, gw), lambda bi, j: (bi, j, 0, 0, 0))
    return pl.pallas_call(
        _ssd_out_kernel,
        grid=(b, nc),
        in_specs=[
            pl.BlockSpec((None, CHUNK, D_INNER), lambda bi, j: (bi, j, 0)),
            pl.BlockSpec((None, CHUNK, gs), lambda bi, j: (bi, j, D_INNER // gs)),
            pl.BlockSpec((None, CHUNK, gs), lambda bi, j: (bi, j, D_INNER // gs + 1)),
            pl.BlockSpec((None, CHUNK, D_INNER), lambda bi, j: (bi, j, C_Z // D_INNER)),
            pl.BlockSpec((None, CHUNK, LANE), lambda bi, j: (bi, j, 0)),
            pl.BlockSpec((LANE, CHUNK), lambda bi, j: (0, bi * nc + j)),
            pl.BlockSpec((8, LANE), lambda bi, j: (0, 0)),
            pl.BlockSpec((LANE, 8), lambda bi, j: (0, 0)),
            pl.BlockSpec((1, D_INNER), lambda bi, j: (0, 0)),
            pl.BlockSpec((1, D_INNER), lambda bi, j: (0, 0)),
            hspec, hspec,
        ],
        out_specs=pl.BlockSpec((None, CHUNK, D_INNER), lambda bi, j: (bi, j, 0)),
        out_shape=jax.ShapeDtypeStruct((b, s, D_INNER), BF16),
        compiler_params=_cparams(("parallel", "parallel"), VMEM_LIMIT),
        name="ssd_output",
    )(xbc, xbc, xbc, proj3, dt3, dtt3, par, part, dskip_x, nw, hf, hb)


def _merge_kernel(x_ref, a_ref, s_ref, g0_ref, g1_ref, wa_ref, ws_ref, wo_ref, o_ref):
    ya = jnp.dot(a_ref[...], wa_ref[...], preferred_element_type=F32)
    ys = jnp.dot(s_ref[...], ws_ref[...], preferred_element_type=F32)
    g0 = 1.0 / (1.0 + jnp.exp(-g0_ref[...].astype(F32)))
    g1 = 1.0 / (1.0 + jnp.exp(-g1_ref[...].astype(F32)))
    merged = (g0 * ya + g1 * ys).astype(BF16)
    o_ref[...] = x_ref[...] + jnp.dot(merged, wo_ref[...], preferred_element_type=F32)


def merge_out(x, attn, ssm, proj, wa, ws, wo, tm):
    t, d = x.shape
    gb = C_GATE // d
    const = lambda i: (0, 0)
    return pl.pallas_call(
        _merge_kernel,
        grid=(t // tm,),
        in_specs=[
            pl.BlockSpec((tm, d), lambda i: (i, 0)),
            pl.BlockSpec((tm, d), lambda i: (i, 0)),
            pl.BlockSpec((tm, D_INNER), lambda i: (i, 0)),
            pl.BlockSpec((tm, d), lambda i: (i, gb)),
            pl.BlockSpec((tm, d), lambda i: (i, gb + 1)),
            pl.BlockSpec((d, d), const),
            pl.BlockSpec((D_INNER, d), const),
            pl.BlockSpec((d, d), const),
        ],
        out_specs=pl.BlockSpec((tm, d), lambda i: (i, 0)),
        out_shape=jax.ShapeDtypeStruct((t, d), F32),
        compiler_params=_cparams(("parallel",), VMEM_LIMIT),
        name="merge_out",
    )(x, attn, ssm, proj, proj, wa, ws, wo)


def _cross_kernel(x_ref, nw_ref, wq_ref, kv_ref, wo_ref, o_ref):
    x = x_ref[...]
    h = _rms(x, nw_ref[...], EPS).astype(BF16)
    q = jnp.dot(h, wq_ref[...], preferred_element_type=F32).astype(BF16)
    kv = kv_ref[...]
    outs = []
    for hd in range(X_HEADS):
        qh = q[:, hd * X_HEAD_DIM:(hd + 1) * X_HEAD_DIM]
        kh = kv[:, hd * X_HEAD_DIM:(hd + 1) * X_HEAD_DIM]
        vh = kv[:, X_WIDTH + hd * X_HEAD_DIM:X_WIDTH + (hd + 1) * X_HEAD_DIM]
        s = lax.dot_general(qh, kh, (((1,), (1,)), ((), ())), preferred_element_type=F32) * (X_HEAD_DIM ** -0.5)
        s = s - jnp.max(s, axis=-1, keepdims=True)
        p = jnp.exp(s)
        p = (p / jnp.sum(p, axis=-1, keepdims=True)).astype(BF16)
        outs.append(jnp.dot(p, vh, preferred_element_type=F32).astype(BF16))
    o = jnp.concatenate(outs, axis=1)
    o_ref[...] = x + jnp.dot(o, wo_ref[...], preferred_element_type=F32)


def cross_attention(x, nw, wq, kv, wo, seq, tm):
    t, d = x.shape
    per_seq = seq // tm
    const = lambda i: (0, 0)
    return pl.pallas_call(
        _cross_kernel,
        grid=(t // tm,),
        in_specs=[
            pl.BlockSpec((tm, d), lambda i: (i, 0)),
            pl.BlockSpec((1, d), const),
            pl.BlockSpec((d, X_WIDTH), const),
            pl.BlockSpec((None, MEM_TOKENS, 2 * X_WIDTH), lambda i: (i // per_seq, 0, 0)),
            pl.BlockSpec((X_WIDTH, d), const),
        ],
        out_specs=pl.BlockSpec((tm, d), lambda i: (i, 0)),
        out_shape=jax.ShapeDtypeStruct((t, d), F32),
        compiler_params=_cparams(("parallel",), VMEM_LIMIT),
        name="cross_attention",
    )(x, nw.reshape(1, d), wq, kv, wo)


def _router_kernel(x_ref, nw_ref, wrt_ref, br_ref, hb_ref, aff_ref):
    h = _rms(x_ref[...], nw_ref[...], EPS)
    hb_ref[...] = h.astype(hb_ref.dtype)
    logits = lax.dot_general(wrt_ref[...], h, (((1,), (1,)), ((), ())), preferred_element_type=F32,
                             precision=lax.Precision.HIGHEST) + br_ref[...]
    e = jnp.exp(logits - jnp.max(logits, axis=0, keepdims=True))
    aff_ref[...] = e / jnp.sum(e, axis=0, keepdims=True)


def router(x, nw, wrt, br, tm):
    t, d = x.shape
    return pl.pallas_call(
        _router_kernel,
        grid=(t // tm,),
        in_specs=[
            pl.BlockSpec((tm, d), lambda i: (i, 0)),
            pl.BlockSpec((1, d), lambda i: (0, 0)),
            pl.BlockSpec((N_EXPERTS, d), lambda i: (0, 0)),
            pl.BlockSpec((N_EXPERTS, 1), lambda i: (0, 0)),
        ],
        out_specs=[pl.BlockSpec((tm, d), lambda i: (i, 0)), pl.BlockSpec((N_EXPERTS, tm), lambda i: (0, i))],
        out_shape=[jax.ShapeDtypeStruct((t, d), BF16), jax.ShapeDtypeStruct((N_EXPERTS, t), F32)],
        compiler_params=_cparams(("parallel",), VMEM_LIMIT),
        name="router",
    )(x, nw.reshape(1, d), wrt, br.reshape(N_EXPERTS, 1))


TB = 256


def _topk_kernel(aff_ref, pos_ref, gate_ref, roff_ref, *, cap):
    a = aff_ref[...]
    nr = a.shape[1]
    bits = pltpu.bitcast(a, I32)

    def count(mask):
        c = jnp.sum(jnp.where(mask, 1.0, 0.0), axis=2, keepdims=True)
        return jnp.sum(c, axis=1, keepdims=True)

    def body(i, thr):
        cand = thr | jnp.left_shift(jnp.int32(1), 30 - i)
        return jnp.where(count(bits >= cand) >= float(cap), cand, thr)

    thr = lax.fori_loop(0, 31, body, jnp.zeros((N_EXPERTS, 1, 1), I32))
    gt = bits > thr
    eq = bits == thr
    need = float(cap) - count(gt)

    ustrict = _tri(TB, lambda r, c: r < c).astype(BF16)
    ones = jnp.ones((TB, TB), BF16)
    lstrict = _tri(nr, lambda r, c: c < r).astype(BF16)

    def excl_prefix(m):
        mb = m.astype(BF16)
        within = jnp.dot(mb, ustrict, preferred_element_type=F32)
        rowsum = jnp.dot(mb, ones, preferred_element_type=F32)
        rowoff = jnp.dot(lstrict, rowsum.astype(BF16), preferred_element_type=F32)
        return within + rowoff, rowoff

    for e in range(N_EXPERTS):
        eq_e = jnp.where(eq[e], 1.0, 0.0)
        pe, _ = excl_prefix(eq_e)
        keep = jnp.where(pe < need[e], eq_e, 0.0)
        sel = jnp.where(gt[e], 1.0, keep)
        ps, roff = excl_prefix(sel)
        chosen = sel > 0.5
        pos_ref[e] = jnp.where(chosen, ps.astype(I32), -1)
        gate_ref[e] = jnp.where(chosen, a[e], 0.0)
        roff_ref[e] = roff.astype(I32)


def topk_select(aff3, cap):
    e, r, tb = aff3.shape
    full = lambda: (0, 0, 0)
    return pl.pallas_call(
        functools.partial(_topk_kernel, cap=cap),
        grid=(),
        in_specs=[pl.BlockSpec((e, r, tb), full)],
        out_specs=[pl.BlockSpec((e, r, tb), full)] * 3,
        out_shape=[jax.ShapeDtypeStruct((e, r, tb), I32), jax.ShapeDtypeStruct((e, r, tb), F32),
                   jax.ShapeDtypeStruct((e, r, tb), I32)],
        compiler_params=pltpu.CompilerParams(vmem_limit_bytes=VMEM_LIMIT),
        name="topk_select",
    )(aff3)


GATHER_BUFS = 8


GATHER_PARTS = 2


def _ffn_kernel(lo_ref, hi_ref, roff_ref, rend_ref, pos_ref, gate_ref, hb_hbm, wg_ref, wu_ref, wd_ref, y_ref,
                xbuf, sem, xc_s, g_s, *, ts, nj):
    e = pl.program_id(0)
    j = pl.program_id(1)
    nr = pos_ref.shape[0]
    step = e * nj + j
    nsteps = pl.num_programs(0) * nj
    lo = lo_ref[step]
    hi = hi_ref[step]
    xc_s[...] = jnp.zeros(xc_s.shape, F32)
    g_s[...] = jnp.zeros(g_s.shape, F32)

    def fetch(r, slot):
        return pltpu.make_async_copy(hb_hbm.at[pl.ds(r * TB, TB)], xbuf.at[slot], sem.at[slot])

    def prime(first, end):
        for i in range(GATHER_BUFS - 1):
            @pl.when(first + i < end)
            def _(i=i):
                fetch(first + i, i).start()

    @pl.when(step == 0)
    def _():
        prime(lo, hi)

    part = ts // GATHER_PARTS
    slot_ids = lax.broadcasted_iota(I32, (part, TB), 0)

    def body(r, carry):
        idx = r - lo
        slot = idx & (GATHER_BUFS - 1)
        fetch(r, slot).wait()

        @pl.when(r + (GATHER_BUFS - 1) < hi)
        def _():
            fetch(r + (GATHER_BUFS - 1), (idx + (GATHER_BUFS - 1)) & (GATHER_BUFS - 1)).start()

        prow = pos_ref[pl.ds(r, 1), :]
        grow = gate_ref[pl.ds(r, 1), :]
        first = roff_ref[e * nr + r]
        last = rend_ref[e * nr + r]
        for p in range(GATHER_PARTS):
            base = j * ts + p * part

            @pl.when(jnp.logical_and(first < base + part, last > base))
            def _(p=p, base=base):
                hit = prow == slot_ids + base
                sel = jnp.where(hit, 1.0, 0.0).astype(BF16)
                rows = slice(p * part, (p + 1) * part)
                xc_s[rows, :] += jnp.dot(sel, xbuf[slot], preferred_element_type=F32)
                g_s[rows, :] += jnp.sum(jnp.where(hit, grow, 0.0), axis=1, keepdims=True)
        return carry

    lax.fori_loop(lo, hi, body, 0)

    @pl.when(step + 1 < nsteps)
    def _():
        prime(lo_ref[step + 1], hi_ref[step + 1])

    xc = xc_s[...].astype(BF16)
    hg = jnp.dot(xc, wg_ref[...], preferred_element_type=F32)
    hu = jnp.dot(xc, wu_ref[...], preferred_element_type=F32)
    hid = (_silu(hg) * hu).astype(BF16)
    y = jnp.dot(hid, wd_ref[...], preferred_element_type=F32) * g_s[...]
    y_ref[...] = y.astype(y_ref.dtype)


def expert_ffn(lo, hi, roff, rend, pos, gate, hb, wg, wu, wd, cap, ts):
    e, r, tb = pos.shape
    d = hb.shape[1]
    nj = cap // ts
    grid_spec = pltpu.PrefetchScalarGridSpec(
        num_scalar_prefetch=4,
        grid=(e, nj),
        in_specs=[
            pl.BlockSpec((None, r, tb), lambda ei, j, *_: (ei, 0, 0)),
            pl.BlockSpec((None, r, tb), lambda ei, j, *_: (ei, 0, 0)),
            pl.BlockSpec(memory_space=pl.ANY),
            pl.BlockSpec((None, d, D_FF), lambda ei, j, *_: (ei, 0, 0)),
            pl.BlockSpec((None, d, D_FF), lambda ei, j, *_: (ei, 0, 0)),
            pl.BlockSpec((None, D_FF, d), lambda ei, j, *_: (ei, 0, 0)),
        ],
        out_specs=pl.BlockSpec((None, ts, d), lambda ei, j, *_: (ei, j, 0)),
        scratch_shapes=[
            pltpu.VMEM((GATHER_BUFS, TB, d), BF16),
            pltpu.SemaphoreType.DMA((GATHER_BUFS,)),
            pltpu.VMEM((ts, d), F32),
            pltpu.VMEM((ts, 1), F32),
        ],
    )
    return pl.pallas_call(
        functools.partial(_ffn_kernel, ts=ts, nj=nj),
        grid_spec=grid_spec,
        out_shape=jax.ShapeDtypeStruct((e, cap, d), BF16),
        compiler_params=_cparams(("arbitrary", "arbitrary"), VMEM_LIMIT),
        name="expert_ffn",
    )(lo, hi, roff.reshape(-1), rend.reshape(-1), pos, gate, hb, wg, wu, wd)


WIN_MAIN = 128
WIN_OVER = TB + BF16_SUBLANE - WIN_MAIN


def _combine_kernel(start_ref, ostart_ref, ovf_ref, x_ref, post_ref, *rest):
    y_main = rest[:N_EXPERTS]
    y_over = rest[N_EXPERTS:2 * N_EXPERTS]
    o_ref = rest[2 * N_EXPERTS]
    r = pl.program_id(0)
    nr = pl.num_programs(0)
    post = post_ref[...]
    rels = [post[:, e:e + 1] - start_ref[e * nr + r] for e in range(N_EXPERTS)]
    lane = lax.broadcasted_iota(I32, (TB, WIN_MAIN), 1)
    acc = x_ref[...]
    for e in range(0, N_EXPERTS, 2):
        sel = jnp.concatenate([jnp.where(rels[e] == lane, 1.0, 0.0),
                               jnp.where(rels[e + 1] == lane, 1.0, 0.0)], axis=1).astype(BF16)
        rows = jnp.concatenate([y_main[e][...], y_main[e + 1][...]], axis=0)
        acc = acc + jnp.dot(sel, rows, preferred_element_type=F32)
    o_ref[...] = acc

    @pl.when(ovf_ref[r] != 0)
    def _():
        lane_o = lax.broadcasted_iota(I32, (TB, WIN_OVER), 1) + WIN_MAIN
        more = o_ref[...]
        for e in range(N_EXPERTS):
            sel = jnp.where(rels[e] == lane_o, 1.0, 0.0).astype(BF16)
            more = more + jnp.dot(sel, y_over[e][...], preferred_element_type=F32)
        o_ref[...] = more


def combine(x, post, start, ostart, ovf, y):
    t, d = x.shape
    e, cap, _ = y.shape
    nr = t // TB

    def y_spec(ei, rows, which):
        return pl.BlockSpec((pl.Element(rows), pl.Element(d)),
                            lambda r, st, ost, ov: (pl.multiple_of(ei * cap + (st, ost)[which][ei * nr + r],
                                                                   BF16_SUBLANE), 0))

    grid_spec = pltpu.PrefetchScalarGridSpec(
        num_scalar_prefetch=3,
        grid=(nr,),
        in_specs=([pl.BlockSpec((TB, d), lambda r, st, ost, ov: (r, 0)),
                   pl.BlockSpec((TB, e), lambda r, st, ost, ov: (r, 0))]
                  + [y_spec(ei, WIN_MAIN, 0) for ei in range(e)]
                  + [y_spec(ei, WIN_OVER, 1) for ei in range(e)]),
        out_specs=pl.BlockSpec((TB, d), lambda r, st, ost, ov: (r, 0)),
    )
    y2 = y.reshape(e * cap, d)
    return pl.pallas_call(
        _combine_kernel,
        grid_spec=grid_spec,
        out_shape=jax.ShapeDtypeStruct((t, d), F32),
        compiler_params=_cparams(("arbitrary",), VMEM_LIMIT),
        name="combine",
    )(start, ostart, ovf, x, post, *([y2] * (2 * e)))


def _norm_kernel(x_ref, w_ref, o_ref):
    o_ref[...] = _rms(x_ref[...], w_ref[...], EPS)


def final_norm(x, w, tm):
    t, d = x.shape
    return pl.pallas_call(
        _norm_kernel,
        grid=(t // tm,),
        in_specs=[pl.BlockSpec((tm, d), lambda i: (i, 0)), pl.BlockSpec((1, d), lambda i: (0, 0))],
        out_specs=pl.BlockSpec((tm, d), lambda i: (i, 0)),
        out_shape=jax.ShapeDtypeStruct((t, d), F32),
        compiler_params=_cparams(("parallel",)),
        name="final_norm",
    )(x, w.reshape(1, d))


def _tiles(t, s):
    def fit(n, pref):
        while n % pref:
            pref //= 2
        return pref

    cap = max(1, CAPACITY_FACTOR * t // N_EXPERTS)
    return dict(
        tm_proj=fit(t, 1024), tn_proj=PROJ_COLS // 4,
        tq=fit(s, 1024), tk=fit(s, 512),
        tm_conv=fit(s, 512), tc_conv=1024,
        tm_tok=fit(s, 512),
        cap=cap, ts=fit(cap, 256),
    )


def _prep_layer(l, p):
    w_in = p['w_in'][l]
    o_q, o_k, o_v, o_z, o_xbc, o_dt = 1024, 2048, 3072, 5120, 8192, 8256
    w_main = jnp.concatenate([w_in[:, o_v:o_z], w_in[:, o_dt:], w_in[:, o_z:o_xbc], w_in[:, :o_k]],
                             axis=1).astype(BF16)
    w_vt = w_in[:, o_k:o_v].T.astype(BF16)
    w_dt = jnp.pad(w_in[:, o_xbc:o_dt], ((0, 0), (0, LANE - 2 * SSM_HEADS))).astype(BF16)
    dt_bias = p['dt_bias'][l].reshape(-1).astype(F32)
    a_neg = -jnp.exp(p['a_log'][l].astype(F32)).reshape(-1)
    par = jnp.zeros((8, LANE), F32).at[0, :2 * SSM_HEADS].set(dt_bias).at[1, :2 * SSM_HEADS].set(a_neg)
    part = jnp.zeros((LANE, 8), F32).at[:2 * SSM_HEADS, 0].set(dt_bias).at[:2 * SSM_HEADS, 1].set(a_neg)
    lam_init = 0.8 - 0.6 * math.exp(-0.3 * l)
    lam = (jnp.exp(jnp.sum(p['lambda_q1'][l].astype(F32) * p['lambda_k1'][l].astype(F32)))
           - jnp.exp(jnp.sum(p['lambda_q2'][l].astype(F32) * p['lambda_k2'][l].astype(F32))) + lam_init)
    slopes = jnp.exp2(-8.0 * (jnp.arange(ATTN_HEADS, dtype=F32) + 1.0) / ATTN_HEADS)
    scal = jnp.concatenate([slopes, lam.reshape(1), jnp.full((1,), 1.0 - lam_init, F32)]).astype(F32)
    return dict(
        norm_mix=p['norm_mix_w'][l], w_main=w_main, w_vt=w_vt, w_dt=w_dt, w_dtt=w_dt.T, par=par, part=part,
        scal=scal,
        subln=p['attn_subln_w'][l].astype(F32),
        conv_w=p['conv_w'][l].astype(F32), conv_b=p['conv_b'][l].astype(F32),
        dskip_x=jnp.repeat(p['d_skip'][l].astype(F32), SSM_HEAD_DIM).reshape(1, D_INNER),
        ssm_nw=p['ssm_norm_w'][l].astype(F32).reshape(1, D_INNER),
        wa=p['w_attn_branch'][l].astype(BF16), ws=p['w_ssm_branch'][l].astype(BF16),
        wo=p['w_out'][l].astype(BF16),
        norm_cross=p['norm_cross_w'][l], norm_mem=p['norm_mem_w'][l],
        wxq=p['w_xq'][l].astype(BF16), wxkv=p['w_xkv'][l].astype(BF16), wxo=p['w_xo'][l].astype(BF16),
        norm_ffn=p['norm_ffn_w'][l], wrt=p['w_router'][l].astype(F32).T, br=p['b_router'][l].astype(F32),
        wg=p['w_gate'][l].astype(BF16), wu=p['w_up'][l].astype(BF16), wd=p['w_down'][l].astype(BF16),
    )


def _layer(x, mem, lw, b, s):
    t = b * s
    tl = _tiles(t, s)
    proj = norm_matmul(x, lw['norm_mix'], lw['w_main'], BF16, tl['tm_proj'], tl['tn_proj'])
    vt, dt, dtt = proj_transposed(x, lw['norm_mix'], lw['w_vt'], lw['w_dt'], lw['w_dtt'], tl['tm_tok'])
    proj3 = proj.reshape(b, s, PROJ_COLS)
    dt3 = dt.reshape(b, s, LANE)
    attn = diff_attention(proj3, vt, lw['scal'], lw['subln'], tl['tq'], tl['tk'])
    xbc = conv_silu(proj3, lw['conv_w'], lw['conv_b'], tl['tm_conv'], tl['tc_conv'])
    hf, hb = ssd_states(xbc, dt3, lw['par'])
    ssm = ssd_output(xbc, proj3, dt3, dtt, lw['par'], lw['part'], lw['dskip_x'], lw['ssm_nw'], hf, hb)
    x = merge_out(x, attn.reshape(t, D_MODEL), ssm.reshape(t, D_INNER), proj, lw['wa'], lw['ws'], lw['wo'],
                  tl['tm_tok'])
    nm = mem.shape[0] * mem.shape[1]
    kv = norm_matmul(mem.reshape(nm, D_MODEL), lw['norm_mem'], lw['wxkv'], BF16, min(nm, 512), 2 * X_WIDTH)
    x = cross_attention(x, lw['norm_cross'], lw['wxq'], kv.reshape(b, MEM_TOKENS, 2 * X_WIDTH), lw['wxo'], s,
                        tl['tm_tok'])
    cap, ts = tl['cap'], tl['ts']
    nr = t // TB
    hbf, aff = router(x, lw['norm_ffn'], lw['wrt'], lw['br'], tl['tm_tok'])
    pos, gate, roff = topk_select(aff.reshape(N_EXPERTS, nr, TB), cap)
    roff = roff[:, :, 0]
    rend = jnp.concatenate([roff[:, 1:], jnp.full((N_EXPERTS, 1), cap, I32)], axis=1)
    edges = jnp.arange(cap // ts, dtype=I32) * ts
    lo = jnp.sum(rend[:, None, :] <= edges[None, :, None], axis=2).astype(I32).reshape(-1)
    hi = jnp.sum(roff[:, None, :] < (edges + ts)[None, :, None], axis=2).astype(I32).reshape(-1)
    y = expert_ffn(lo, hi, roff, rend, pos, gate, hbf, lw['wg'], lw['wu'], lw['wd'], cap, ts)
    start = jnp.minimum((roff // BF16_SUBLANE) * BF16_SUBLANE, cap - (WIN_MAIN + WIN_OVER)).astype(I32)
    ovf = jnp.any(rend - start > WIN_MAIN, axis=0).astype(I32)
    ostart = jnp.where(ovf[None, :] != 0, start + WIN_MAIN, 0).astype(I32)
    post = jnp.transpose(pos.reshape(N_EXPERTS, t))
    return combine(x, post, start.reshape(-1), ostart.reshape(-1), ovf, y)


def _trunk(x, mem, layers, norm_final_w):
    b, s, d = x.shape
    xt = x.reshape(b * s, d)
    for lw in layers:
        xt = _layer(xt, mem, lw, b, s)
    return final_norm(xt, norm_final_w, 512).reshape(b, s, d)


def kernel(x_prompt, x_sample, mem_prompt, mem_sample, norm_mix_w, w_in, lambda_q1, lambda_k1, lambda_q2,
           lambda_k2, attn_subln_w, conv_w, conv_b, a_log, dt_bias, d_skip, ssm_norm_w, w_attn_branch,
           w_ssm_branch, w_out, norm_cross_w, norm_mem_w, w_xq, w_xkv, w_xo, norm_ffn_w, w_router, b_router,
           w_gate, w_up, w_down, norm_final_w):
    p = dict(norm_mix_w=norm_mix_w, w_in=w_in, lambda_q1=lambda_q1, lambda_k1=lambda_k1, lambda_q2=lambda_q2,
             lambda_k2=lambda_k2, attn_subln_w=attn_subln_w, conv_w=conv_w, conv_b=conv_b, a_log=a_log,
             dt_bias=dt_bias, d_skip=d_skip, ssm_norm_w=ssm_norm_w, w_attn_branch=w_attn_branch,
             w_ssm_branch=w_ssm_branch, w_out=w_out, norm_cross_w=norm_cross_w, norm_mem_w=norm_mem_w,
             w_xq=w_xq, w_xkv=w_xkv, w_xo=w_xo, norm_ffn_w=norm_ffn_w, w_router=w_router, b_router=b_router,
             w_gate=w_gate, w_up=w_up, w_down=w_down)
    layers = [_prep_layer(l, p) for l in range(w_in.shape[0])]
    y_prompt = _trunk(x_prompt, mem_prompt, layers, norm_final_w)
    y_sample = _trunk(x_sample, mem_sample, layers, norm_final_w)
    return (y_prompt, y_sample)
```

```python
import functools
import math

import jax
import jax.numpy as jnp
from jax import lax
from jax.experimental import pallas as pl
from jax.experimental.pallas import tpu as pltpu

F32 = jnp.float32
BF16 = jnp.bfloat16
I32 = jnp.int32

D_MODEL = 1024
ATTN_HEADS = 8
ATTN_HEAD_DIM = 64
ATTN_V_DIM = 128
SSM_HEADS = 32
SSM_HEAD_DIM = 64
SSM_GROUPS = 4
D_STATE = 128
D_INNER = 2048
D_CONV = 5
XBC_DIM = 3072
CHUNK = 128
MEM_TOKENS = 256
X_HEADS = 4
X_HEAD_DIM = 128
X_WIDTH = 512
N_EXPERTS = 16
CAPACITY_FACTOR = 2
D_FF = 2816
EPS = 1e-6
SUBLN_EPS = 1e-5

C_Z = 0
C_GATE = 2048
C_XBC = 4096
C_Q = 7168
C_K = 8192
PROJ_COLS = 9216

LANE = 128
BF16_SUBLANE = 16
VMEM_LIMIT = 56 * 1024 * 1024


def _cparams(sem, vmem=None):
    return pltpu.CompilerParams(dimension_semantics=sem, vmem_limit_bytes=vmem)


def _rms(x, w, eps):
    return x * lax.rsqrt(jnp.mean(x * x, axis=-1, keepdims=True) + eps) * w


def _softplus(x):
    return jnp.maximum(x, 0.0) + jnp.log(1.0 + jnp.exp(-jnp.abs(x)))


def _silu(x):
    return x * (1.0 / (1.0 + jnp.exp(-x)))


def _norm_matmul_kernel(x_ref, nw_ref, w_ref, o_ref, xn_ref):
    @pl.when(pl.program_id(1) == 0)
    def _():
        xn_ref[...] = _rms(x_ref[...], nw_ref[...], EPS).astype(xn_ref.dtype)

    o_ref[...] = jnp.dot(xn_ref[...], w_ref[...], preferred_element_type=F32).astype(o_ref.dtype)


def norm_matmul(x, nw, w, out_dtype, tm, tn):
    t, d = x.shape
    n = w.shape[1]
    return pl.pallas_call(
        _norm_matmul_kernel,
        grid=(t // tm, n // tn),
        in_specs=[
            pl.BlockSpec((tm, d), lambda i, j: (i, 0)),
            pl.BlockSpec((1, d), lambda i, j: (0, 0)),
            pl.BlockSpec((d, tn), lambda i, j: (0, j)),
        ],
        out_specs=pl.BlockSpec((tm, tn), lambda i, j: (i, j)),
        out_shape=jax.ShapeDtypeStruct((t, n), out_dtype),
        scratch_shapes=[pltpu.VMEM((tm, d), BF16)],
        compiler_params=_cparams(("parallel", "arbitrary"), VMEM_LIMIT),
        name="norm_matmul",
    )(x, nw.reshape(1, d), w)


ONES_ROWS = BF16_SUBLANE
SKIP_MARGIN = 17.0
POS_SPLIT = 32


FIXED_REF_MAX_RANGE = 60.0


def _attn_kernel(scal_ref, q_ref, k_ref, vt_ref, w_ref, qf_ref, kf_ref, o_ref, qa_s, m_s, acc_s, kn_s, ref_s,
                 *, tq, tk, seq):
    h = pl.program_id(1)
    qi = pl.program_id(2)
    slope = scal_ref[h]
    nk = seq // tk
    ndiag = tq // tk
    t0 = qi * tq
    dlo = qi * ndiag
    half = ATTN_HEAD_DIM
    nt = (((1,), (1,)), ((), ()))

    lane_k = lax.broadcasted_iota(I32, (tk, LANE), 1)

    @pl.when(qi == 0)
    def _():
        def body(i, carry):
            kc = k_ref[pl.ds(pl.multiple_of(i * tk, tk), tk), :].astype(F32)
            sq = kc * kc
            n0 = jnp.max(jnp.sum(jnp.where(lane_k < half, sq, 0.0), axis=1, keepdims=True), axis=0, keepdims=True)
            n1 = jnp.max(jnp.sum(jnp.where(lane_k >= half, sq, 0.0), axis=1, keepdims=True), axis=0, keepdims=True)
            return jnp.maximum(carry[0], n0), jnp.maximum(carry[1], n1)

        z = jnp.zeros((1, 1), F32)
        n0, n1 = lax.fori_loop(0, nk, body, (z, z))
        kn_s[0] = jnp.broadcast_to(n0, (8, LANE))
        kn_s[1] = jnp.broadcast_to(n1, (8, LANE))

    qb = q_ref[...] * (ATTN_HEAD_DIM ** -0.5)
    lane_q = lax.broadcasted_iota(I32, (tq, LANE), 1)
    q = qb.astype(F32)
    kd = k_ref[pl.ds(pl.multiple_of(t0, tq), tq), :].astype(F32)
    pick_r = lax.broadcasted_iota(I32, (8, LANE), 0)
    pick = jnp.where(lax.broadcasted_iota(I32, (8, LANE), 1) // half == pick_r, 1.0, 0.0)
    qn_rows = lax.dot_general(pick, q * q, nt, preferred_element_type=F32, precision=lax.Precision.HIGHEST)
    self_rows = lax.dot_general(pick, q * kd, nt, preferred_element_type=F32, precision=lax.Precision.HIGHEST)
    bounds = []
    for c in range(2):
        data = (lane_q < half) if c == 0 else (lane_q >= half)
        qa_s[c] = jnp.where(data, qb, qf_ref[c])
        ref_c = jnp.sqrt(qn_rows[c:c + 1, :] * kn_s[c][0:1, 0:1]) * 1.02
        ref_s[c] = ref_c
        bounds.append(jnp.max(ref_c - self_rows[c:c + 1, :], axis=1, keepdims=True))

    span = jnp.maximum(bounds[0], bounds[1])
    count = jnp.minimum(span * 0.0 + scal_ref[ATTN_HEADS + 2 + h], math.log(seq))
    dskip = jnp.minimum((span + count + SKIP_MARGIN) / slope, 2.0 * seq)
    t0f = jnp.full((1, 1), t0, I32).astype(F32)
    dlof = jnp.full((1, 1), dlo, I32).astype(F32)
    klo_f = jnp.clip(jnp.floor((t0f + 1.0 - dskip) / tk), 0.0, dlof)
    khi_f = jnp.clip(jnp.ceil((dskip + t0f + (tq - 1.0)) / tk), dlof + ndiag, float(nk))
    klo = jnp.max(klo_f).astype(I32)
    khi = jnp.max(khi_f).astype(I32)
    fixed_ref = jnp.max(jnp.where(span <= FIXED_REF_MAX_RANGE, 1.0, 0.0)).astype(I32)

    m_s[...] = jnp.full(m_s.shape, -jnp.inf, F32)
    acc_s[...] = jnp.zeros(acc_s.shape, F32)
    ones = jnp.ones((ONES_ROWS, tk), BF16)

    def block(ki, side, fixed, d=0):
        s0 = pl.multiple_of(ki * tk, tk)
        kblk = k_ref[pl.ds(s0, tk), :]
        vaug = jnp.concatenate([vt_ref[:, pl.ds(s0, tk)], ones], axis=0)
        off = jnp.full((1, tq), t0 - s0, I32).astype(F32) * slope
        if side == 'D':
            right = ndiag > 1 and d == ndiag - 1
            c_lo, c_hi = (d * tk, tq) if right else (0, (d + 1) * tk)
            rel = (lax.broadcasted_iota(I32, (tk, c_hi - c_lo), 0) + (d * tk - c_lo)
                   - lax.broadcasted_iota(I32, (tk, c_hi - c_lo), 1)).astype(F32)
            corr = jnp.maximum(-rel if right else rel, 0.0) * (-2.0 * slope)
            feats = 'R' if right else 'L'
        else:
            feats = side
        cblk = -off if feats == 'R' else off
        for c in range(2):
            data = (lane_k < half) if c == 0 else (lane_k >= half)
            kaug = jnp.where(data, kblk, kf_ref[2 * c + (1 if feats == 'R' else 0)])
            st = lax.dot_general(kaug, qa_s[c], nt, preferred_element_type=F32)
            if side == 'D':
                pieces = [st[:, c_lo:c_hi] + corr]
                if c_lo > 0:
                    pieces.insert(0, st[:, :c_lo])
                if c_hi < tq:
                    pieces.append(st[:, c_hi:])
                st = jnp.concatenate(pieces, axis=1) if len(pieces) > 1 else pieces[0]
            if fixed:
                p = jnp.exp(st - (ref_s[c] + cblk)).astype(BF16)
                acc_s[c] += jnp.dot(vaug, p, preferred_element_type=F32)
            else:
                m_prev = m_s[c]
                m_new = jnp.maximum(m_prev, jnp.max(st, axis=0, keepdims=True) - cblk)
                p = jnp.exp(st - (m_new + cblk)).astype(BF16)
                alpha = jnp.exp(m_prev - m_new)
                acc_s[c] = acc_s[c] * alpha + jnp.dot(vaug, p, preferred_element_type=F32)
                m_s[c] = m_new

    def sweep(fixed):
        def run(side):
            def body(ki, carry):
                block(ki, side, fixed)
                return carry
            return body

        lax.fori_loop(klo, dlo, run('L'), 0)
        for d in range(ndiag):
            block(dlo + d, 'D', fixed, d)
        lax.fori_loop(dlo + ndiag, khi, run('R'), 0)

    @pl.when(fixed_ref == 1)
    def _():
        sweep(True)

    @pl.when(fixed_ref == 0)
    def _():
        sweep(False)

    lam = scal_ref[ATTN_HEADS]
    post = scal_ref[ATTN_HEADS + 1]
    a0 = acc_s[0]
    a1 = acc_s[1]
    o = a0[0:ATTN_V_DIM] / a0[ATTN_V_DIM:ATTN_V_DIM + 1] - lam * (a1[0:ATTN_V_DIM] / a1[ATTN_V_DIM:ATTN_V_DIM + 1])
    o = o * lax.rsqrt(jnp.mean(o * o, axis=0, keepdims=True) + SUBLN_EPS) * (w_ref[...] * post)
    o_ref[...] = o.T.astype(o_ref.dtype)


def _position_features(slopes, tq, tk):
    half = ATTN_HEAD_DIM
    sl = slopes.reshape(-1, 1, 1).astype(F32)
    lane = jnp.arange(LANE)[None, None, :]
    one = jnp.ones((1, 1, 1), F32)

    def tile(n, vals, base):
        out = jnp.zeros((slopes.shape[0], n, LANE), F32)
        for i, v in enumerate(vals):
            out = jnp.where(lane == base + i, v, out)
        return out

    def split(n):
        pos = jnp.arange(n, dtype=F32)[None, :, None]
        lo = jnp.mod(pos, float(POS_SPLIT))
        return pos - lo, lo

    qhi, qlo = split(tq)
    khi, klo = split(tk)
    qf, kf = [], []
    for c in range(2):
        base = half if c == 0 else 0
        qf.append(tile(tq, (-sl * qhi, -sl * qlo, one, one), base))
        left = tile(tk, (one, one, sl * khi, sl * klo), base)
        kf += [left, -left]
    return jnp.stack(qf, axis=1).astype(BF16), jnp.stack(kf, axis=1).astype(BF16)


def diff_attention(proj3, vt, scal, subln_w, tq, tk):
    b, s, _ = proj3.shape
    kq, kk = C_Q // LANE, C_K // LANE
    qfeat, kfeat = _position_features(scal[:ATTN_HEADS], tq, tk)
    return pl.pallas_call(
        functools.partial(_attn_kernel, tq=tq, tk=tk, seq=s),
        grid=(b, ATTN_HEADS, s // tq),
        in_specs=[
            pl.BlockSpec(memory_space=pltpu.SMEM),
            pl.BlockSpec((None, tq, LANE), lambda bi, h, qi: (bi, qi, kq + h)),
            pl.BlockSpec((None, s, LANE), lambda bi, h, qi: (bi, 0, kk + h)),
            pl.BlockSpec((ATTN_V_DIM, s), lambda bi, h, qi: (h, bi)),
            pl.BlockSpec((ATTN_V_DIM, 1), lambda bi, h, qi: (0, 0)),
            pl.BlockSpec((None, 2, tq, LANE), lambda bi, h, qi: (h, 0, 0, 0)),
            pl.BlockSpec((None, 4, tk, LANE), lambda bi, h, qi: (h, 0, 0, 0)),
        ],
        out_specs=pl.BlockSpec((None, tq, LANE), lambda bi, h, qi: (bi, qi, h)),
        out_shape=jax.ShapeDtypeStruct((b, s, ATTN_HEADS * ATTN_V_DIM), BF16),
        scratch_shapes=[
            pltpu.VMEM((2, tq, LANE), BF16),
            pltpu.VMEM((2, 1, tq), F32),
            pltpu.VMEM((2, ATTN_V_DIM + ONES_ROWS, tq), F32),
            pltpu.VMEM((2, 8, LANE), F32),
            pltpu.VMEM((2, 1, tq), F32),
        ],
        compiler_params=_cparams(("parallel", "parallel", "arbitrary"), VMEM_LIMIT),
        name="diff_attention",
    )(scal, proj3, proj3, vt, subln_w.reshape(ATTN_V_DIM, 1), qfeat, kfeat)


def _proj_t_kernel(x_ref, nw_ref, wvt_ref, wdt_ref, wdtt_ref, vt_ref, dt_ref, dtt_ref):
    xn = _rms(x_ref[...], nw_ref[...], EPS).astype(BF16)
    nt = (((1,), (1,)), ((), ()))
    vt_ref[...] = lax.dot_general(wvt_ref[...], xn, nt, preferred_element_type=F32).astype(vt_ref.dtype)
    dt_ref[...] = jnp.dot(xn, wdt_ref[...], preferred_element_type=F32)
    dtt_ref[...] = lax.dot_general(wdtt_ref[...], xn, nt, preferred_element_type=F32)


def proj_transposed(x, nw, wvt, wdt, wdtt, tm):
    t, d = x.shape
    nv = wvt.shape[0]
    const = lambda i: (0, 0)
    return pl.pallas_call(
        _proj_t_kernel,
        grid=(t // tm,),
        in_specs=[
            pl.BlockSpec((tm, d), lambda i: (i, 0)),
            pl.BlockSpec((1, d), const),
            pl.BlockSpec((nv, d), const),
            pl.BlockSpec((d, LANE), const),
            pl.BlockSpec((LANE, d), const),
        ],
        out_specs=[pl.BlockSpec((nv, tm), lambda i: (0, i)), pl.BlockSpec((tm, LANE), lambda i: (i, 0)),
                   pl.BlockSpec((LANE, tm), lambda i: (0, i))],
        out_shape=[jax.ShapeDtypeStruct((nv, t), BF16), jax.ShapeDtypeStruct((t, LANE), F32),
                   jax.ShapeDtypeStruct((LANE, t), F32)],
        compiler_params=_cparams(("parallel",), VMEM_LIMIT),
        name="proj_transposed",
    )(x, nw.reshape(1, d), wvt, wdt, wdtt)


HALO = 16


def _conv_kernel(prev_ref, cur_ref, next_ref, w_ref, b_ref, o_ref, ext_ref, *, tm):
    i = pl.program_id(1)
    last = pl.num_programs(1) - 1
    prev = prev_ref[...].astype(F32)
    nxt = next_ref[...].astype(F32)
    ext_ref[pl.ds(0, HALO), :] = jnp.where(i > 0, prev, 0.0)
    ext_ref[pl.ds(HALO, tm), :] = cur_ref[...].astype(F32)
    ext_ref[pl.ds(HALO + tm, HALO), :] = jnp.where(i < last, nxt, 0.0)
    w = w_ref[...]
    acc = jnp.zeros(o_ref.shape, F32) + b_ref[...]
    for j in range(D_CONV):
        acc = acc + ext_ref[pl.ds(HALO - D_CONV // 2 + j, tm), :] * w[j:j + 1, :]
    o_ref[...] = _silu(acc).astype(o_ref.dtype)


def conv_silu(proj3, conv_w, conv_b, tm, tc):
    b, s, _ = proj3.shape
    c0 = C_XBC // tc
    hb = tm // HALO
    nh = s // HALO
    return pl.pallas_call(
        functools.partial(_conv_kernel, tm=tm),
        grid=(b, s // tm, XBC_DIM // tc),
        in_specs=[
            pl.BlockSpec((None, HALO, tc), lambda bi, i, j: (bi, jnp.maximum(i * hb - 1, 0), c0 + j)),
            pl.BlockSpec((None, tm, tc), lambda bi, i, j: (bi, i, c0 + j)),
            pl.BlockSpec((None, HALO, tc), lambda bi, i, j: (bi, jnp.minimum((i + 1) * hb, nh - 1), c0 + j)),
            pl.BlockSpec((D_CONV, tc), lambda bi, i, j: (0, j)),
            pl.BlockSpec((1, tc), lambda bi, i, j: (0, j)),
        ],
        out_specs=pl.BlockSpec((None, tm, tc), lambda bi, i, j: (bi, i, j)),
        out_shape=jax.ShapeDtypeStruct((b, s, XBC_DIM), BF16),
        scratch_shapes=[pltpu.VMEM((tm + 2 * HALO, tc), F32)],
        compiler_params=_cparams(("parallel", "parallel", "parallel"), VMEM_LIMIT),
        name="conv_silu",
    )(proj3, proj3, proj3, conv_w, conv_b.reshape(1, XBC_DIM))


def _expand_heads(a):
    hi = a.astype(BF16)
    lo = (a - hi.astype(F32)).astype(BF16)
    head_of = lax.broadcasted_iota(I32, (SSM_HEADS, D_INNER), 1) // SSM_HEAD_DIM
    e = jnp.where(head_of == lax.broadcasted_iota(I32, (SSM_HEADS, D_INNER), 0), 1.0, 0.0).astype(BF16)
    return jnp.dot(hi, e, preferred_element_type=F32) + jnp.dot(lo, e, preferred_element_type=F32)


def _tri(n, fn):
    r = lax.broadcasted_iota(I32, (n, n), 0)
    c = lax.broadcasted_iota(I32, (n, n), 1)
    return jnp.where(fn(r, c), 1.0, 0.0).astype(F32)


def _dot_hi(a, b):
    return jnp.dot(a, b, preferred_element_type=F32, precision=lax.Precision.HIGHEST)


def _ssd_state_kernel(xf_ref, bf_ref, dtf_ref, xb_ref, bb_ref, dtb_ref, par_ref, hf_ref, hb_ref, sf_s, sb_s):
    j = pl.program_id(1)

    @pl.when(j == 0)
    def _():
        sf_s[...] = jnp.zeros(sf_s.shape, F32)
        sb_s[...] = jnp.zeros(sb_s.shape, F32)

    hf_ref[...] = sf_s[...].astype(hf_ref.dtype)
    hb_ref[...] = sb_s[...].astype(hb_ref.dtype)

    par = par_ref[...]
    lower_incl = _tri(CHUNK, lambda r, c: c <= r)
    lower_strict = _tri(CHUNK, lambda r, c: c < r)

    def one_direction(x_ref, b_ref, dt_ref, s_ref, col0, backward):
        raw = dt_ref[...][:, col0:col0 + SSM_HEADS]
        dt = _softplus(raw + par[0:1, col0:col0 + SSM_HEADS])
        a = dt * par[1:2, col0:col0 + SSM_HEADS]
        if backward:
            excl = _dot_hi(lower_strict, a)
            wgt = dt * jnp.exp(excl)
            total = excl[CHUNK - 1:CHUNK, :] + a[CHUNK - 1:CHUNK, :]
        else:
            cum = _dot_hi(lower_incl, a)
            total = cum[CHUNK - 1:CHUNK, :]
            wgt = dt * jnp.exp(total - cum)
        xw = (x_ref[...].astype(F32) * _expand_heads(wgt)).astype(BF16)
        dec = _expand_heads(jnp.exp(jnp.broadcast_to(total, (8, SSM_HEADS))))[0:1, :]
        bmat = b_ref[...]
        gw = D_INNER // SSM_GROUPS
        for g in range(SSM_GROUPS):
            contrib = lax.dot_general(bmat[:, g * D_STATE:(g + 1) * D_STATE], xw[:, g * gw:(g + 1) * gw],
                                      (((0,), (0,)), ((), ())), preferred_element_type=F32)
            s_ref[g] = s_ref[g] * dec[:, g * gw:(g + 1) * gw] + contrib

    one_direction(xf_ref, bf_ref, dtf_ref, sf_s, 0, False)
    one_direction(xb_ref, bb_ref, dtb_ref, sb_s, SSM_HEADS, True)


def ssd_states(xbc, dt3, par):
    b, s, _ = xbc.shape
    nc = s // CHUNK
    gw = D_INNER // SSM_GROUPS
    xblk = D_INNER // D_INNER
    bcol = D_INNER // (SSM_GROUPS * D_STATE)
    hshape = jax.ShapeDtypeStruct((b, nc, SSM_GROUPS, D_STATE, gw), BF16)
    hspec_f = pl.BlockSpec((None, None, SSM_GROUPS, D_STATE, gw), lambda bi, j: (bi, j, 0, 0, 0))
    hspec_b = pl.BlockSpec((None, None, SSM_GROUPS, D_STATE, gw), lambda bi, j: (bi, nc - 1 - j, 0, 0, 0))
    del xblk
    return pl.pallas_call(
        _ssd_state_kernel,
        grid=(b, nc),
        in_specs=[
            pl.BlockSpec((None, CHUNK, D_INNER), lambda bi, j: (bi, j, 0)),
            pl.BlockSpec((None, CHUNK, SSM_GROUPS * D_STATE), lambda bi, j: (bi, j, bcol)),
            pl.BlockSpec((None, CHUNK, LANE), lambda bi, j: (bi, j, 0)),
            pl.BlockSpec((None, CHUNK, D_INNER), lambda bi, j: (bi, nc - 1 - j, 0)),
            pl.BlockSpec((None, CHUNK, SSM_GROUPS * D_STATE), lambda bi, j: (bi, nc - 1 - j, bcol)),
            pl.BlockSpec((None, CHUNK, LANE), lambda bi, j: (bi, nc - 1 - j, 0)),
            pl.BlockSpec((8, LANE), lambda bi, j: (0, 0)),
        ],
        out_specs=[hspec_f, hspec_b],
        out_shape=[hshape, hshape],
        scratch_shapes=[pltpu.VMEM((SSM_GROUPS, D_STATE, gw), F32), pltpu.VMEM((SSM_GROUPS, D_STATE, gw), F32)],
        compiler_params=_cparams(("parallel", "arbitrary"), VMEM_LIMIT),
        name="ssd_states",
    )(xbc, xbc, dt3, xbc, xbc, dt3, par)


def _ssd_out_kernel(x_ref, b_ref, c_ref, z_ref, dt_ref, dtt_ref, par_ref, part_ref, dsk_ref, nw_ref,
                    hf_ref, hb_ref, o_ref):
    par = par_ref[...]
    part = part_ref[...]
    lower_incl = _tri(CHUNK, lambda r, c: c <= r)
    lower_strict = _tri(CHUNK, lambda r, c: c < r)
    upper_incl = _tri(CHUNK, lambda r, c: r <= c)
    upper_strict = _tri(CHUNK, lambda r, c: r < c)

    dtc = _softplus(dt_ref[...][:, 0:2 * SSM_HEADS] + par[0:1, 0:2 * SSM_HEADS])
    ac = dtc * par[1:2, 0:2 * SSM_HEADS]
    cum_c = _dot_hi(lower_incl, ac)
    excl_c = _dot_hi(lower_strict, ac)
    dtr = _softplus(dtt_ref[...][0:2 * SSM_HEADS, :] + part[0:2 * SSM_HEADS, 0:1])
    ar = dtr * part[0:2 * SSM_HEADS, 1:2]
    cum_r = _dot_hi(ar, upper_incl)
    excl_r = _dot_hi(ar, upper_strict)

    cumf_c = cum_c[:, 0:SSM_HEADS]
    exclb_c = excl_c[:, SSM_HEADS:2 * SSM_HEADS]
    totb = cum_c[CHUNK - 1:CHUNK, SSM_HEADS:2 * SSM_HEADS]
    dtf_r = dtr[0:SSM_HEADS, :]
    dtb_r = dtr[SSM_HEADS:2 * SSM_HEADS, :]
    fwd_r = cum_r[0:SSM_HEADS, :] - jnp.log(dtf_r)
    bwd_r = excl_r[SSM_HEADS:2 * SSM_HEADS, :] + jnp.log(dtb_r)
    diag_r = jnp.log(dtf_r + dtb_r)

    row = lax.broadcasted_iota(I32, (CHUNK, CHUNK), 0)
    col = lax.broadcasted_iota(I32, (CHUNK, CHUNK), 1)
    low = col < row
    diag = col == row
    lane = lax.broadcasted_iota(I32, (CHUNK, LANE), 1)
    first = lane < SSM_HEAD_DIM

    gw = D_INNER // SSM_GROUPS
    hpg = SSM_HEADS // SSM_GROUPS
    scale_f = _expand_heads(jnp.exp(cumf_c))
    scale_b = _expand_heads(jnp.exp(totb - exclb_c))
    for g in range(SSM_GROUPS):
        cg = c_ref[:, g * D_STATE:(g + 1) * D_STATE]
        bg = b_ref[:, g * D_STATE:(g + 1) * D_STATE]
        cb = lax.dot_general(cg, bg, (((1,), (1,)), ((), ())), preferred_element_type=F32)
        inter_f = jnp.dot(cg, hf_ref[g], preferred_element_type=F32)
        inter_b = jnp.dot(cg, hb_ref[g], preferred_element_type=F32)
        parts = []
        for pp in range(hpg // 2):
            gms = []
            for h in (g * hpg + 2 * pp, g * hpg + 2 * pp + 1):
                arg = jnp.where(low, cumf_c[:, h:h + 1] - fwd_r[h:h + 1, :],
                                bwd_r[h:h + 1, :] - exclb_c[:, h:h + 1])
                arg = jnp.where(diag, diag_r[h:h + 1, :], arg)
                gms.append((cb * jnp.exp(arg)).astype(BF16))
            ls = slice(g * gw + pp * LANE, g * gw + (pp + 1) * LANE)
            il = slice(pp * LANE, (pp + 1) * LANE)
            xp = x_ref[:, ls]
            rhs = jnp.concatenate([jnp.where(first, xp, 0), jnp.where(first, 0, xp)], axis=0)
            yp = jnp.dot(jnp.concatenate(gms, axis=1), rhs, preferred_element_type=F32)
            yp = (yp + inter_f[:, il] * scale_f[:, ls] + inter_b[:, il] * scale_b[:, ls]
                  + xp.astype(F32) * dsk_ref[:, ls])
            parts.append(yp * _silu(z_ref[:, ls].astype(F32)))
        yg = jnp.concatenate(parts, axis=1)
        sl = slice(g * gw, (g + 1) * gw)
        o_ref[:, sl] = (yg * lax.rsqrt(jnp.mean(yg * yg, axis=-1, keepdims=True) + EPS)
                        * nw_ref[:, sl]).astype(o_ref.dtype)


def ssd_output(xbc, proj3, dt3, dtt3, par, part, dskip_x, nw, hf, hb):
    b, s, _ = xbc.shape
    nc = s // CHUNK
    gw = D_INNER // SSM_GROUPS
    gs = SSM_GROUPS * D_STATE
    hspec = pl.BlockSpec((None, None, SSM_GROUPS, D_STATE, gw), lambda bi, j: (bi, j, 0, 0, 0))
    return pl.pallas_call(
        _ssd_out_kernel,
        grid=(b, nc),
        in_specs=[
            pl.BlockSpec((None, CHUNK, D_INNER), lambda bi, j: (bi, j, 0)),
            pl.BlockSpec((None, CHUNK, gs), lambda bi, j: (bi, j, D_INNER // gs)),
            pl.BlockSpec((None, CHUNK, gs), lambda bi, j: (bi, j, D_INNER // gs + 1)),
            pl.BlockSpec((None, CHUNK, D_INNER), lambda bi, j: (bi, j, C_Z // D_INNER)),
            pl.BlockSpec((None, CHUNK, LANE), lambda bi, j: (bi, j, 0)),
            pl.BlockSpec((LANE, CHUNK), lambda bi, j: (0, bi * nc + j)),
            pl.BlockSpec((8, LANE), lambda bi, j: (0, 0)),
            pl.BlockSpec((LANE, 8), lambda bi, j: (0, 0)),
            pl.BlockSpec((1, D_INNER), lambda bi, j: (0, 0)),
            pl.BlockSpec((1, D_INNER), lambda bi, j: (0, 0)),
            hspec, hspec,
        ],
        out_specs=pl.BlockSpec((None, CHUNK, D_INNER), lambda bi, j: (bi, j, 0)),
        out_shape=jax.ShapeDtypeStruct((b, s, D_INNER), BF16),
        compiler_params=_cparams(("parallel", "parallel"), VMEM_LIMIT),
        name="ssd_output",
    )(xbc, xbc, xbc, proj3, dt3, dtt3, par, part, dskip_x, nw, hf, hb)


def _merge_kernel(x_ref, a_ref, s_ref, g0_ref, g1_ref, wa_ref, ws_ref, wo_ref, o_ref):
    ya = jnp.dot(a_ref[...], wa_ref[...], preferred_element_type=F32)
    ys = jnp.dot(s_ref[...], ws_ref[...], preferred_element_type=F32)
    g0 = 1.0 / (1.0 + jnp.exp(-g0_ref[...].astype(F32)))
    g1 = 1.0 / (1.0 + jnp.exp(-g1_ref[...].astype(F32)))
    merged = (g0 * ya + g1 * ys).astype(BF16)
    o_ref[...] = x_ref[...] + jnp.dot(merged, wo_ref[...], preferred_element_type=F32)


def merge_out(x, attn, ssm, proj, wa, ws, wo, tm):
    t, d = x.shape
    gb = C_GATE // d
    const = lambda i: (0, 0)
    return pl.pallas_call(
        _merge_kernel,
        grid=(t // tm,),
        in_specs=[
            pl.BlockSpec((tm, d), lambda i: (i, 0)),
            pl.BlockSpec((tm, d), lambda i: (i, 0)),
            pl.BlockSpec((tm, D_INNER), lambda i: (i, 0)),
            pl.BlockSpec((tm, d), lambda i: (i, gb)),
            pl.BlockSpec((tm, d), lambda i: (i, gb + 1)),
            pl.BlockSpec((d, d), const),
            pl.BlockSpec((D_INNER, d), const),
            pl.BlockSpec((d, d), const),
        ],
        out_specs=pl.BlockSpec((tm, d), lambda i: (i, 0)),
        out_shape=jax.ShapeDtypeStruct((t, d), F32),
        compiler_params=_cparams(("parallel",), VMEM_LIMIT),
        name="merge_out",
    )(x, attn, ssm, proj, proj, wa, ws, wo)


def _cross_kernel(x_ref, nw_ref, wq_ref, kv_ref, wo_ref, o_ref):
    x = x_ref[...]
    h = _rms(x, nw_ref[...], EPS).astype(BF16)
    q = jnp.dot(h, wq_ref[...], preferred_element_type=F32).astype(BF16)
    kv = kv_ref[...]
    outs = []
    for hd in range(X_HEADS):
        qh = q[:, hd * X_HEAD_DIM:(hd + 1) * X_HEAD_DIM]
        kh = kv[:, hd * X_HEAD_DIM:(hd + 1) * X_HEAD_DIM]
        vh = kv[:, X_WIDTH + hd * X_HEAD_DIM:X_WIDTH + (hd + 1) * X_HEAD_DIM]
        s = lax.dot_general(qh, kh, (((1,), (1,)), ((), ())), preferred_element_type=F32) * (X_HEAD_DIM ** -0.5)
        s = s - jnp.max(s, axis=-1, keepdims=True)
        p = jnp.exp(s)
        p = (p / jnp.sum(p, axis=-1, keepdims=True)).astype(BF16)
        outs.append(jnp.dot(p, vh, preferred_element_type=F32).astype(BF16))
    o = jnp.concatenate(outs, axis=1)
    o_ref[...] = x + jnp.dot(o, wo_ref[...], preferred_element_type=F32)


def cross_attention(x, nw, wq, kv, wo, seq, tm):
    t, d = x.shape
    per_seq = seq // tm
    const = lambda i: (0, 0)
    return pl.pallas_call(
        _cross_kernel,
        grid=(t // tm,),
        in_specs=[
            pl.BlockSpec((tm, d), lambda i: (i, 0)),
            pl.BlockSpec((1, d), const),
            pl.BlockSpec((d, X_WIDTH), const),
            pl.BlockSpec((None, MEM_TOKENS, 2 * X_WIDTH), lambda i: (i // per_seq, 0, 0)),
            pl.BlockSpec((X_WIDTH, d), const),
        ],
        out_specs=pl.BlockSpec((tm, d), lambda i: (i, 0)),
        out_shape=jax.ShapeDtypeStruct((t, d), F32),
        compiler_params=_cparams(("parallel",), VMEM_LIMIT),
        name="cross_attention",
    )(x, nw.reshape(1, d), wq, kv, wo)


def _router_kernel(x_ref, nw_ref, wrt_ref, br_ref, hb_ref, aff_ref):
    h = _rms(x_ref[...], nw_ref[...], EPS)
    hb_ref[...] = h.astype(hb_ref.dtype)
    logits = lax.dot_general(wrt_ref[...], h, (((1,), (1,)), ((), ())), preferred_element_type=F32,
                             precision=lax.Precision.HIGHEST) + br_ref[...]
    e = jnp.exp(logits - jnp.max(logits, axis=0, keepdims=True))
    aff_ref[...] = e / jnp.sum(e, axis=0, keepdims=True)


def router(x, nw, wrt, br, tm):
    t, d = x.shape
    return pl.pallas_call(
        _router_kernel,
        grid=(t // tm,),
        in_specs=[
            pl.BlockSpec((tm, d), lambda i: (i, 0)),
            pl.BlockSpec((1, d), lambda i: (0, 0)),
            pl.BlockSpec((N_EXPERTS, d), lambda i: (0, 0)),
            pl.BlockSpec((N_EXPERTS, 1), lambda i: (0, 0)),
        ],
        out_specs=[pl.BlockSpec((tm, d), lambda i: (i, 0)), pl.BlockSpec((N_EXPERTS, tm), lambda i: (0, i))],
        out_shape=[jax.ShapeDtypeStruct((t, d), BF16), jax.ShapeDtypeStruct((N_EXPERTS, t), F32)],
        compiler_params=_cparams(("parallel",), VMEM_LIMIT),
        name="router",
    )(x, nw.reshape(1, d), wrt, br.reshape(N_EXPERTS, 1))


TB = 256


def _topk_kernel(aff_ref, pos_ref, gate_ref, roff_ref, *, cap):
    a = aff_ref[...]
    nr = a.shape[1]
    bits = pltpu.bitcast(a, I32)

    def count(mask):
        c = jnp.sum(jnp.where(mask, 1.0, 0.0), axis=2, keepdims=True)
        return jnp.sum(c, axis=1, keepdims=True)

    def body(i, thr):
        cand = thr | jnp.left_shift(jnp.int32(1), 30 - i)
        return jnp.where(count(bits >= cand) >= float(cap), cand, thr)

    thr = lax.fori_loop(0, 31, body, jnp.zeros((N_EXPERTS, 1, 1), I32))
    gt = bits > thr
    eq = bits == thr
    need = float(cap) - count(gt)

    ustrict = _tri(TB, lambda r, c: r < c).astype(BF16)
    ones = jnp.ones((TB, TB), BF16)
    lstrict = _tri(nr, lambda r, c: c < r).astype(BF16)

    def excl_prefix(m):
        mb = m.astype(BF16)
        within = jnp.dot(mb, ustrict, preferred_element_type=F32)
        rowsum = jnp.dot(mb, ones, preferred_element_type=F32)
        rowoff = jnp.dot(lstrict, rowsum.astype(BF16), preferred_element_type=F32)
        return within + rowoff, rowoff

    for e in range(N_EXPERTS):
        eq_e = jnp.where(eq[e], 1.0, 0.0)
        pe, _ = excl_prefix(eq_e)
        keep = jnp.where(pe < need[e], eq_e, 0.0)
        sel = jnp.where(gt[e], 1.0, keep)
        ps, roff = excl_prefix(sel)
        chosen = sel > 0.5
        pos_ref[e] = jnp.where(chosen, ps.astype(I32), -1)
        gate_ref[e] = jnp.where(chosen, a[e], 0.0)
        roff_ref[e] = roff.astype(I32)


def topk_select(aff3, cap):
    e, r, tb = aff3.shape
    full = lambda: (0, 0, 0)
    return pl.pallas_call(
        functools.partial(_topk_kernel, cap=cap),
        grid=(),
        in_specs=[pl.BlockSpec((e, r, tb), full)],
        out_specs=[pl.BlockSpec((e, r, tb), full)] * 3,
        out_shape=[jax.ShapeDtypeStruct((e, r, tb), I32), jax.ShapeDtypeStruct((e, r, tb), F32),
                   jax.ShapeDtypeStruct((e, r, tb), I32)],
        compiler_params=pltpu.CompilerParams(vmem_limit_bytes=VMEM_LIMIT),
        name="topk_select",
    )(aff3)


GATHER_BUFS = 8


GATHER_PARTS = 2


def _ffn_kernel(lo_ref, hi_ref, roff_ref, rend_ref, pos_ref, gate_ref, hb_hbm, wg_ref, wu_ref, wd_ref, y_ref,
                xbuf, sem, xc_s, g_s, *, ts, nj):
    e = pl.program_id(0)
    j = pl.program_id(1)
    nr = pos_ref.shape[0]
    step = e * nj + j
    nsteps = pl.num_programs(0) * nj
    lo = lo_ref[step]
    hi = hi_ref[step]
    xc_s[...] = jnp.zeros(xc_s.shape, F32)
    g_s[...] = jnp.zeros(g_s.shape, F32)

    def fetch(r, slot):
        return pltpu.make_async_copy(hb_hbm.at[pl.ds(r * TB, TB)], xbuf.at[slot], sem.at[slot])

    def prime(first, end):
        for i in range(GATHER_BUFS - 1):
            @pl.when(first + i < end)
            def _(i=i):
                fetch(first + i, i).start()

    @pl.when(step == 0)
    def _():
        prime(lo, hi)

    part = ts // GATHER_PARTS
    slot_ids = lax.broadcasted_iota(I32, (part, TB), 0)

    def body(r, carry):
        idx = r - lo
        slot = idx & (GATHER_BUFS - 1)
        fetch(r, slot).wait()

        @pl.when(r + (GATHER_BUFS - 1) < hi)
        def _():
            fetch(r + (GATHER_BUFS - 1), (idx + (GATHER_BUFS - 1)) & (GATHER_BUFS - 1)).start()

        prow = pos_ref[pl.ds(r, 1), :]
        grow = gate_ref[pl.ds(r, 1), :]
        first = roff_ref[e * nr + r]
        last = rend_ref[e * nr + r]
        for p in range(GATHER_PARTS):
            base = j * ts + p * part

            @pl.when(jnp.logical_and(first < base + part, last > base))
            def _(p=p, base=base):
                hit = prow == slot_ids + base
                sel = jnp.where(hit, 1.0, 0.0).astype(BF16)
                rows = slice(p * part, (p + 1) * part)
                xc_s[rows, :] += jnp.dot(sel, xbuf[slot], preferred_element_type=F32)
                g_s[rows, :] += jnp.sum(jnp.where(hit, grow, 0.0), axis=1, keepdims=True)
        return carry

    lax.fori_loop(lo, hi, body, 0)

    @pl.when(step + 1 < nsteps)
    def _():
        prime(lo_ref[step + 1], hi_ref[step + 1])

    xc = xc_s[...].astype(BF16)
    hg = jnp.dot(xc, wg_ref[...], preferred_element_type=F32)
    hu = jnp.dot(xc, wu_ref[...], preferred_element_type=F32)
    hid = (_silu(hg) * hu).astype(BF16)
    y = jnp.dot(hid, wd_ref[...], preferred_element_type=F32) * g_s[...]
    y_ref[...] = y.astype(y_ref.dtype)


def expert_ffn(lo, hi, roff, rend, pos, gate, hb, wg, wu, wd, cap, ts):
    e, r, tb = pos.shape
    d = hb.shape[1]
    nj = cap // ts
    grid_spec = pltpu.PrefetchScalarGridSpec(
        num_scalar_prefetch=4,
        grid=(e, nj),
        in_specs=[
            pl.BlockSpec((None, r, tb), lambda ei, j, *_: (ei, 0, 0)),
            pl.BlockSpec((None, r, tb), lambda ei, j, *_: (ei, 0, 0)),
            pl.BlockSpec(memory_space=pl.ANY),
            pl.BlockSpec((None, d, D_FF), lambda ei, j, *_: (ei, 0, 0)),
            pl.BlockSpec((None, d, D_FF), lambda ei, j, *_: (ei, 0, 0)),
            pl.BlockSpec((None, D_FF, d), lambda ei, j, *_: (ei, 0, 0)),
        ],
        out_specs=pl.BlockSpec((None, ts, d), lambda ei, j, *_: (ei, j, 0)),
        scratch_shapes=[
            pltpu.VMEM((GATHER_BUFS, TB, d), BF16),
            pltpu.SemaphoreType.DMA((GATHER_BUFS,)),
            pltpu.VMEM((ts, d), F32),
            pltpu.VMEM((ts, 1), F32),
        ],
    )
    return pl.pallas_call(
        functools.partial(_ffn_kernel, ts=ts, nj=nj),
        grid_spec=grid_spec,
        out_shape=jax.ShapeDtypeStruct((e, cap, d), BF16),
        compiler_params=_cparams(("arbitrary", "arbitrary"), VMEM_LIMIT),
        name="expert_ffn",
    )(lo, hi, roff.reshape(-1), rend.reshape(-1), pos, gate, hb, wg, wu, wd)


WIN_MAIN = 128
WIN_OVER = TB + BF16_SUBLANE - WIN_MAIN


def _combine_kernel(start_ref, ostart_ref, ovf_ref, x_ref, post_ref, nw_ref, *rest, final):
    y_main = rest[:N_EXPERTS]
    y_over = rest[N_EXPERTS:2 * N_EXPERTS]
    o_ref = rest[2 * N_EXPERTS]
    r = pl.program_id(0)
    nr = pl.num_programs(0)
    post = post_ref[...]
    rels = [post[:, e:e + 1] - start_ref[e * nr + r] for e in range(N_EXPERTS)]
    lane = lax.broadcasted_iota(I32, (TB, WIN_MAIN), 1)
    acc = x_ref[...]
    for e in range(0, N_EXPERTS, 2):
        sel = jnp.concatenate([jnp.where(rels[e] == lane, 1.0, 0.0),
                               jnp.where(rels[e + 1] == lane, 1.0, 0.0)], axis=1).astype(BF16)
        rows = jnp.concatenate([y_main[e][...], y_main[e + 1][...]], axis=0)
        acc = acc + jnp.dot(sel, rows, preferred_element_type=F32)
    o_ref[...] = acc

    @pl.when(ovf_ref[r] != 0)
    def _():
        lane_o = lax.broadcasted_iota(I32, (TB, WIN_OVER), 1) + WIN_MAIN
        more = o_ref[...]
        for e in range(N_EXPERTS):
            sel = jnp.where(rels[e] == lane_o, 1.0, 0.0).astype(BF16)
            more = more + jnp.dot(sel, y_over[e][...], preferred_element_type=F32)
        o_ref[...] = more

    if final:
        o_ref[...] = _rms(o_ref[...], nw_ref[...], EPS)


def combine(x, post, start, ostart, ovf, y, norm_w, final):
    t, d = x.shape
    e, cap, _ = y.shape
    nr = t // TB

    def y_spec(ei, rows, which):
        return pl.BlockSpec((pl.Element(rows), pl.Element(d)),
                            lambda r, st, ost, ov: (pl.multiple_of(ei * cap + (st, ost)[which][ei * nr + r],
                                                                   BF16_SUBLANE), 0))

    grid_spec = pltpu.PrefetchScalarGridSpec(
        num_scalar_prefetch=3,
        grid=(nr,),
        in_specs=([pl.BlockSpec((TB, d), lambda r, st, ost, ov: (r, 0)),
                   pl.BlockSpec((TB, e), lambda r, st, ost, ov: (r, 0)),
                   pl.BlockSpec((1, d), lambda r, st, ost, ov: (0, 0))]
                  + [y_spec(ei, WIN_MAIN, 0) for ei in range(e)]
                  + [y_spec(ei, WIN_OVER, 1) for ei in range(e)]),
        out_specs=pl.BlockSpec((TB, d), lambda r, st, ost, ov: (r, 0)),
    )
    y2 = y.reshape(e * cap, d)
    return pl.pallas_call(
        functools.partial(_combine_kernel, final=final),
        grid_spec=grid_spec,
        out_shape=jax.ShapeDtypeStruct((t, d), F32),
        compiler_params=_cparams(("arbitrary",), VMEM_LIMIT),
        name="combine",
    )(start, ostart, ovf, x, post, norm_w.reshape(1, d), *([y2] * (2 * e)))


def _tiles(t, s):
    def fit(n, pref):
        while n % pref:
            pref //= 2
        return pref

    cap = max(1, CAPACITY_FACTOR * t // N_EXPERTS)
    return dict(
        tm_proj=fit(t, 1024), tn_proj=PROJ_COLS // 4,
        tq=fit(s, 1024), tk=fit(s, 512),
        tm_conv=fit(s, 512), tc_conv=1024,
        tm_tok=fit(s, 512),
        cap=cap, ts=fit(cap, 256),
    )


def _prep_layer(l, p):
    w_in = p['w_in'][l]
    o_q, o_k, o_v, o_z, o_xbc, o_dt = 1024, 2048, 3072, 5120, 8192, 8256
    w_main = jnp.concatenate([w_in[:, o_v:o_z], w_in[:, o_dt:], w_in[:, o_z:o_xbc], w_in[:, :o_k]],
                             axis=1).astype(BF16)
    w_vt = w_in[:, o_k:o_v].T.astype(BF16)
    w_dt = jnp.pad(w_in[:, o_xbc:o_dt], ((0, 0), (0, LANE - 2 * SSM_HEADS))).astype(BF16)
    dt_bias = p['dt_bias'][l].reshape(-1).astype(F32)
    a_neg = -jnp.exp(p['a_log'][l].astype(F32)).reshape(-1)
    par = jnp.zeros((8, LANE), F32).at[0, :2 * SSM_HEADS].set(dt_bias).at[1, :2 * SSM_HEADS].set(a_neg)
    part = jnp.zeros((LANE, 8), F32).at[:2 * SSM_HEADS, 0].set(dt_bias).at[:2 * SSM_HEADS, 1].set(a_neg)
    lam_init = 0.8 - 0.6 * math.exp(-0.3 * l)
    lam = (jnp.exp(jnp.sum(p['lambda_q1'][l].astype(F32) * p['lambda_k1'][l].astype(F32)))
           - jnp.exp(jnp.sum(p['lambda_q2'][l].astype(F32) * p['lambda_k2'][l].astype(F32))) + lam_init)
    slopes = jnp.exp2(-8.0 * (jnp.arange(ATTN_HEADS, dtype=F32) + 1.0) / ATTN_HEADS)
    tail = jnp.log(2.0 / (1.0 - jnp.exp(-slopes)))
    scal = jnp.concatenate([slopes, lam.reshape(1), jnp.full((1,), 1.0 - lam_init, F32), tail]).astype(F32)
    return dict(
        norm_mix=p['norm_mix_w'][l], w_main=w_main, w_vt=w_vt, w_dt=w_dt, w_dtt=w_dt.T, par=par, part=part,
        scal=scal,
        subln=p['attn_subln_w'][l].astype(F32),
        conv_w=p['conv_w'][l].astype(F32), conv_b=p['conv_b'][l].astype(F32),
        dskip_x=jnp.repeat(p['d_skip'][l].astype(F32), SSM_HEAD_DIM).reshape(1, D_INNER),
        ssm_nw=p['ssm_norm_w'][l].astype(F32).reshape(1, D_INNER),
        wa=p['w_attn_branch'][l].astype(BF16), ws=p['w_ssm_branch'][l].astype(BF16),
        wo=p['w_out'][l].astype(BF16),
        norm_cross=p['norm_cross_w'][l], norm_mem=p['norm_mem_w'][l],
        wxq=p['w_xq'][l].astype(BF16), wxkv=p['w_xkv'][l].astype(BF16), wxo=p['w_xo'][l].astype(BF16),
        norm_ffn=p['norm_ffn_w'][l], wrt=p['w_router'][l].astype(F32).T, br=p['b_router'][l].astype(F32),
        wg=p['w_gate'][l].astype(BF16), wu=p['w_up'][l].astype(BF16), wd=p['w_down'][l].astype(BF16),
    )


def _layer(x, mem, lw, b, s, norm_final_w, last):
    t = b * s
    tl = _tiles(t, s)
    proj = norm_matmul(x, lw['norm_mix'], lw['w_main'], BF16, tl['tm_proj'], tl['tn_proj'])
    vt, dt, dtt = proj_transposed(x, lw['norm_mix'], lw['w_vt'], lw['w_dt'], lw['w_dtt'], tl['tm_tok'])
    proj3 = proj.reshape(b, s, PROJ_COLS)
    dt3 = dt.reshape(b, s, LANE)
    attn = diff_attention(proj3, vt, lw['scal'], lw['subln'], tl['tq'], tl['tk'])
    xbc = conv_silu(proj3, lw['conv_w'], lw['conv_b'], tl['tm_conv'], tl['tc_conv'])
    hf, hb = ssd_states(xbc, dt3, lw['par'])
    ssm = ssd_output(xbc, proj3, dt3, dtt, lw['par'], lw['part'], lw['dskip_x'], lw['ssm_nw'], hf, hb)
    x = merge_out(x, attn.reshape(t, D_MODEL), ssm.reshape(t, D_INNER), proj, lw['wa'], lw['ws'], lw['wo'],
                  tl['tm_tok'])
    nm = mem.shape[0] * mem.shape[1]
    kv = norm_matmul(mem.reshape(nm, D_MODEL), lw['norm_mem'], lw['wxkv'], BF16, min(nm, 512), 2 * X_WIDTH)
    x = cross_attention(x, lw['norm_cross'], lw['wxq'], kv.reshape(b, MEM_TOKENS, 2 * X_WIDTH), lw['wxo'], s,
                        tl['tm_tok'])
    cap, ts = tl['cap'], tl['ts']
    nr = t // TB
    hbf, aff = router(x, lw['norm_ffn'], lw['wrt'], lw['br'], tl['tm_tok'])
    pos, gate, roff = topk_select(aff.reshape(N_EXPERTS, nr, TB), cap)
    roff = roff[:, :, 0]
    rend = jnp.concatenate([roff[:, 1:], jnp.full((N_EXPERTS, 1), cap, I32)], axis=1)
    edges = jnp.arange(cap // ts, dtype=I32) * ts
    lo = jnp.sum(rend[:, None, :] <= edges[None, :, None], axis=2).astype(I32).reshape(-1)
    hi = jnp.sum(roff[:, None, :] < (edges + ts)[None, :, None], axis=2).astype(I32).reshape(-1)
    y = expert_ffn(lo, hi, roff, rend, pos, gate, hbf, lw['wg'], lw['wu'], lw['wd'], cap, ts)
    start = jnp.minimum((roff // BF16_SUBLANE) * BF16_SUBLANE, cap - (WIN_MAIN + WIN_OVER)).astype(I32)
    ovf = jnp.any(rend - start > WIN_MAIN, axis=0).astype(I32)
    ostart = jnp.where(ovf[None, :] != 0, start + WIN_MAIN, 0).astype(I32)
    post = jnp.transpose(pos.reshape(N_EXPERTS, t))
    return combine(x, post, start.reshape(-1), ostart.reshape(-1), ovf, y, norm_final_w, last)


def _trunk(x, mem, layers, norm_final_w):
    b, s, d = x.shape
    xt = x.reshape(b * s, d)
    for l, lw in enumerate(layers):
        xt = _layer(xt, mem, lw, b, s, norm_final_w, l == len(layers) - 1)
    return xt.reshape(b, s, d)


def kernel(x_prompt, x_sample, mem_prompt, mem_sample, norm_mix_w, w_in, lambda_q1, lambda_k1, lambda_q2,
           lambda_k2, attn_subln_w, conv_w, conv_b, a_log, dt_bias, d_skip, ssm_norm_w, w_attn_branch,
           w_ssm_branch, w_out, norm_cross_w, norm_mem_w, w_xq, w_xkv, w_xo, norm_ffn_w, w_router, b_router,
           w_gate, w_up, w_down, norm_final_w):
    p = dict(norm_mix_w=norm_mix_w, w_in=w_in, lambda_q1=lambda_q1, lambda_k1=lambda_k1, lambda_q2=lambda_q2,
             lambda_k2=lambda_k2, attn_subln_w=attn_subln_w, conv_w=conv_w, conv_b=conv_b, a_log=a_log,
             dt_bias=dt_bias, d_skip=d_skip, ssm_norm_w=ssm_norm_w, w_attn_branch=w_attn_branch,
             w_ssm_branch=w_ssm_branch, w_out=w_out, norm_cross_w=norm_cross_w, norm_mem_w=norm_mem_w,
             w_xq=w_xq, w_xkv=w_xkv, w_xo=w_xo, norm_ffn_w=norm_ffn_w, w_router=w_router, b_router=b_router,
             w_gate=w_gate, w_up=w_up, w_down=w_down)
    layers = [_prep_layer(l, p) for l in range(w_in.shape[0])]
    y_prompt = _trunk(x_prompt, mem_prompt, layers, norm_final_w)
    y_sample = _trunk(x_sample, mem_sample, layers, norm_final_w)
    return (y_prompt, y_sample)
```

```python
import functools
import math

import jax
import jax.numpy as jnp
from jax import lax
from jax.experimental import pallas as pl
from jax.experimental.pallas import tpu as pltpu

F32 = jnp.float32
BF16 = jnp.bfloat16
I32 = jnp.int32

D_MODEL = 1024
ATTN_HEADS = 8
ATTN_HEAD_DIM = 64
ATTN_V_DIM = 128
SSM_HEADS = 32
SSM_HEAD_DIM = 64
SSM_GROUPS = 4
D_STATE = 128
D_INNER = 2048
D_CONV = 5
XBC_DIM = 3072
CHUNK = 128
MEM_TOKENS = 256
X_HEADS = 4
X_HEAD_DIM = 128
X_WIDTH = 512
N_EXPERTS = 16
CAPACITY_FACTOR = 2
D_FF = 2816
EPS = 1e-6
SUBLN_EPS = 1e-5

C_Z = 0
C_GATE = 2048
C_XBC = 4096
C_Q = 7168
C_K = 8192
PROJ_COLS = 9216

LANE = 128
BF16_SUBLANE = 16
VMEM_LIMIT = 56 * 1024 * 1024


def _cparams(sem, vmem=None):
    return pltpu.CompilerParams(dimension_semantics=sem, vmem_limit_bytes=vmem)


def _rms(x, w, eps):
    return x * lax.rsqrt(jnp.mean(x * x, axis=-1, keepdims=True) + eps) * w


def _softplus(x):
    return jnp.maximum(x, 0.0) + jnp.log(1.0 + jnp.exp(-jnp.abs(x)))


def _silu(x):
    return x * (1.0 / (1.0 + jnp.exp2(x * (-1.0 / math.log(2.0)))))


def _norm_matmul_kernel(x_ref, nw_ref, w_ref, o_ref, xn_ref):
    @pl.when(pl.program_id(1) == 0)
    def _():
        xn_ref[...] = _rms(x_ref[...], nw_ref[...], EPS).astype(xn_ref.dtype)

    o_ref[...] = jnp.dot(xn_ref[...], w_ref[...], preferred_element_type=F32).astype(o_ref.dtype)


def norm_matmul(x, nw, w, out_dtype, tm, tn):
    t, d = x.shape
    n = w.shape[1]
    return pl.pallas_call(
        _norm_matmul_kernel,
        grid=(t // tm, n // tn),
        in_specs=[
            pl.BlockSpec((tm, d), lambda i, j: (i, 0)),
            pl.BlockSpec((1, d), lambda i, j: (0, 0)),
            pl.BlockSpec((d, tn), lambda i, j: (0, j)),
        ],
        out_specs=pl.BlockSpec((tm, tn), lambda i, j: (i, j)),
        out_shape=jax.ShapeDtypeStruct((t, n), out_dtype),
        scratch_shapes=[pltpu.VMEM((tm, d), BF16)],
        compiler_params=_cparams(("parallel", "arbitrary"), VMEM_LIMIT),
        name="norm_matmul",
    )(x, nw.reshape(1, d), w)


ONES_ROWS = BF16_SUBLANE
SKIP_MARGIN = 17.0
POS_SPLIT = 32


FIXED_REF_MAX_RANGE = 60.0


def _attn_kernel(scal_ref, q_ref, k_ref, vt_ref, w_ref, qf_ref, kf_ref, o_ref, qa_s, m_s, acc_s, kn_s, ref_s,
                 *, tq, tk, seq):
    h = pl.program_id(1)
    qi = pl.program_id(2)
    slope = scal_ref[h]
    nk = seq // tk
    ndiag = tq // tk
    t0 = qi * tq
    dlo = qi * ndiag
    half = ATTN_HEAD_DIM
    nt = (((1,), (1,)), ((), ()))

    lane_k = lax.broadcasted_iota(I32, (tk, LANE), 1)

    @pl.when(qi == 0)
    def _():
        def body(i, carry):
            kc = k_ref[pl.ds(pl.multiple_of(i * tk, tk), tk), :].astype(F32)
            sq = kc * kc
            n0 = jnp.max(jnp.sum(jnp.where(lane_k < half, sq, 0.0), axis=1, keepdims=True), axis=0, keepdims=True)
            n1 = jnp.max(jnp.sum(jnp.where(lane_k >= half, sq, 0.0), axis=1, keepdims=True), axis=0, keepdims=True)
            return jnp.maximum(carry[0], n0), jnp.maximum(carry[1], n1)

        z = jnp.zeros((1, 1), F32)
        n0, n1 = lax.fori_loop(0, nk, body, (z, z))
        kn_s[0] = jnp.broadcast_to(n0, (8, LANE))
        kn_s[1] = jnp.broadcast_to(n1, (8, LANE))

    qb = q_ref[...] * (ATTN_HEAD_DIM ** -0.5)
    lane_q = lax.broadcasted_iota(I32, (tq, LANE), 1)
    q = qb.astype(F32)
    kd = k_ref[pl.ds(pl.multiple_of(t0, tq), tq), :].astype(F32)
    pick_r = lax.broadcasted_iota(I32, (8, LANE), 0)
    pick = jnp.where(lax.broadcasted_iota(I32, (8, LANE), 1) // half == pick_r, 1.0, 0.0)
    qn_rows = lax.dot_general(pick, q * q, nt, preferred_element_type=F32, precision=lax.Precision.HIGHEST)
    self_rows = lax.dot_general(pick, q * kd, nt, preferred_element_type=F32, precision=lax.Precision.HIGHEST)
    bounds = []
    for c in range(2):
        data = (lane_q < half) if c == 0 else (lane_q >= half)
        qa_s[c] = jnp.where(data, qb, qf_ref[c])
        ref_c = jnp.sqrt(qn_rows[c:c + 1, :] * kn_s[c][0:1, 0:1]) * 1.02
        ref_s[c] = ref_c
        bounds.append(jnp.max(ref_c - self_rows[c:c + 1, :], axis=1, keepdims=True))

    span = jnp.maximum(bounds[0], bounds[1])
    count = jnp.minimum(span * 0.0 + scal_ref[ATTN_HEADS + 2 + h], math.log(seq))
    dskip = jnp.minimum((span + count + SKIP_MARGIN) / slope, 2.0 * seq)
    t0f = jnp.full((1, 1), t0, I32).astype(F32)
    dlof = jnp.full((1, 1), dlo, I32).astype(F32)
    klo_f = jnp.clip(jnp.floor((t0f + 1.0 - dskip) / tk), 0.0, dlof)
    khi_f = jnp.clip(jnp.ceil((dskip + t0f + (tq - 1.0)) / tk), dlof + ndiag, float(nk))
    klo = jnp.max(klo_f).astype(I32)
    khi = jnp.max(khi_f).astype(I32)
    fixed_ref = jnp.max(jnp.where(span <= FIXED_REF_MAX_RANGE, 1.0, 0.0)).astype(I32)

    m_s[...] = jnp.full(m_s.shape, -jnp.inf, F32)
    acc_s[...] = jnp.zeros(acc_s.shape, F32)
    ones = jnp.ones((ONES_ROWS, tk), BF16)

    def block(ki, side, fixed, d=0):
        s0 = pl.multiple_of(ki * tk, tk)
        kblk = k_ref[pl.ds(s0, tk), :]
        vaug = jnp.concatenate([vt_ref[:, pl.ds(s0, tk)], ones], axis=0)
        off = jnp.full((1, tq), t0 - s0, I32).astype(F32) * slope
        if side == 'D':
            right = ndiag > 1 and d == ndiag - 1
            c_lo, c_hi = (d * tk, tq) if right else (0, (d + 1) * tk)
            rel = (lax.broadcasted_iota(I32, (tk, c_hi - c_lo), 0) + (d * tk - c_lo)
                   - lax.broadcasted_iota(I32, (tk, c_hi - c_lo), 1)).astype(F32)
            corr = jnp.maximum(-rel if right else rel, 0.0) * (-2.0 * slope)
            feats = 'R' if right else 'L'
        else:
            feats = side
        cblk = -off if feats == 'R' else off
        for c in range(2):
            data = (lane_k < half) if c == 0 else (lane_k >= half)
            kaug = jnp.where(data, kblk, kf_ref[2 * c + (1 if feats == 'R' else 0)])
            st = lax.dot_general(kaug, qa_s[c], nt, preferred_element_type=F32)
            if side == 'D':
                pieces = [st[:, c_lo:c_hi] + corr]
                if c_lo > 0:
                    pieces.insert(0, st[:, :c_lo])
                if c_hi < tq:
                    pieces.append(st[:, c_hi:])
                st = jnp.concatenate(pieces, axis=1) if len(pieces) > 1 else pieces[0]
            if fixed:
                p = jnp.exp(st - (ref_s[c] + cblk)).astype(BF16)
                acc_s[c] += jnp.dot(vaug, p, preferred_element_type=F32)
            else:
                m_prev = m_s[c]
                m_new = jnp.maximum(m_prev, jnp.max(st, axis=0, keepdims=True) - cblk)
                p = jnp.exp(st - (m_new + cblk)).astype(BF16)
                alpha = jnp.exp(m_prev - m_new)
                acc_s[c] = acc_s[c] * alpha + jnp.dot(vaug, p, preferred_element_type=F32)
                m_s[c] = m_new

    def sweep(fixed):
        def run(side):
            def body(ki, carry):
                block(ki, side, fixed)
                return carry
            return body

        lax.fori_loop(klo, dlo, run('L'), 0)
        for d in range(ndiag):
            block(dlo + d, 'D', fixed, d)
        lax.fori_loop(dlo + ndiag, khi, run('R'), 0)

    @pl.when(fixed_ref == 1)
    def _():
        sweep(True)

    @pl.when(fixed_ref == 0)
    def _():
        sweep(False)

    lam = scal_ref[ATTN_HEADS]
    post = scal_ref[ATTN_HEADS + 1]
    a0 = acc_s[0]
    a1 = acc_s[1]
    o = a0[0:ATTN_V_DIM] / a0[ATTN_V_DIM:ATTN_V_DIM + 1] - lam * (a1[0:ATTN_V_DIM] / a1[ATTN_V_DIM:ATTN_V_DIM + 1])
    o = o * lax.rsqrt(jnp.mean(o * o, axis=0, keepdims=True) + SUBLN_EPS) * (w_ref[...] * post)
    o_ref[...] = o.T.astype(o_ref.dtype)


def _position_features(slopes, tq, tk):
    half = ATTN_HEAD_DIM
    sl = slopes.reshape(-1, 1, 1).astype(F32)
    lane = jnp.arange(LANE)[None, None, :]
    one = jnp.ones((1, 1, 1), F32)

    def tile(n, vals, base):
        out = jnp.zeros((slopes.shape[0], n, LANE), F32)
        for i, v in enumerate(vals):
            out = jnp.where(lane == base + i, v, out)
        return out

    def split(n):
        pos = jnp.arange(n, dtype=F32)[None, :, None]
        lo = jnp.mod(pos, float(POS_SPLIT))
        return pos - lo, lo

    qhi, qlo = split(tq)
    khi, klo = split(tk)
    qf, kf = [], []
    for c in range(2):
        base = half if c == 0 else 0
        qf.append(tile(tq, (-sl * qhi, -sl * qlo, one, one), base))
        left = tile(tk, (one, one, sl * khi, sl * klo), base)
        kf += [left, -left]
    return jnp.stack(qf, axis=1).astype(BF16), jnp.stack(kf, axis=1).astype(BF16)


def diff_attention(proj3, vt, scal, subln_w, tq, tk):
    b, s, _ = proj3.shape
    kq, kk = C_Q // LANE, C_K // LANE
    qfeat, kfeat = _position_features(scal[:ATTN_HEADS], tq, tk)
    return pl.pallas_call(
        functools.partial(_attn_kernel, tq=tq, tk=tk, seq=s),
        grid=(b, ATTN_HEADS, s // tq),
        in_specs=[
            pl.BlockSpec(memory_space=pltpu.SMEM),
            pl.BlockSpec((None, tq, LANE), lambda bi, h, qi: (bi, qi, kq + h)),
            pl.BlockSpec((None, s, LANE), lambda bi, h, qi: (bi, 0, kk + h)),
            pl.BlockSpec((ATTN_V_DIM, s), lambda bi, h, qi: (h, bi)),
            pl.BlockSpec((ATTN_V_DIM, 1), lambda bi, h, qi: (0, 0)),
            pl.BlockSpec((None, 2, tq, LANE), lambda bi, h, qi: (h, 0, 0, 0)),
            pl.BlockSpec((None, 4, tk, LANE), lambda bi, h, qi: (h, 0, 0, 0)),
        ],
        out_specs=pl.BlockSpec((None, tq, LANE), lambda bi, h, qi: (bi, qi, h)),
        out_shape=jax.ShapeDtypeStruct((b, s, ATTN_HEADS * ATTN_V_DIM), BF16),
        scratch_shapes=[
            pltpu.VMEM((2, tq, LANE), BF16),
            pltpu.VMEM((2, 1, tq), F32),
            pltpu.VMEM((2, ATTN_V_DIM + ONES_ROWS, tq), F32),
            pltpu.VMEM((2, 8, LANE), F32),
            pltpu.VMEM((2, 1, tq), F32),
        ],
        compiler_params=_cparams(("parallel", "parallel", "arbitrary"), VMEM_LIMIT),
        name="diff_attention",
    )(scal, proj3, proj3, vt, subln_w.reshape(ATTN_V_DIM, 1), qfeat, kfeat)


def _proj_t_kernel(x_ref, nw_ref, wvt_ref, wdt_ref, wdtt_ref, vt_ref, dt_ref, dtt_ref):
    xn = _rms(x_ref[...], nw_ref[...], EPS).astype(BF16)
    nt = (((1,), (1,)), ((), ()))
    vt_ref[...] = lax.dot_general(wvt_ref[...], xn, nt, preferred_element_type=F32).astype(vt_ref.dtype)
    dt_ref[...] = jnp.dot(xn, wdt_ref[...], preferred_element_type=F32)
    dtt_ref[...] = lax.dot_general(wdtt_ref[...], xn, nt, preferred_element_type=F32)


def proj_transposed(x, nw, wvt, wdt, wdtt, tm):
    t, d = x.shape
    nv = wvt.shape[0]
    const = lambda i: (0, 0)
    return pl.pallas_call(
        _proj_t_kernel,
        grid=(t // tm,),
        in_specs=[
            pl.BlockSpec((tm, d), lambda i: (i, 0)),
            pl.BlockSpec((1, d), const),
            pl.BlockSpec((nv, d), const),
            pl.BlockSpec((d, LANE), const),
            pl.BlockSpec((LANE, d), const),
        ],
        out_specs=[pl.BlockSpec((nv, tm), lambda i: (0, i)), pl.BlockSpec((tm, LANE), lambda i: (i, 0)),
                   pl.BlockSpec((LANE, tm), lambda i: (0, i))],
        out_shape=[jax.ShapeDtypeStruct((nv, t), BF16), jax.ShapeDtypeStruct((t, LANE), F32),
                   jax.ShapeDtypeStruct((LANE, t), F32)],
        compiler_params=_cparams(("parallel",), VMEM_LIMIT),
        name="proj_transposed",
    )(x, nw.reshape(1, d), wvt, wdt, wdtt)


HALO = 16


CONV_SUB = 128


def _conv_kernel(prev_ref, cur_ref, next_ref, w_ref, b_ref, o_ref, ext_ref, *, tm):
    i = pl.program_id(1)
    last = pl.num_programs(1) - 1
    zero = jnp.zeros(prev_ref.shape, BF16)
    ext_ref[pl.ds(0, HALO), :] = jnp.where(i > 0, prev_ref[...], zero)
    ext_ref[pl.ds(HALO, tm), :] = cur_ref[...]
    ext_ref[pl.ds(HALO + tm, HALO), :] = jnp.where(i < last, next_ref[...], zero)
    w = w_ref[...]
    bias = b_ref[...]
    taps = [j for j in range(D_CONV) if j != D_CONV // 2]
    win = CONV_SUB + 2 * HALO
    r = lax.broadcasted_iota(I32, (len(taps) * CONV_SUB, win), 0)
    c = lax.broadcasted_iota(I32, (len(taps) * CONV_SUB, win), 1)
    n = r >> (CONV_SUB.bit_length() - 1)
    shift = jnp.where(n < D_CONV // 2, n, n + 1) - D_CONV // 2
    pick = jnp.where(c == HALO + (r - n * CONV_SUB) + shift, 1.0, 0.0).astype(BF16)
    for sb in range(tm // CONV_SUB):
        xe = ext_ref[pl.ds(sb * CONV_SUB, win), :]
        moved = jnp.dot(pick, xe, preferred_element_type=F32)
        acc = bias + xe[HALO:HALO + CONV_SUB].astype(F32) * w[D_CONV // 2:D_CONV // 2 + 1, :]
        for k, j in enumerate(taps):
            acc = acc + moved[k * CONV_SUB:(k + 1) * CONV_SUB] * w[j:j + 1, :]
        o_ref[pl.ds(sb * CONV_SUB, CONV_SUB), :] = _silu(acc).astype(o_ref.dtype)


def conv_silu(proj3, conv_w, conv_b, tm, tc):
    b, s, _ = proj3.shape
    c0 = C_XBC // tc
    hb = tm // HALO
    nh = s // HALO
    return pl.pallas_call(
        functools.partial(_conv_kernel, tm=tm),
        grid=(b, s // tm, XBC_DIM // tc),
        in_specs=[
            pl.BlockSpec((None, HALO, tc), lambda bi, i, j: (bi, jnp.maximum(i * hb - 1, 0), c0 + j)),
            pl.BlockSpec((None, tm, tc), lambda bi, i, j: (bi, i, c0 + j)),
            pl.BlockSpec((None, HALO, tc), lambda bi, i, j: (bi, jnp.minimum((i + 1) * hb, nh - 1), c0 + j)),
            pl.BlockSpec((D_CONV, tc), lambda bi, i, j: (0, j)),
            pl.BlockSpec((1, tc), lambda bi, i, j: (0, j)),
        ],
        out_specs=pl.BlockSpec((None, tm, tc), lambda bi, i, j: (bi, i, j)),
        out_shape=jax.ShapeDtypeStruct((b, s, XBC_DIM), BF16),
        scratch_shapes=[pltpu.VMEM((tm + 2 * HALO, tc), BF16)],
        compiler_params=_cparams(("parallel", "parallel", "parallel"), VMEM_LIMIT),
        name="conv_silu",
    )(proj3, proj3, proj3, conv_w, conv_b.reshape(1, XBC_DIM))


def _expand_heads(a):
    hi = a.astype(BF16)
    lo = (a - hi.astype(F32)).astype(BF16)
    head_of = lax.broadcasted_iota(I32, (SSM_HEADS, D_INNER), 1) // SSM_HEAD_DIM
    e = jnp.where(head_of == lax.broadcasted_iota(I32, (SSM_HEADS, D_INNER), 0), 1.0, 0.0).astype(BF16)
    return jnp.dot(hi, e, preferred_element_type=F32) + jnp.dot(lo, e, preferred_element_type=F32)


def _tri(n, fn):
    r = lax.broadcasted_iota(I32, (n, n), 0)
    c = lax.broadcasted_iota(I32, (n, n), 1)
    return jnp.where(fn(r, c), 1.0, 0.0).astype(F32)


def _dot_hi(a, b):
    return jnp.dot(a, b, preferred_element_type=F32, precision=lax.Precision.HIGHEST)


def _ssd_state_kernel(xf_ref, bf_ref, dtf_ref, xb_ref, bb_ref, dtb_ref, par_ref, hf_ref, hb_ref, sf_s, sb_s):
    j = pl.program_id(1)

    @pl.when(j == 0)
    def _():
        sf_s[...] = jnp.zeros(sf_s.shape, F32)
        sb_s[...] = jnp.zeros(sb_s.shape, F32)

    hf_ref[...] = sf_s[...].astype(hf_ref.dtype)
    hb_ref[...] = sb_s[...].astype(hb_ref.dtype)

    par = par_ref[...]
    lower_incl = _tri(CHUNK, lambda r, c: c <= r)
    lower_strict = _tri(CHUNK, lambda r, c: c < r)

    def one_direction(x_ref, b_ref, dt_ref, s_ref, col0, backward):
        raw = dt_ref[...][:, col0:col0 + SSM_HEADS]
        dt = _softplus(raw + par[0:1, col0:col0 + SSM_HEADS])
        a = dt * par[1:2, col0:col0 + SSM_HEADS]
        if backward:
            excl = _dot_hi(lower_strict, a)
            wgt = dt * jnp.exp(excl)
            total = excl[CHUNK - 1:CHUNK, :] + a[CHUNK - 1:CHUNK, :]
        else:
            cum = _dot_hi(lower_incl, a)
            total = cum[CHUNK - 1:CHUNK, :]
            wgt = dt * jnp.exp(total - cum)
        xw = (x_ref[...].astype(F32) * _expand_heads(wgt)).astype(BF16)
        dec = _expand_heads(jnp.exp(jnp.broadcast_to(total, (8, SSM_HEADS))))[0:1, :]
        bmat = b_ref[...]
        gw = D_INNER // SSM_GROUPS
        for g in range(SSM_GROUPS):
            contrib = lax.dot_general(bmat[:, g * D_STATE:(g + 1) * D_STATE], xw[:, g * gw:(g + 1) * gw],
                                      (((0,), (0,)), ((), ())), preferred_element_type=F32)
            s_ref[g] = s_ref[g] * dec[:, g * gw:(g + 1) * gw] + contrib

    one_direction(xf_ref, bf_ref, dtf_ref, sf_s, 0, False)
    one_direction(xb_ref, bb_ref, dtb_ref, sb_s, SSM_HEADS, True)


def ssd_states(xbc, dt3, par):
    b, s, _ = xbc.shape
    nc = s // CHUNK
    gw = D_INNER // SSM_GROUPS
    xblk = D_INNER // D_INNER
    bcol = D_INNER // (SSM_GROUPS * D_STATE)
    hshape = jax.ShapeDtypeStruct((b, nc, SSM_GROUPS, D_STATE, gw), BF16)
    hspec_f = pl.BlockSpec((None, None, SSM_GROUPS, D_STATE, gw), lambda bi, j: (bi, j, 0, 0, 0))
    hspec_b = pl.BlockSpec((None, None, SSM_GROUPS, D_STATE, gw), lambda bi, j: (bi, nc - 1 - j, 0, 0, 0))
    del xblk
    return pl.pallas_call(
        _ssd_state_kernel,
        grid=(b, nc),
        in_specs=[
            pl.BlockSpec((None, CHUNK, D_INNER), lambda bi, j: (bi, j, 0)),
            pl.BlockSpec((None, CHUNK, SSM_GROUPS * D_STATE), lambda bi, j: (bi, j, bcol)),
            pl.BlockSpec((None, CHUNK, LANE), lambda bi, j: (bi, j, 0)),
            pl.BlockSpec((None, CHUNK, D_INNER), lambda bi, j: (bi, nc - 1 - j, 0)),
            pl.BlockSpec((None, CHUNK, SSM_GROUPS * D_STATE), lambda bi, j: (bi, nc - 1 - j, bcol)),
            pl.BlockSpec((None, CHUNK, LANE), lambda bi, j: (bi, nc - 1 - j, 0)),
            pl.BlockSpec((8, LANE), lambda bi, j: (0, 0)),
        ],
        out_specs=[hspec_f, hspec_b],
        out_shape=[hshape, hshape],
        scratch_shapes=[pltpu.VMEM((SSM_GROUPS, D_STATE, gw), F32), pltpu.VMEM((SSM_GROUPS, D_STATE, gw), F32)],
        compiler_params=_cparams(("parallel", "arbitrary"), VMEM_LIMIT),
        name="ssd_states",
    )(xbc, xbc, dt3, xbc, xbc, dt3, par)


def _ssd_out_kernel(x_ref, b_ref, c_ref, z_ref, dt_ref, dtt_ref, par_ref, part_ref, dsk_ref, nw_ref,
                    hf_ref, hb_ref, o_ref):
    par = par_ref[...]
    part = part_ref[...]
    lower_incl = _tri(CHUNK, lambda r, c: c <= r)
    lower_strict = _tri(CHUNK, lambda r, c: c < r)
    upper_incl = _tri(CHUNK, lambda r, c: r <= c)
    upper_strict = _tri(CHUNK, lambda r, c: r < c)

    dtc = _softplus(dt_ref[...][:, 0:2 * SSM_HEADS] + par[0:1, 0:2 * SSM_HEADS])
    ac = dtc * par[1:2, 0:2 * SSM_HEADS]
    cum_c = _dot_hi(lower_incl, ac)
    excl_c = _dot_hi(lower_strict, ac)
    dtr = _softplus(dtt_ref[...][0:2 * SSM_HEADS, :] + part[0:2 * SSM_HEADS, 0:1])
    ar = dtr * part[0:2 * SSM_HEADS, 1:2]
    cum_r = _dot_hi(ar, upper_incl)
    excl_r = _dot_hi(ar, upper_strict)

    cumf_c = cum_c[:, 0:SSM_HEADS]
    exclb_c = excl_c[:, SSM_HEADS:2 * SSM_HEADS]
    totb = cum_c[CHUNK - 1:CHUNK, SSM_HEADS:2 * SSM_HEADS]
    dtf_r = dtr[0:SSM_HEADS, :]
    dtb_r = dtr[SSM_HEADS:2 * SSM_HEADS, :]
    fwd_r = cum_r[0:SSM_HEADS, :] - jnp.log(dtf_r)
    bwd_r = excl_r[SSM_HEADS:2 * SSM_HEADS, :] + jnp.log(dtb_r)
    diag_r = jnp.log(dtf_r + dtb_r)

    row = lax.broadcasted_iota(I32, (CHUNK, CHUNK), 0)
    col = lax.broadcasted_iota(I32, (CHUNK, CHUNK), 1)
    low = col < row
    diag = col == row
    lane = lax.broadcasted_iota(I32, (CHUNK, LANE), 1)
    first = lane < SSM_HEAD_DIM

    gw = D_INNER // SSM_GROUPS
    hpg = SSM_HEADS // SSM_GROUPS
    scale_f = _expand_heads(jnp.exp(cumf_c))
    scale_b = _expand_heads(jnp.exp(totb - exclb_c))
    for g in range(SSM_GROUPS):
        cg = c_ref[:, g * D_STATE:(g + 1) * D_STATE]
        bg = b_ref[:, g * D_STATE:(g + 1) * D_STATE]
        cb = lax.dot_general(cg, bg, (((1,), (1,)), ((), ())), preferred_element_type=F32)
        inter_f = jnp.dot(cg, hf_ref[g], preferred_element_type=F32)
        inter_b = jnp.dot(cg, hb_ref[g], preferred_element_type=F32)
        parts = []
        for pp in range(hpg // 2):
            gms = []
            for h in (g * hpg + 2 * pp, g * hpg + 2 * pp + 1):
                arg = jnp.where(low, cumf_c[:, h:h + 1] - fwd_r[h:h + 1, :],
                                bwd_r[h:h + 1, :] - exclb_c[:, h:h + 1])
                arg = jnp.where(diag, diag_r[h:h + 1, :], arg)
                gms.append((cb * jnp.exp(arg)).astype(BF16))
            ls = slice(g * gw + pp * LANE, g * gw + (pp + 1) * LANE)
            il = slice(pp * LANE, (pp + 1) * LANE)
            xp = x_ref[:, ls]
            rhs = jnp.concatenate([jnp.where(first, xp, 0), jnp.where(first, 0, xp)], axis=0)
            yp = jnp.dot(jnp.concatenate(gms, axis=1), rhs, preferred_element_type=F32)
            yp = (yp + inter_f[:, il] * scale_f[:, ls] + inter_b[:, il] * scale_b[:, ls]
                  + xp.astype(F32) * dsk_ref[:, ls])
            parts.append(yp * _silu(z_ref[:, ls].astype(F32)))
        yg = jnp.concatenate(parts, axis=1)
        sl = slice(g * gw, (g + 1) * gw)
        o_ref[:, sl] = (yg * lax.rsqrt(jnp.mean(yg * yg, axis=-1, keepdims=True) + EPS)
                        * nw_ref[:, sl]).astype(o_ref.dtype)


def ssd_output(xbc, proj3, dt3, dtt3, par, part, dskip_x, nw, hf, hb):
    b, s, _ = xbc.shape
    nc = s // CHUNK
    gw = D_INNER // SSM_GROUPS
    gs = SSM_GROUPS * D_STATE
    hspec = pl.BlockSpec((None, None, SSM_GROUPS, D_STATE, gw), lambda bi, j: (bi, j, 0, 0, 0))
    return pl.pallas_call(
        _ssd_out_kernel,
        grid=(b, nc),
        in_specs=[
            pl.BlockSpec((None, CHUNK, D_INNER), lambda bi, j: (bi, j, 0)),
            pl.BlockSpec((None, CHUNK, gs), lambda bi, j: (bi, j, D_INNER // gs)),
            pl.BlockSpec((None, CHUNK, gs), lambda bi, j: (bi, j, D_INNER // gs + 1)),
            pl.BlockSpec((None, CHUNK, D_INNER), lambda bi, j: (bi, j, C_Z // D_INNER)),
            pl.BlockSpec((None, CHUNK, LANE), lambda bi, j: (bi, j, 0)),
            pl.BlockSpec((LANE, CHUNK), lambda bi, j: (0, bi * nc + j)),
            pl.BlockSpec((8, LANE), lambda bi, j: (0, 0)),
            pl.BlockSpec((LANE, 8), lambda bi, j: (0, 0)),
            pl.BlockSpec((1, D_INNER), lambda bi, j: (0, 0)),
            pl.BlockSpec((1, D_INNER), lambda bi, j: (0, 0)),
            hspec, hspec,
        ],
        out_specs=pl.BlockSpec((None, CHUNK, D_INNER), lambda bi, j: (bi, j, 0)),
        out_shape=jax.ShapeDtypeStruct((b, s, D_INNER), BF16),
        compiler_params=_cparams(("parallel", "parallel"), VMEM_LIMIT),
        name="ssd_output",
    )(xbc, xbc, xbc, proj3, dt3, dtt3, par, part, dskip_x, nw, hf, hb)


def _merge_kernel(x_ref, a_ref, s_ref, g0_ref, g1_ref, wa_ref, ws_ref, wo_ref, o_ref):
    ya = jnp.dot(a_ref[...], wa_ref[...], preferred_element_type=F32)
    ys = jnp.dot(s_ref[...], ws_ref[...], preferred_element_type=F32)
    g0 = 1.0 / (1.0 + jnp.exp(-g0_ref[...].astype(F32)))
    g1 = 1.0 / (1.0 + jnp.exp(-g1_ref[...].astype(F32)))
    merged = (g0 * ya + g1 * ys).astype(BF16)
    o_ref[...] = x_ref[...] + jnp.dot(merged, wo_ref[...], preferred_element_type=F32)


def merge_out(x, attn, ssm, proj, wa, ws, wo, tm):
    t, d = x.shape
    gb = C_GATE // d
    const = lambda i: (0, 0)
    return pl.pallas_call(
        _merge_kernel,
        grid=(t // tm,),
        in_specs=[
            pl.BlockSpec((tm, d), lambda i: (i, 0)),
            pl.BlockSpec((tm, d), lambda i: (i, 0)),
            pl.BlockSpec((tm, D_INNER), lambda i: (i, 0)),
            pl.BlockSpec((tm, d), lambda i: (i, gb)),
            pl.BlockSpec((tm, d), lambda i: (i, gb + 1)),
            pl.BlockSpec((d, d), const),
            pl.BlockSpec((D_INNER, d), const),
            pl.BlockSpec((d, d), const),
        ],
        out_specs=pl.BlockSpec((tm, d), lambda i: (i, 0)),
        out_shape=jax.ShapeDtypeStruct((t, d), F32),
        compiler_params=_cparams(("parallel",), VMEM_LIMIT),
        name="merge_out",
    )(x, attn, ssm, proj, proj, wa, ws, wo)


def _cross_kernel(x_ref, nw_ref, wq_ref, kv_ref, wo_ref, o_ref):
    x = x_ref[...]
    h = _rms(x, nw_ref[...], EPS).astype(BF16)
    q = jnp.dot(h, wq_ref[...], preferred_element_type=F32).astype(BF16)
    kv = kv_ref[...]
    outs = []
    for hd in range(X_HEADS):
        qh = q[:, hd * X_HEAD_DIM:(hd + 1) * X_HEAD_DIM]
        kh = kv[:, hd * X_HEAD_DIM:(hd + 1) * X_HEAD_DIM]
        vh = kv[:, X_WIDTH + hd * X_HEAD_DIM:X_WIDTH + (hd + 1) * X_HEAD_DIM]
        s = lax.dot_general(qh, kh, (((1,), (1,)), ((), ())), preferred_element_type=F32) * (X_HEAD_DIM ** -0.5)
        s = s - jnp.max(s, axis=-1, keepdims=True)
        p = jnp.exp(s)
        p = (p / jnp.sum(p, axis=-1, keepdims=True)).astype(BF16)
        outs.append(jnp.dot(p, vh, preferred_element_type=F32).astype(BF16))
    o = jnp.concatenate(outs, axis=1)
    o_ref[...] = x + jnp.dot(o, wo_ref[...], preferred_element_type=F32)


def cross_attention(x, nw, wq, kv, wo, seq, tm):
    t, d = x.shape
    per_seq = seq // tm
    const = lambda i: (0, 0)
    return pl.pallas_call(
        _cross_kernel,
        grid=(t // tm,),
        in_specs=[
            pl.BlockSpec((tm, d), lambda i: (i, 0)),
            pl.BlockSpec((1, d), const),
            pl.BlockSpec((d, X_WIDTH), const),
            pl.BlockSpec((None, MEM_TOKENS, 2 * X_WIDTH), lambda i: (i // per_seq, 0, 0)),
            pl.BlockSpec((X_WIDTH, d), const),
        ],
        out_specs=pl.BlockSpec((tm, d), lambda i: (i, 0)),
        out_shape=jax.ShapeDtypeStruct((t, d), F32),
        compiler_params=_cparams(("parallel",), VMEM_LIMIT),
        name="cross_attention",
    )(x, nw.reshape(1, d), wq, kv, wo)


def _router_kernel(x_ref, nw_ref, wrt_ref, br_ref, hb_ref, aff_ref):
    h = _rms(x_ref[...], nw_ref[...], EPS)
    hb_ref[...] = h.astype(hb_ref.dtype)
    logits = lax.dot_general(wrt_ref[...], h, (((1,), (1,)), ((), ())), preferred_element_type=F32,
                             precision=lax.Precision.HIGHEST) + br_ref[...]
    e = jnp.exp(logits - jnp.max(logits, axis=0, keepdims=True))
    aff_ref[...] = e / jnp.sum(e, axis=0, keepdims=True)


def router(x, nw, wrt, br, tm):
    t, d = x.shape
    return pl.pallas_call(
        _router_kernel,
        grid=(t // tm,),
        in_specs=[
            pl.BlockSpec((tm, d), lambda i: (i, 0)),
            pl.BlockSpec((1, d), lambda i: (0, 0)),
            pl.BlockSpec((N_EXPERTS, d), lambda i: (0, 0)),
            pl.BlockSpec((N_EXPERTS, 1), lambda i: (0, 0)),
        ],
        out_specs=[pl.BlockSpec((tm, d), lambda i: (i, 0)), pl.BlockSpec((N_EXPERTS, tm), lambda i: (0, i))],
        out_shape=[jax.ShapeDtypeStruct((t, d), BF16), jax.ShapeDtypeStruct((N_EXPERTS, t), F32)],
        compiler_params=_cparams(("parallel",), VMEM_LIMIT),
        name="router",
    )(x, nw.reshape(1, d), wrt, br.reshape(N_EXPERTS, 1))


TB = 256


def _topk_kernel(aff_ref, pos_ref, gate_ref, roff_ref, *, cap):
    a = aff_ref[...]
    nr = a.shape[1]
    bits = pltpu.bitcast(a, I32)

    def count(mask):
        c = jnp.sum(jnp.where(mask, 1.0, 0.0), axis=2, keepdims=True)
        return jnp.sum(c, axis=1, keepdims=True)

    def body(i, thr):
        cand = thr | jnp.left_shift(jnp.int32(1), 30 - i)
        return jnp.where(count(bits >= cand) >= float(cap), cand, thr)

    thr = lax.fori_loop(0, 31, body, jnp.zeros((N_EXPERTS, 1, 1), I32))
    gt = bits > thr
    eq = bits == thr
    need = float(cap) - count(gt)

    ustrict = _tri(TB, lambda r, c: r < c).astype(BF16)
    ones = jnp.ones((TB, TB), BF16)
    lstrict = _tri(nr, lambda r, c: c < r).astype(BF16)

    def excl_prefix(m):
        mb = m.astype(BF16)
        within = jnp.dot(mb, ustrict, preferred_element_type=F32)
        rowsum = jnp.dot(mb, ones, preferred_element_type=F32)
        rowoff = jnp.dot(lstrict, rowsum.astype(BF16), preferred_element_type=F32)
        return within + rowoff, rowoff

    for e in range(N_EXPERTS):
        eq_e = jnp.where(eq[e], 1.0, 0.0)
        pe, _ = excl_prefix(eq_e)
        keep = jnp.where(pe < need[e], eq_e, 0.0)
        sel = jnp.where(gt[e], 1.0, keep)
        ps, roff = excl_prefix(sel)
        chosen = sel > 0.5
        pos_ref[e] = jnp.where(chosen, ps.astype(I32), -1)
        gate_ref[e] = jnp.where(chosen, a[e], 0.0)
        roff_ref[e] = roff.astype(I32)


def topk_select(aff3, cap):
    e, r, tb = aff3.shape
    full = lambda: (0, 0, 0)
    return pl.pallas_call(
        functools.partial(_topk_kernel, cap=cap),
        grid=(),
        in_specs=[pl.BlockSpec((e, r, tb), full)],
        out_specs=[pl.BlockSpec((e, r, tb), full)] * 3,
        out_shape=[jax.ShapeDtypeStruct((e, r, tb), I32), jax.ShapeDtypeStruct((e, r, tb), F32),
                   jax.ShapeDtypeStruct((e, r, tb), I32)],
        compiler_params=pltpu.CompilerParams(vmem_limit_bytes=VMEM_LIMIT),
        name="topk_select",
    )(aff3)


GATHER_BUFS = 8


GATHER_PARTS = 2


def _ffn_kernel(lo_ref, hi_ref, roff_ref, rend_ref, pos_ref, gate_ref, hb_hbm, wg_ref, wu_ref, wd_ref, y_ref,
                xbuf, sem, xc_s, g_s, *, ts, nj):
    e = pl.program_id(0)
    j = pl.program_id(1)
    nr = pos_ref.shape[0]
    step = e * nj + j
    nsteps = pl.num_programs(0) * nj
    lo = lo_ref[step]
    hi = hi_ref[step]
    xc_s[...] = jnp.zeros(xc_s.shape, F32)
    g_s[...] = jnp.zeros(g_s.shape, F32)

    def fetch(r, slot):
        return pltpu.make_async_copy(hb_hbm.at[pl.ds(r * TB, TB)], xbuf.at[slot], sem.at[slot])

    def prime(first, end):
        for i in range(GATHER_BUFS - 1):
            @pl.when(first + i < end)
            def _(i=i):
                fetch(first + i, i).start()

    @pl.when(step == 0)
    def _():
        prime(lo, hi)

    part = ts // GATHER_PARTS
    slot_ids = lax.broadcasted_iota(I32, (part, TB), 0)

    def body(r, carry):
        idx = r - lo
        slot = idx & (GATHER_BUFS - 1)
        fetch(r, slot).wait()

        @pl.when(r + (GATHER_BUFS - 1) < hi)
        def _():
            fetch(r + (GATHER_BUFS - 1), (idx + (GATHER_BUFS - 1)) & (GATHER_BUFS - 1)).start()

        prow = pos_ref[pl.ds(r, 1), :]
        grow = gate_ref[pl.ds(r, 1), :]
        first = roff_ref[e * nr + r]
        last = rend_ref[e * nr + r]
        for p in range(GATHER_PARTS):
            base = j * ts + p * part

            @pl.when(jnp.logical_and(first < base + part, last > base))
            def _(p=p, base=base):
                hit = prow == slot_ids + base
                sel = jnp.where(hit, 1.0, 0.0).astype(BF16)
                rows = slice(p * part, (p + 1) * part)
                xc_s[rows, :] += jnp.dot(sel, xbuf[slot], preferred_element_type=F32)
                g_s[rows, :] += jnp.sum(jnp.where(hit, grow, 0.0), axis=1, keepdims=True)
        return carry

    lax.fori_loop(lo, hi, body, 0)

    @pl.when(step + 1 < nsteps)
    def _():
        prime(lo_ref[step + 1], hi_ref[step + 1])

    xc = xc_s[...].astype(BF16)
    hg = jnp.dot(xc, wg_ref[...], preferred_element_type=F32)
    hu = jnp.dot(xc, wu_ref[...], preferred_element_type=F32)
    hid = (_silu(hg) * hu).astype(BF16)
    y = jnp.dot(hid, wd_ref[...], preferred_element_type=F32) * g_s[...]
    y_ref[...] = y.astype(y_ref.dtype)


def expert_ffn(lo, hi, roff, rend, pos, gate, hb, wg, wu, wd, cap, ts):
    e, r, tb = pos.shape
    d = hb.shape[1]
    nj = cap // ts
    grid_spec = pltpu.PrefetchScalarGridSpec(
        num_scalar_prefetch=4,
        grid=(e, nj),
        in_specs=[
            pl.BlockSpec((None, r, tb), lambda ei, j, *_: (ei, 0, 0)),
            pl.BlockSpec((None, r, tb), lambda ei, j, *_: (ei, 0, 0)),
            pl.BlockSpec(memory_space=pl.ANY),
            pl.BlockSpec((None, d, D_FF), lambda ei, j, *_: (ei, 0, 0)),
            pl.BlockSpec((None, d, D_FF), lambda ei, j, *_: (ei, 0, 0)),
            pl.BlockSpec((None, D_FF, d), lambda ei, j, *_: (ei, 0, 0)),
        ],
        out_specs=pl.BlockSpec((None, ts, d), lambda ei, j, *_: (ei, j, 0)),
        scratch_shapes=[
            pltpu.VMEM((GATHER_BUFS, TB, d), BF16),
            pltpu.SemaphoreType.DMA((GATHER_BUFS,)),
            pltpu.VMEM((ts, d), F32),
            pltpu.VMEM((ts, 1), F32),
        ],
    )
    return pl.pallas_call(
        functools.partial(_ffn_kernel, ts=ts, nj=nj),
        grid_spec=grid_spec,
        out_shape=jax.ShapeDtypeStruct((e, cap, d), BF16),
        compiler_params=_cparams(("arbitrary", "arbitrary"), VMEM_LIMIT),
        name="expert_ffn",
    )(lo, hi, roff.reshape(-1), rend.reshape(-1), pos, gate, hb, wg, wu, wd)


WIN_MAIN = 128
WIN_OVER = TB + BF16_SUBLANE - WIN_MAIN


def _combine_kernel(start_ref, ostart_ref, ovf_ref, x_ref, post_ref, nw_ref, *rest, final):
    y_main = rest[:N_EXPERTS]
    y_over = rest[N_EXPERTS:2 * N_EXPERTS]
    o_ref = rest[2 * N_EXPERTS]
    r = pl.program_id(0)
    nr = pl.num_programs(0)
    post = post_ref[...]
    rels = [post[:, e:e + 1] - start_ref[e * nr + r] for e in range(N_EXPERTS)]
    lane = lax.broadcasted_iota(I32, (TB, WIN_MAIN), 1)
    acc = x_ref[...]
    for e in range(0, N_EXPERTS, 2):
        sel = jnp.concatenate([jnp.where(rels[e] == lane, 1.0, 0.0),
                               jnp.where(rels[e + 1] == lane, 1.0, 0.0)], axis=1).astype(BF16)
        rows = jnp.concatenate([y_main[e][...], y_main[e + 1][...]], axis=0)
        acc = acc + jnp.dot(sel, rows, preferred_element_type=F32)
    o_ref[...] = acc

    @pl.when(ovf_ref[r] != 0)
    def _():
        lane_o = lax.broadcasted_iota(I32, (TB, WIN_OVER), 1) + WIN_MAIN
        more = o_ref[...]
        for e in range(N_EXPERTS):
            sel = jnp.where(rels[e] == lane_o, 1.0, 0.0).astype(BF16)
            more = more + jnp.dot(sel, y_over[e][...], preferred_element_type=F32)
        o_ref[...] = more

    if final:
        o_ref[...] = _rms(o_ref[...], nw_ref[...], EPS)


def combine(x, post, start, ostart, ovf, y, norm_w, final):
    t, d = x.shape
    e, cap, _ = y.shape
    nr = t // TB

    def y_spec(ei, rows, which):
        return pl.BlockSpec((pl.Element(rows), pl.Element(d)),
                            lambda r, st, ost, ov: (pl.multiple_of(ei * cap + (st, ost)[which][ei * nr + r],
                                                                   BF16_SUBLANE), 0))

    grid_spec = pltpu.PrefetchScalarGridSpec(
        num_scalar_prefetch=3,
        grid=(nr,),
        in_specs=([pl.BlockSpec((TB, d), lambda r, st, ost, ov: (r, 0)),
                   pl.BlockSpec((TB, e), lambda r, st, ost, ov: (r, 0)),
                   pl.BlockSpec((1, d), lambda r, st, ost, ov: (0, 0))]
                  + [y_spec(ei, WIN_MAIN, 0) for ei in range(e)]
                  + [y_spec(ei, WIN_OVER, 1) for ei in range(e)]),
        out_specs=pl.BlockSpec((TB, d), lambda r, st, ost, ov: (r, 0)),
    )
    y2 = y.reshape(e * cap, d)
    return pl.pallas_call(
        functools.partial(_combine_kernel, final=final),
        grid_spec=grid_spec,
        out_shape=jax.ShapeDtypeStruct((t, d), F32),
        compiler_params=_cparams(("arbitrary",), VMEM_LIMIT),
        name="combine",
    )(start, ostart, ovf, x, post, norm_w.reshape(1, d), *([y2] * (2 * e)))


def _tiles(t, s):
    def fit(n, pref):
        while n % pref:
            pref //= 2
        return pref

    cap = max(1, CAPACITY_FACTOR * t // N_EXPERTS)
    return dict(
        tm_proj=fit(t, 1024), tn_proj=PROJ_COLS // 4,
        tq=fit(s, 1024), tk=fit(s, 512),
        tm_conv=fit(s, 512), tc_conv=1024,
        tm_tok=fit(s, 512),
        cap=cap, ts=fit(cap, 256),
    )


def _prep_layer(l, p):
    w_in = p['w_in'][l]
    o_q, o_k, o_v, o_z, o_xbc, o_dt = 1024, 2048, 3072, 5120, 8192, 8256
    w_main = jnp.concatenate([w_in[:, o_v:o_z], w_in[:, o_dt:], w_in[:, o_z:o_xbc], w_in[:, :o_k]],
                             axis=1).astype(BF16)
    w_vt = w_in[:, o_k:o_v].T.astype(BF16)
    w_dt = jnp.pad(w_in[:, o_xbc:o_dt], ((0, 0), (0, LANE - 2 * SSM_HEADS))).astype(BF16)
    dt_bias = p['dt_bias'][l].reshape(-1).astype(F32)
    a_neg = -jnp.exp(p['a_log'][l].astype(F32)).reshape(-1)
    par = jnp.zeros((8, LANE), F32).at[0, :2 * SSM_HEADS].set(dt_bias).at[1, :2 * SSM_HEADS].set(a_neg)
    part = jnp.zeros((LANE, 8), F32).at[:2 * SSM_HEADS, 0].set(dt_bias).at[:2 * SSM_HEADS, 1].set(a_neg)
    lam_init = 0.8 - 0.6 * math.exp(-0.3 * l)
    lam = (jnp.exp(jnp.sum(p['lambda_q1'][l].astype(F32) * p['lambda_k1'][l].astype(F32)))
           - jnp.exp(jnp.sum(p['lambda_q2'][l].astype(F32) * p['lambda_k2'][l].astype(F32))) + lam_init)
    slopes = jnp.exp2(-8.0 * (jnp.arange(ATTN_HEADS, dtype=F32) + 1.0) / ATTN_HEADS)
    tail = jnp.log(2.0 / (1.0 - jnp.exp(-slopes)))
    scal = jnp.concatenate([slopes, lam.reshape(1), jnp.full((1,), 1.0 - lam_init, F32), tail]).astype(F32)
    return dict(
        norm_mix=p['norm_mix_w'][l], w_main=w_main, w_vt=w_vt, w_dt=w_dt, w_dtt=w_dt.T, par=par, part=part,
        scal=scal,
        subln=p['attn_subln_w'][l].astype(F32),
        conv_w=p['conv_w'][l].astype(F32), conv_b=p['conv_b'][l].astype(F32),
        dskip_x=jnp.repeat(p['d_skip'][l].astype(F32), SSM_HEAD_DIM).reshape(1, D_INNER),
        ssm_nw=p['ssm_norm_w'][l].astype(F32).reshape(1, D_INNER),
        wa=p['w_attn_branch'][l].astype(BF16), ws=p['w_ssm_branch'][l].astype(BF16),
        wo=p['w_out'][l].astype(BF16),
        norm_cross=p['norm_cross_w'][l], norm_mem=p['norm_mem_w'][l],
        wxq=p['w_xq'][l].astype(BF16), wxkv=p['w_xkv'][l].astype(BF16), wxo=p['w_xo'][l].astype(BF16),
        norm_ffn=p['norm_ffn_w'][l], wrt=p['w_router'][l].astype(F32).T, br=p['b_router'][l].astype(F32),
        wg=p['w_gate'][l].astype(BF16), wu=p['w_up'][l].astype(BF16), wd=p['w_down'][l].astype(BF16),
    )


def _layer(x, mem, lw, b, s, norm_final_w, last):
    t = b * s
    tl = _tiles(t, s)
    proj = norm_matmul(x, lw['norm_mix'], lw['w_main'], BF16, tl['tm_proj'], tl['tn_proj'])
    vt, dt, dtt = proj_transposed(x, lw['norm_mix'], lw['w_vt'], lw['w_dt'], lw['w_dtt'], tl['tm_tok'])
    proj3 = proj.reshape(b, s, PROJ_COLS)
    dt3 = dt.reshape(b, s, LANE)
    attn = diff_attention(proj3, vt, lw['scal'], lw['subln'], tl['tq'], tl['tk'])
    xbc = conv_silu(proj3, lw['conv_w'], lw['conv_b'], tl['tm_conv'], tl['tc_conv'])
    hf, hb = ssd_states(xbc, dt3, lw['par'])
    ssm = ssd_output(xbc, proj3, dt3, dtt, lw['par'], lw['part'], lw['dskip_x'], lw['ssm_nw'], hf, hb)
    x = merge_out(x, attn.reshape(t, D_MODEL), ssm.reshape(t, D_INNER), proj, lw['wa'], lw['ws'], lw['wo'],
                  tl['tm_tok'])
    nm = mem.shape[0] * mem.shape[1]
    kv = norm_matmul(mem.reshape(nm, D_MODEL), lw['norm_mem'], lw['wxkv'], BF16, min(nm, 512), 2 * X_WIDTH)
    x = cross_attention(x, lw['norm_cross'], lw['wxq'], kv.reshape(b, MEM_TOKENS, 2 * X_WIDTH), lw['wxo'], s,
                        tl['tm_tok'])
    cap, ts = tl['cap'], tl['ts']
    nr = t // TB
    hbf, aff = router(x, lw['norm_ffn'], lw['wrt'], lw['br'], tl['tm_tok'])
    pos, gate, roff = topk_select(aff.reshape(N_EXPERTS, nr, TB), cap)
    roff = roff[:, :, 0]
    rend = jnp.concatenate([roff[:, 1:], jnp.full((N_EXPERTS, 1), cap, I32)], axis=1)
    edges = jnp.arange(cap // ts, dtype=I32) * ts
    lo = jnp.sum(rend[:, None, :] <= edges[None, :, None], axis=2).astype(I32).reshape(-1)
    hi = jnp.sum(roff[:, None, :] < (edges + ts)[None, :, None], axis=2).astype(I32).reshape(-1)
    y = expert_ffn(lo, hi, roff, rend, pos, gate, hbf, lw['wg'], lw['wu'], lw['wd'], cap, ts)
    start = jnp.minimum((roff // BF16_SUBLANE) * BF16_SUBLANE, cap - (WIN_MAIN + WIN_OVER)).astype(I32)
    ovf = jnp.any(rend - start > WIN_MAIN, axis=0).astype(I32)
    ostart = jnp.where(ovf[None, :] != 0, start + WIN_MAIN, 0).astype(I32)
    post = jnp.transpose(pos.reshape(N_EXPERTS, t))
    return combine(x, post, start.reshape(-1), ostart.reshape(-1), ovf, y, norm_final_w, last)


def _trunk(x, mem, layers, norm_final_w):
    b, s, d = x.shape
    xt = x.reshape(b * s, d)
    for l, lw in enumerate(layers):
        xt = _layer(xt, mem, lw, b, s, norm_final_w, l == len(layers) - 1)
    return xt.reshape(b, s, d)


def kernel(x_prompt, x_sample, mem_prompt, mem_sample, norm_mix_w, w_in, lambda_q1, lambda_k1, lambda_q2,
           lambda_k2, attn_subln_w, conv_w, conv_b, a_log, dt_bias, d_skip, ssm_norm_w, w_attn_branch,
           w_ssm_branch, w_out, norm_cross_w, norm_mem_w, w_xq, w_xkv, w_xo, norm_ffn_w, w_router, b_router,
           w_gate, w_up, w_down, norm_final_w):
    p = dict(norm_mix_w=norm_mix_w, w_in=w_in, lambda_q1=lambda_q1, lambda_k1=lambda_k1, lambda_q2=lambda_q2,
             lambda_k2=lambda_k2, attn_subln_w=attn_subln_w, conv_w=conv_w, conv_b=conv_b, a_log=a_log,
             dt_bias=dt_bias, d_skip=d_skip, ssm_norm_w=ssm_norm_w, w_attn_branch=w_attn_branch,
             w_ssm_branch=w_ssm_branch, w_out=w_out, norm_cross_w=norm_cross_w, norm_mem_w=norm_mem_w,
             w_xq=w_xq, w_xkv=w_xkv, w_xo=w_xo, norm_ffn_w=norm_ffn_w, w_router=w_router, b_router=b_router,
             w_gate=w_gate, w_up=w_up, w_down=w_down)
    layers = [_prep_layer(l, p) for l in range(w_in.shape[0])]
    y_prompt = _trunk(x_prompt, mem_prompt, layers, norm_final_w)
    y_sample = _trunk(x_sample, mem_sample, layers, norm_final_w)
    return (y_prompt, y_sample)
```

```python
import functools
import math

import jax
import jax.numpy as jnp
from jax import lax
from jax.experimental import pallas as pl
from jax.experimental.pallas import tpu as pltpu

F32 = jnp.float32
BF16 = jnp.bfloat16
I32 = jnp.int32

D_MODEL = 1024
ATTN_HEADS = 8
ATTN_HEAD_DIM = 64
ATTN_V_DIM = 128
SSM_HEADS = 32
SSM_HEAD_DIM = 64
SSM_GROUPS = 4
D_STATE = 128
D_INNER = 2048
D_CONV = 5
XBC_DIM = 3072
CHUNK = 128
MEM_TOKENS = 256
X_HEADS = 4
X_HEAD_DIM = 128
X_WIDTH = 512
N_EXPERTS = 16
CAPACITY_FACTOR = 2
D_FF = 2816
EPS = 1e-6
SUBLN_EPS = 1e-5

C_Z = 0
C_GATE = 2048
C_XBC = 4096
C_Q = 7168
C_K = 8192
PROJ_COLS = 9216

LANE = 128
BF16_SUBLANE = 16
VMEM_LIMIT = 56 * 1024 * 1024


def _cparams(sem, vmem=None):
    return pltpu.CompilerParams(dimension_semantics=sem, vmem_limit_bytes=vmem)


def _rms(x, w, eps):
    return x * lax.rsqrt(jnp.mean(x * x, axis=-1, keepdims=True) + eps) * w


def _softplus(x):
    return jnp.maximum(x, 0.0) + jnp.log(1.0 + jnp.exp(-jnp.abs(x)))


def _silu(x):
    return x * (1.0 / (1.0 + jnp.exp2(x * (-1.0 / math.log(2.0)))))


def _norm_matmul_kernel(x_ref, nw_ref, w_ref, o_ref, xn_ref):
    @pl.when(pl.program_id(1) == 0)
    def _():
        xn_ref[...] = _rms(x_ref[...], nw_ref[...], EPS).astype(xn_ref.dtype)

    o_ref[...] = jnp.dot(xn_ref[...], w_ref[...], preferred_element_type=F32).astype(o_ref.dtype)


def norm_matmul(x, nw, w, out_dtype, tm, tn):
    t, d = x.shape
    n = w.shape[1]
    return pl.pallas_call(
        _norm_matmul_kernel,
        grid=(t // tm, n // tn),
        in_specs=[
            pl.BlockSpec((tm, d), lambda i, j: (i, 0)),
            pl.BlockSpec((1, d), lambda i, j: (0, 0)),
            pl.BlockSpec((d, tn), lambda i, j: (0, j)),
        ],
        out_specs=pl.BlockSpec((tm, tn), lambda i, j: (i, j)),
        out_shape=jax.ShapeDtypeStruct((t, n), out_dtype),
        scratch_shapes=[pltpu.VMEM((tm, d), BF16)],
        compiler_params=_cparams(("parallel", "arbitrary"), VMEM_LIMIT),
        name="norm_matmul",
    )(x, nw.reshape(1, d), w)


ONES_ROWS = BF16_SUBLANE
SKIP_MARGIN = 17.0
POS_SPLIT = 32


FIXED_REF_MAX_RANGE = 60.0


def _attn_kernel(scal_ref, q_ref, k_ref, vt_ref, w_ref, qf_ref, kf_ref, o_ref, qa_s, m_s, acc_s, kn_s, ref_s,
                 *, tq, tk, seq):
    h = pl.program_id(1)
    qi = pl.program_id(2)
    slope = scal_ref[h]
    nk = seq // tk
    ndiag = tq // tk
    t0 = qi * tq
    dlo = qi * ndiag
    half = ATTN_HEAD_DIM
    nt = (((1,), (1,)), ((), ()))

    lane_k = lax.broadcasted_iota(I32, (tk, LANE), 1)

    @pl.when(qi == 0)
    def _():
        def body(i, carry):
            kc = k_ref[pl.ds(pl.multiple_of(i * tk, tk), tk), :].astype(F32)
            sq = kc * kc
            n0 = jnp.max(jnp.sum(jnp.where(lane_k < half, sq, 0.0), axis=1, keepdims=True), axis=0, keepdims=True)
            n1 = jnp.max(jnp.sum(jnp.where(lane_k >= half, sq, 0.0), axis=1, keepdims=True), axis=0, keepdims=True)
            return jnp.maximum(carry[0], n0), jnp.maximum(carry[1], n1)

        z = jnp.zeros((1, 1), F32)
        n0, n1 = lax.fori_loop(0, nk, body, (z, z))
        kn_s[0] = jnp.broadcast_to(n0, (8, LANE))
        kn_s[1] = jnp.broadcast_to(n1, (8, LANE))

    qb = q_ref[...] * (ATTN_HEAD_DIM ** -0.5)
    lane_q = lax.broadcasted_iota(I32, (tq, LANE), 1)
    q = qb.astype(F32)
    kd = k_ref[pl.ds(pl.multiple_of(t0, tq), tq), :].astype(F32)
    pick_r = lax.broadcasted_iota(I32, (8, LANE), 0)
    pick = jnp.where(lax.broadcasted_iota(I32, (8, LANE), 1) // half == pick_r, 1.0, 0.0)
    pick = pick.astype(BF16)
    qn_rows = lax.dot_general(pick, (q * q).astype(BF16), nt, preferred_element_type=F32)
    self_rows = lax.dot_general(pick, (q * kd).astype(BF16), nt, preferred_element_type=F32)
    bounds = []
    for c in range(2):
        data = (lane_q < half) if c == 0 else (lane_q >= half)
        qa_s[c] = jnp.where(data, qb, qf_ref[c])
        ref_c = jnp.sqrt(qn_rows[c:c + 1, :] * kn_s[c][0:1, 0:1]) * 1.02
        ref_s[c] = ref_c
        bounds.append(jnp.max(ref_c * (1.0 + 2.0 ** -8) - self_rows[c:c + 1, :], axis=1, keepdims=True))

    span = jnp.maximum(bounds[0], bounds[1])
    count = jnp.minimum(span * 0.0 + scal_ref[ATTN_HEADS + 2 + h], math.log(seq))
    dskip = jnp.minimum((span + count + SKIP_MARGIN) / slope, 2.0 * seq)
    t0f = jnp.full((1, 1), t0, I32).astype(F32)
    dlof = jnp.full((1, 1), dlo, I32).astype(F32)
    klo_f = jnp.clip(jnp.floor((t0f + 1.0 - dskip) / tk), 0.0, dlof)
    khi_f = jnp.clip(jnp.ceil((dskip + t0f + (tq - 1.0)) / tk), dlof + ndiag, float(nk))
    klo = jnp.max(klo_f).astype(I32)
    khi = jnp.max(khi_f).astype(I32)
    fixed_ref = jnp.max(jnp.where(span <= FIXED_REF_MAX_RANGE, 1.0, 0.0)).astype(I32)

    m_s[...] = jnp.full(m_s.shape, -jnp.inf, F32)
    acc_s[...] = jnp.zeros(acc_s.shape, F32)
    ones = jnp.ones((ONES_ROWS, tk), BF16)

    def block(ki, side, fixed, d=0):
        s0 = pl.multiple_of(ki * tk, tk)
        kblk = k_ref[pl.ds(s0, tk), :]
        vaug = jnp.concatenate([vt_ref[:, pl.ds(s0, tk)], ones], axis=0)
        off = jnp.full((1, tq), t0 - s0, I32).astype(F32) * slope
        if side == 'D':
            right = ndiag > 1 and d == ndiag - 1
            c_lo, c_hi = (d * tk, tq) if right else (0, (d + 1) * tk)
            rel = (lax.broadcasted_iota(I32, (tk, c_hi - c_lo), 0) + (d * tk - c_lo)
                   - lax.broadcasted_iota(I32, (tk, c_hi - c_lo), 1)).astype(F32)
            corr = jnp.maximum(-rel if right else rel, 0.0) * (-2.0 * slope)
            feats = 'R' if right else 'L'
        else:
            feats = side
        cblk = -off if feats == 'R' else off
        for c in range(2):
            data = (lane_k < half) if c == 0 else (lane_k >= half)
            kaug = jnp.where(data, kblk, kf_ref[2 * c + (1 if feats == 'R' else 0)])
            st = lax.dot_general(kaug, qa_s[c], nt, preferred_element_type=F32)
            if side == 'D':
                pieces = [st[:, c_lo:c_hi] + corr]
                if c_lo > 0:
                    pieces.insert(0, st[:, :c_lo])
                if c_hi < tq:
                    pieces.append(st[:, c_hi:])
                st = jnp.concatenate(pieces, axis=1) if len(pieces) > 1 else pieces[0]
            if fixed:
                p = jnp.exp(st - (ref_s[c] + cblk)).astype(BF16)
                acc_s[c] += jnp.dot(vaug, p, preferred_element_type=F32)
            else:
                m_prev = m_s[c]
                m_new = jnp.maximum(m_prev, jnp.max(st, axis=0, keepdims=True) - cblk)
                p = jnp.exp(st - (m_new + cblk)).astype(BF16)
                alpha = jnp.exp(m_prev - m_new)
                acc_s[c] = acc_s[c] * alpha + jnp.dot(vaug, p, preferred_element_type=F32)
                m_s[c] = m_new

    def sweep(fixed):
        def run(side):
            def body(ki, carry):
                block(ki, side, fixed)
                return carry
            return body

        lax.fori_loop(klo, dlo, run('L'), 0)
        for d in range(ndiag):
            block(dlo + d, 'D', fixed, d)
        lax.fori_loop(dlo + ndiag, khi, run('R'), 0)

    @pl.when(fixed_ref == 1)
    def _():
        sweep(True)

    @pl.when(fixed_ref == 0)
    def _():
        sweep(False)

    lam = scal_ref[ATTN_HEADS]
    post = scal_ref[ATTN_HEADS + 1]
    a0 = acc_s[0]
    a1 = acc_s[1]
    o = a0[0:ATTN_V_DIM] / a0[ATTN_V_DIM:ATTN_V_DIM + 1] - lam * (a1[0:ATTN_V_DIM] / a1[ATTN_V_DIM:ATTN_V_DIM + 1])
    o = o * lax.rsqrt(jnp.mean(o * o, axis=0, keepdims=True) + SUBLN_EPS) * (w_ref[...] * post)
    o_ref[...] = o.T.astype(o_ref.dtype)


def _position_features(slopes, tq, tk):
    half = ATTN_HEAD_DIM
    sl = slopes.reshape(-1, 1, 1).astype(F32)
    lane = jnp.arange(LANE)[None, None, :]
    one = jnp.ones((1, 1, 1), F32)

    def tile(n, vals, base):
        out = jnp.zeros((slopes.shape[0], n, LANE), F32)
        for i, v in enumerate(vals):
            out = jnp.where(lane == base + i, v, out)
        return out

    def split(n):
        pos = jnp.arange(n, dtype=F32)[None, :, None]
        lo = jnp.mod(pos, float(POS_SPLIT))
        return pos - lo, lo

    qhi, qlo = split(tq)
    khi, klo = split(tk)
    qf, kf = [], []
    for c in range(2):
        base = half if c == 0 else 0
        qf.append(tile(tq, (-sl * qhi, -sl * qlo, one, one), base))
        left = tile(tk, (one, one, sl * khi, sl * klo), base)
        kf += [left, -left]
    return jnp.stack(qf, axis=1).astype(BF16), jnp.stack(kf, axis=1).astype(BF16)


def diff_attention(proj3, vt, scal, subln_w, tq, tk):
    b, s, _ = proj3.shape
    kq, kk = C_Q // LANE, C_K // LANE
    qfeat, kfeat = _position_features(scal[:ATTN_HEADS], tq, tk)
    return pl.pallas_call(
        functools.partial(_attn_kernel, tq=tq, tk=tk, seq=s),
        grid=(b, ATTN_HEADS, s // tq),
        in_specs=[
            pl.BlockSpec(memory_space=pltpu.SMEM),
            pl.BlockSpec((None, tq, LANE), lambda bi, h, qi: (bi, qi, kq + h)),
            pl.BlockSpec((None, s, LANE), lambda bi, h, qi: (bi, 0, kk + h)),
            pl.BlockSpec((ATTN_V_DIM, s), lambda bi, h, qi: (h, bi)),
            pl.BlockSpec((ATTN_V_DIM, 1), lambda bi, h, qi: (0, 0)),
            pl.BlockSpec((None, 2, tq, LANE), lambda bi, h, qi: (h, 0, 0, 0)),
            pl.BlockSpec((None, 4, tk, LANE), lambda bi, h, qi: (h, 0, 0, 0)),
        ],
        out_specs=pl.BlockSpec((None, tq, LANE), lambda bi, h, qi: (bi, qi, h)),
        out_shape=jax.ShapeDtypeStruct((b, s, ATTN_HEADS * ATTN_V_DIM), BF16),
        scratch_shapes=[
            pltpu.VMEM((2, tq, LANE), BF16),
            pltpu.VMEM((2, 1, tq), F32),
            pltpu.VMEM((2, ATTN_V_DIM + ONES_ROWS, tq), F32),
            pltpu.VMEM((2, 8, LANE), F32),
            pltpu.VMEM((2, 1, tq), F32),
        ],
        compiler_params=_cparams(("parallel", "parallel", "arbitrary"), VMEM_LIMIT),
        name="diff_attention",
    )(scal, proj3, proj3, vt, subln_w.reshape(ATTN_V_DIM, 1), qfeat, kfeat)


def _proj_t_kernel(x_ref, nw_ref, wvt_ref, wdt_ref, wdtt_ref, vt_ref, dt_ref, dtt_ref):
    xn = _rms(x_ref[...], nw_ref[...], EPS).astype(BF16)
    nt = (((1,), (1,)), ((), ()))
    vt_ref[...] = lax.dot_general(wvt_ref[...], xn, nt, preferred_element_type=F32).astype(vt_ref.dtype)
    dt_ref[...] = jnp.dot(xn, wdt_ref[...], preferred_element_type=F32)
    dtt_ref[...] = lax.dot_general(wdtt_ref[...], xn, nt, preferred_element_type=F32)


def proj_transposed(x, nw, wvt, wdt, wdtt, tm):
    t, d = x.shape
    nv = wvt.shape[0]
    const = lambda i: (0, 0)
    return pl.pallas_call(
        _proj_t_kernel,
        grid=(t // tm,),
        in_specs=[
            pl.BlockSpec((tm, d), lambda i: (i, 0)),
            pl.BlockSpec((1, d), const),
            pl.BlockSpec((nv, d), const),
            pl.BlockSpec((d, LANE), const),
            pl.BlockSpec((LANE, d), const),
        ],
        out_specs=[pl.BlockSpec((nv, tm), lambda i: (0, i)), pl.BlockSpec((tm, LANE), lambda i: (i, 0)),
                   pl.BlockSpec((LANE, tm), lambda i: (0, i))],
        out_shape=[jax.ShapeDtypeStruct((nv, t), BF16), jax.ShapeDtypeStruct((t, LANE), F32),
                   jax.ShapeDtypeStruct((LANE, t), F32)],
        compiler_params=_cparams(("parallel",), VMEM_LIMIT),
        name="proj_transposed",
    )(x, nw.reshape(1, d), wvt, wdt, wdtt)


HALO = 16


CONV_SUB = 128


def _conv_kernel(prev_ref, cur_ref, next_ref, w_ref, b_ref, o_ref, ext_ref, *, tm):
    i = pl.program_id(1)
    last = pl.num_programs(1) - 1
    zero = jnp.zeros(prev_ref.shape, BF16)
    ext_ref[pl.ds(0, HALO), :] = jnp.where(i > 0, prev_ref[...], zero)
    ext_ref[pl.ds(HALO, tm), :] = cur_ref[...]
    ext_ref[pl.ds(HALO + tm, HALO), :] = jnp.where(i < last, next_ref[...], zero)
    w = w_ref[...]
    bias = b_ref[...]
    taps = [j for j in range(D_CONV) if j != D_CONV // 2]
    win = CONV_SUB + 2 * HALO
    r = lax.broadcasted_iota(I32, (len(taps) * CONV_SUB, win), 0)
    c = lax.broadcasted_iota(I32, (len(taps) * CONV_SUB, win), 1)
    n = r >> (CONV_SUB.bit_length() - 1)
    shift = jnp.where(n < D_CONV // 2, n, n + 1) - D_CONV // 2
    pick = jnp.where(c == HALO + (r - n * CONV_SUB) + shift, 1.0, 0.0).astype(BF16)
    for sb in range(tm // CONV_SUB):
        xe = ext_ref[pl.ds(sb * CONV_SUB, win), :]
        moved = jnp.dot(pick, xe, preferred_element_type=F32)
        acc = bias + xe[HALO:HALO + CONV_SUB].astype(F32) * w[D_CONV // 2:D_CONV // 2 + 1, :]
        for k, j in enumerate(taps):
            acc = acc + moved[k * CONV_SUB:(k + 1) * CONV_SUB] * w[j:j + 1, :]
        o_ref[pl.ds(sb * CONV_SUB, CONV_SUB), :] = _silu(acc).astype(o_ref.dtype)


def conv_silu(proj3, conv_w, conv_b, tm, tc):
    b, s, _ = proj3.shape
    c0 = C_XBC // tc
    hb = tm // HALO
    nh = s // HALO
    return pl.pallas_call(
        functools.partial(_conv_kernel, tm=tm),
        grid=(b, s // tm, XBC_DIM // tc),
        in_specs=[
            pl.BlockSpec((None, HALO, tc), lambda bi, i, j: (bi, jnp.maximum(i * hb - 1, 0), c0 + j)),
            pl.BlockSpec((None, tm, tc), lambda bi, i, j: (bi, i, c0 + j)),
            pl.BlockSpec((None, HALO, tc), lambda bi, i, j: (bi, jnp.minimum((i + 1) * hb, nh - 1), c0 + j)),
            pl.BlockSpec((D_CONV, tc), lambda bi, i, j: (0, j)),
            pl.BlockSpec((1, tc), lambda bi, i, j: (0, j)),
        ],
        out_specs=pl.BlockSpec((None, tm, tc), lambda bi, i, j: (bi, i, j)),
        out_shape=jax.ShapeDtypeStruct((b, s, XBC_DIM), BF16),
        scratch_shapes=[pltpu.VMEM((tm + 2 * HALO, tc), BF16)],
        compiler_params=_cparams(("parallel", "parallel", "parallel"), VMEM_LIMIT),
        name="conv_silu",
    )(proj3, proj3, proj3, conv_w, conv_b.reshape(1, XBC_DIM))


def _expand_heads(a):
    hi = a.astype(BF16)
    lo = (a - hi.astype(F32)).astype(BF16)
    head_of = lax.broadcasted_iota(I32, (SSM_HEADS, D_INNER), 1) // SSM_HEAD_DIM
    e = jnp.where(head_of == lax.broadcasted_iota(I32, (SSM_HEADS, D_INNER), 0), 1.0, 0.0).astype(BF16)
    return jnp.dot(hi, e, preferred_element_type=F32) + jnp.dot(lo, e, preferred_element_type=F32)


def _tri(n, fn):
    r = lax.broadcasted_iota(I32, (n, n), 0)
    c = lax.broadcasted_iota(I32, (n, n), 1)
    return jnp.where(fn(r, c), 1.0, 0.0).astype(F32)


def _dot_hi(a, b):
    return jnp.dot(a, b, preferred_element_type=F32, precision=lax.Precision.HIGHEST)


def _ssd_state_kernel(xf_ref, bf_ref, dtf_ref, xb_ref, bb_ref, dtb_ref, par_ref, hf_ref, hb_ref, sf_s, sb_s):
    j = pl.program_id(1)

    @pl.when(j == 0)
    def _():
        sf_s[...] = jnp.zeros(sf_s.shape, F32)
        sb_s[...] = jnp.zeros(sb_s.shape, F32)

    hf_ref[...] = sf_s[...].astype(hf_ref.dtype)
    hb_ref[...] = sb_s[...].astype(hb_ref.dtype)

    par = par_ref[...]
    lower_incl = _tri(CHUNK, lambda r, c: c <= r)
    lower_strict = _tri(CHUNK, lambda r, c: c < r)

    def one_direction(x_ref, b_ref, dt_ref, s_ref, col0, backward):
        raw = dt_ref[...][:, col0:col0 + SSM_HEADS]
        dt = _softplus(raw + par[0:1, col0:col0 + SSM_HEADS])
        a = dt * par[1:2, col0:col0 + SSM_HEADS]
        if backward:
            excl = _dot_hi(lower_strict, a)
            wgt = dt * jnp.exp(excl)
            total = excl[CHUNK - 1:CHUNK, :] + a[CHUNK - 1:CHUNK, :]
        else:
            cum = _dot_hi(lower_incl, a)
            total = cum[CHUNK - 1:CHUNK, :]
            wgt = dt * jnp.exp(total - cum)
        xw = (x_ref[...].astype(F32) * _expand_heads(wgt)).astype(BF16)
        dec = _expand_heads(jnp.exp(jnp.broadcast_to(total, (8, SSM_HEADS))))[0:1, :]
        bmat = b_ref[...]
        gw = D_INNER // SSM_GROUPS
        for g in range(SSM_GROUPS):
            contrib = lax.dot_general(bmat[:, g * D_STATE:(g + 1) * D_STATE], xw[:, g * gw:(g + 1) * gw],
                                      (((0,), (0,)), ((), ())), preferred_element_type=F32)
            s_ref[g] = s_ref[g] * dec[:, g * gw:(g + 1) * gw] + contrib

    one_direction(xf_ref, bf_ref, dtf_ref, sf_s, 0, False)
    one_direction(xb_ref, bb_ref, dtb_ref, sb_s, SSM_HEADS, True)


def ssd_states(xbc, dt3, par):
    b, s, _ = xbc.shape
    nc = s // CHUNK
    gw = D_INNER // SSM_GROUPS
    xblk = D_INNER // D_INNER
    bcol = D_INNER // (SSM_GROUPS * D_STATE)
    hshape = jax.ShapeDtypeStruct((b, nc, SSM_GROUPS, D_STATE, gw), BF16)
    hspec_f = pl.BlockSpec((None, None, SSM_GROUPS, D_STATE, gw), lambda bi, j: (bi, j, 0, 0, 0))
    hspec_b = pl.BlockSpec((None, None, SSM_GROUPS, D_STATE, gw), lambda bi, j: (bi, nc - 1 - j, 0, 0, 0))
    del xblk
    return pl.pallas_call(
        _ssd_state_kernel,
        grid=(b, nc),
        in_specs=[
            pl.BlockSpec((None, CHUNK, D_INNER), lambda bi, j: (bi, j, 0)),
            pl.BlockSpec((None, CHUNK, SSM_GROUPS * D_STATE), lambda bi, j: (bi, j, bcol)),
            pl.BlockSpec((None, CHUNK, LANE), lambda bi, j: (bi, j, 0)),
            pl.BlockSpec((None, CHUNK, D_INNER), lambda bi, j: (bi, nc - 1 - j, 0)),
            pl.BlockSpec((None, CHUNK, SSM_GROUPS * D_STATE), lambda bi, j: (bi, nc - 1 - j, bcol)),
            pl.BlockSpec((None, CHUNK, LANE), lambda bi, j: (bi, nc - 1 - j, 0)),
            pl.BlockSpec((8, LANE), lambda bi, j: (0, 0)),
        ],
        out_specs=[hspec_f, hspec_b],
        out_shape=[hshape, hshape],
        scratch_shapes=[pltpu.VMEM((SSM_GROUPS, D_STATE, gw), F32), pltpu.VMEM((SSM_GROUPS, D_STATE, gw), F32)],
        compiler_params=_cparams(("parallel", "arbitrary"), VMEM_LIMIT),
        name="ssd_states",
    )(xbc, xbc, dt3, xbc, xbc, dt3, par)


def _ssd_out_kernel(x_ref, b_ref, c_ref, z_ref, dt_ref, dtt_ref, par_ref, part_ref, dsk_ref, nw_ref,
                    hf_ref, hb_ref, o_ref):
    par = par_ref[...]
    part = part_ref[...]
    lower_incl = _tri(CHUNK, lambda r, c: c <= r)
    lower_strict = _tri(CHUNK, lambda r, c: c < r)
    upper_incl = _tri(CHUNK, lambda r, c: r <= c)
    upper_strict = _tri(CHUNK, lambda r, c: r < c)

    dtc = _softplus(dt_ref[...][:, 0:2 * SSM_HEADS] + par[0:1, 0:2 * SSM_HEADS])
    ac = dtc * par[1:2, 0:2 * SSM_HEADS]
    cum_c = _dot_hi(lower_incl, ac)
    excl_c = _dot_hi(lower_strict, ac)
    dtr = _softplus(dtt_ref[...][0:2 * SSM_HEADS, :] + part[0:2 * SSM_HEADS, 0:1])
    ar = dtr * part[0:2 * SSM_HEADS, 1:2]
    cum_r = _dot_hi(ar, upper_incl)
    excl_r = _dot_hi(ar, upper_strict)

    cumf_c = cum_c[:, 0:SSM_HEADS]
    exclb_c = excl_c[:, SSM_HEADS:2 * SSM_HEADS]
    totb = cum_c[CHUNK - 1:CHUNK, SSM_HEADS:2 * SSM_HEADS]
    dtf_r = dtr[0:SSM_HEADS, :]
    dtb_r = dtr[SSM_HEADS:2 * SSM_HEADS, :]
    fwd_r = cum_r[0:SSM_HEADS, :] - jnp.log(dtf_r)
    bwd_r = excl_r[SSM_HEADS:2 * SSM_HEADS, :] + jnp.log(dtb_r)
    diag_r = jnp.log(dtf_r + dtb_r)

    row = lax.broadcasted_iota(I32, (CHUNK, CHUNK), 0)
    col = lax.broadcasted_iota(I32, (CHUNK, CHUNK), 1)
    low = col < row
    diag = col == row
    lane = lax.broadcasted_iota(I32, (CHUNK, LANE), 1)
    first = lane < SSM_HEAD_DIM

    gw = D_INNER // SSM_GROUPS
    hpg = SSM_HEADS // SSM_GROUPS
    scale_f = _expand_heads(jnp.exp(cumf_c))
    scale_b = _expand_heads(jnp.exp(totb - exclb_c))
    for g in range(SSM_GROUPS):
        cg = c_ref[:, g * D_STATE:(g + 1) * D_STATE]
        bg = b_ref[:, g * D_STATE:(g + 1) * D_STATE]
        cb = lax.dot_general(cg, bg, (((1,), (1,)), ((), ())), preferred_element_type=F32)
        inter_f = jnp.dot(cg, hf_ref[g], preferred_element_type=F32)
        inter_b = jnp.dot(cg, hb_ref[g], preferred_element_type=F32)
        parts = []
        for pp in range(hpg // 2):
            gms = []
            for h in (g * hpg + 2 * pp, g * hpg + 2 * pp + 1):
                arg = jnp.where(low, cumf_c[:, h:h + 1] - fwd_r[h:h + 1, :],
                                bwd_r[h:h + 1, :] - exclb_c[:, h:h + 1])
                arg = jnp.where(diag, diag_r[h:h + 1, :], arg)
                gms.append((cb * jnp.exp(arg)).astype(BF16))
            ls = slice(g * gw + pp * LANE, g * gw + (pp + 1) * LANE)
            il = slice(pp * LANE, (pp + 1) * LANE)
            xp = x_ref[:, ls]
            rhs = jnp.concatenate([jnp.where(first, xp, 0), jnp.where(first, 0, xp)], axis=0)
            yp = jnp.dot(jnp.concatenate(gms, axis=1), rhs, preferred_element_type=F32)
            yp = (yp + inter_f[:, il] * scale_f[:, ls] + inter_b[:, il] * scale_b[:, ls]
                  + xp.astype(F32) * dsk_ref[:, ls])
            parts.append(yp * _silu(z_ref[:, ls].astype(F32)))
        yg = jnp.concatenate(parts, axis=1)
        sl = slice(g * gw, (g + 1) * gw)
        o_ref[:, sl] = (yg * lax.rsqrt(jnp.mean(yg * yg, axis=-1, keepdims=True) + EPS)
                        * nw_ref[:, sl]).astype(o_ref.dtype)


def ssd_output(xbc, proj3, dt3, dtt3, par, part, dskip_x, nw, hf, hb):
    b, s, _ = xbc.shape
    nc = s // CHUNK
    gw = D_INNER // SSM_GROUPS
    gs = SSM_GROUPS * D_STATE
    hspec = pl.BlockSpec((None, None, SSM_GROUPS, D_STATE, gw), lambda bi, j: (bi, j, 0, 0, 0))
    return pl.pallas_call(
        _ssd_out_kernel,
        grid=(b, nc),
        in_specs=[
            pl.BlockSpec((None, CHUNK, D_INNER), lambda bi, j: (bi, j, 0)),
            pl.BlockSpec((None, CHUNK, gs), lambda bi, j: (bi, j, D_INNER // gs)),
            pl.BlockSpec((None, CHUNK, gs), lambda bi, j: (bi, j, D_INNER // gs + 1)),
            pl.BlockSpec((None, CHUNK, D_INNER), lambda bi, j: (bi, j, C_Z // D_INNER)),
            pl.BlockSpec((None, CHUNK, LANE), lambda bi, j: (bi, j, 0)),
            pl.BlockSpec((LANE, CHUNK), lambda bi, j: (0, bi * nc + j)),
            pl.BlockSpec((8, LANE), lambda bi, j: (0, 0)),
            pl.BlockSpec((LANE, 8), lambda bi, j: (0, 0)),
            pl.BlockSpec((1, D_INNER), lambda bi, j: (0, 0)),
            pl.BlockSpec((1, D_INNER), lambda bi, j: (0, 0)),
            hspec, hspec,
        ],
        out_specs=pl.BlockSpec((None, CHUNK, D_INNER), lambda bi, j: (bi, j, 0)),
        out_shape=jax.ShapeDtypeStruct((b, s, D_INNER), BF16),
        compiler_params=_cparams(("parallel", "parallel"), VMEM_LIMIT),
        name="ssd_output",
    )(xbc, xbc, xbc, proj3, dt3, dtt3, par, part, dskip_x, nw, hf, hb)


def _merge_kernel(x_ref, a_ref, s_ref, g0_ref, g1_ref, wa_ref, ws_ref, wo_ref, o_ref):
    ya = jnp.dot(a_ref[...], wa_ref[...], preferred_element_type=F32)
    ys = jnp.dot(s_ref[...], ws_ref[...], preferred_element_type=F32)
    g0 = 1.0 / (1.0 + jnp.exp(-g0_ref[...].astype(F32)))
    g1 = 1.0 / (1.0 + jnp.exp(-g1_ref[...].astype(F32)))
    merged = (g0 * ya + g1 * ys).astype(BF16)
    o_ref[...] = x_ref[...] + jnp.dot(merged, wo_ref[...], preferred_element_type=F32)


def merge_out(x, attn, ssm, proj, wa, ws, wo, tm):
    t, d = x.shape
    gb = C_GATE // d
    const = lambda i: (0, 0)
    return pl.pallas_call(
        _merge_kernel,
        grid=(t // tm,),
        in_specs=[
            pl.BlockSpec((tm, d), lambda i: (i, 0)),
            pl.BlockSpec((tm, d), lambda i: (i, 0)),
            pl.BlockSpec((tm, D_INNER), lambda i: (i, 0)),
            pl.BlockSpec((tm, d), lambda i: (i, gb)),
            pl.BlockSpec((tm, d), lambda i: (i, gb + 1)),
            pl.BlockSpec((d, d), const),
            pl.BlockSpec((D_INNER, d), const),
            pl.BlockSpec((d, d), const),
        ],
        out_specs=pl.BlockSpec((tm, d), lambda i: (i, 0)),
        out_shape=jax.ShapeDtypeStruct((t, d), F32),
        compiler_params=_cparams(("parallel",), VMEM_LIMIT),
        name="merge_out",
    )(x, attn, ssm, proj, proj, wa, ws, wo)


def _cross_kernel(x_ref, nw_ref, wq_ref, kv_ref, wo_ref, o_ref):
    x = x_ref[...]
    h = _rms(x, nw_ref[...], EPS).astype(BF16)
    q = jnp.dot(h, wq_ref[...], preferred_element_type=F32).astype(BF16)
    kv = kv_ref[...]
    outs = []
    for hd in range(X_HEADS):
        qh = q[:, hd * X_HEAD_DIM:(hd + 1) * X_HEAD_DIM]
        kh = kv[:, hd * X_HEAD_DIM:(hd + 1) * X_HEAD_DIM]
        vh = kv[:, X_WIDTH + hd * X_HEAD_DIM:X_WIDTH + (hd + 1) * X_HEAD_DIM]
        s = lax.dot_general(qh, kh, (((1,), (1,)), ((), ())), preferred_element_type=F32) * (X_HEAD_DIM ** -0.5)
        s = s - jnp.max(s, axis=-1, keepdims=True)
        p = jnp.exp(s)
        p = (p / jnp.sum(p, axis=-1, keepdims=True)).astype(BF16)
        outs.append(jnp.dot(p, vh, preferred_element_type=F32).astype(BF16))
    o = jnp.concatenate(outs, axis=1)
    o_ref[...] = x + jnp.dot(o, wo_ref[...], preferred_element_type=F32)


def cross_attention(x, nw, wq, kv, wo, seq, tm):
    t, d = x.shape
    per_seq = seq // tm
    const = lambda i: (0, 0)
    return pl.pallas_call(
        _cross_kernel,
        grid=(t // tm,),
        in_specs=[
            pl.BlockSpec((tm, d), lambda i: (i, 0)),
            pl.BlockSpec((1, d), const),
            pl.BlockSpec((d, X_WIDTH), const),
            pl.BlockSpec((None, MEM_TOKENS, 2 * X_WIDTH), lambda i: (i // per_seq, 0, 0)),
            pl.BlockSpec((X_WIDTH, d), const),
        ],
        out_specs=pl.BlockSpec((tm, d), lambda i: (i, 0)),
        out_shape=jax.ShapeDtypeStruct((t, d), F32),
        compiler_params=_cparams(("parallel",), VMEM_LIMIT),
        name="cross_attention",
    )(x, nw.reshape(1, d), wq, kv, wo)


def _router_kernel(x_ref, nw_ref, wrt_ref, br_ref, hb_ref, aff_ref):
    h = _rms(x_ref[...], nw_ref[...], EPS)
    hb_ref[...] = h.astype(hb_ref.dtype)
    logits = lax.dot_general(wrt_ref[...], h, (((1,), (1,)), ((), ())), preferred_element_type=F32,
                             precision=lax.Precision.HIGHEST) + br_ref[...]
    e = jnp.exp(logits - jnp.max(logits, axis=0, keepdims=True))
    aff_ref[...] = e / jnp.sum(e, axis=0, keepdims=True)


def router(x, nw, wrt, br, tm):
    t, d = x.shape
    return pl.pallas_call(
        _router_kernel,
        grid=(t // tm,),
        in_specs=[
            pl.BlockSpec((tm, d), lambda i: (i, 0)),
            pl.BlockSpec((1, d), lambda i: (0, 0)),
            pl.BlockSpec((N_EXPERTS, d), lambda i: (0, 0)),
            pl.BlockSpec((N_EXPERTS, 1), lambda i: (0, 0)),
        ],
        out_specs=[pl.BlockSpec((tm, d), lambda i: (i, 0)), pl.BlockSpec((N_EXPERTS, tm), lambda i: (0, i))],
        out_shape=[jax.ShapeDtypeStruct((t, d), BF16), jax.ShapeDtypeStruct((N_EXPERTS, t), F32)],
        compiler_params=_cparams(("parallel",), VMEM_LIMIT),
        name="router",
    )(x, nw.reshape(1, d), wrt, br.reshape(N_EXPERTS, 1))


TB = 256


def _topk_kernel(aff_ref, pos_ref, gate_ref, roff_ref, *, cap):
    a = aff_ref[...]
    nr = a.shape[1]
    bits = pltpu.bitcast(a, I32)

    def count(mask):
        c = jnp.sum(jnp.where(mask, 1.0, 0.0), axis=2, keepdims=True)
        return jnp.sum(c, axis=1, keepdims=True)

    def body(i, thr):
        cand = thr | jnp.left_shift(jnp.int32(1), 30 - i)
        return jnp.where(count(bits >= cand) >= float(cap), cand, thr)

    thr = lax.fori_loop(0, 31, body, jnp.zeros((N_EXPERTS, 1, 1), I32))
    gt = bits > thr
    eq = bits == thr
    need = float(cap) - count(gt)

    ustrict = _tri(TB, lambda r, c: r < c).astype(BF16)
    ones = jnp.ones((TB, TB), BF16)
    lstrict = _tri(nr, lambda r, c: c < r).astype(BF16)

    def excl_prefix(m):
        mb = m.astype(BF16)
        within = jnp.dot(mb, ustrict, preferred_element_type=F32)
        rowsum = jnp.dot(mb, ones, preferred_element_type=F32)
        rowoff = jnp.dot(lstrict, rowsum.astype(BF16), preferred_element_type=F32)
        return within + rowoff, rowoff

    for e in range(N_EXPERTS):
        eq_e = jnp.where(eq[e], 1.0, 0.0)
        pe, _ = excl_prefix(eq_e)
        keep = jnp.where(pe < need[e], eq_e, 0.0)
        sel = jnp.where(gt[e], 1.0, keep)
        ps, roff = excl_prefix(sel)
        chosen = sel > 0.5
        pos_ref[e] = jnp.where(chosen, ps.astype(I32), -1)
        gate_ref[e] = jnp.where(chosen, a[e], 0.0)
        roff_ref[e] = roff.astype(I32)


def topk_select(aff3, cap):
    e, r, tb = aff3.shape
    full = lambda: (0, 0, 0)
    return pl.pallas_call(
        functools.partial(_topk_kernel, cap=cap),
        grid=(),
        in_specs=[pl.BlockSpec((e, r, tb), full)],
        out_specs=[pl.BlockSpec((e, r, tb), full)] * 3,
        out_shape=[jax.ShapeDtypeStruct((e, r, tb), I32), jax.ShapeDtypeStruct((e, r, tb), F32),
                   jax.ShapeDtypeStruct((e, r, tb), I32)],
        compiler_params=pltpu.CompilerParams(vmem_limit_bytes=VMEM_LIMIT),
        name="topk_select",
    )(aff3)


GATHER_BUFS = 8


GATHER_PARTS = 2


def _ffn_kernel(lo_ref, hi_ref, roff_ref, rend_ref, pos_ref, gate_ref, hb_hbm, wg_ref, wu_ref, wd_ref, y_ref,
                xbuf, sem, xc_s, g_s, *, ts, nj):
    e = pl.program_id(0)
    j = pl.program_id(1)
    nr = pos_ref.shape[0]
    step = e * nj + j
    nsteps = pl.num_programs(0) * nj
    lo = lo_ref[step]
    hi = hi_ref[step]
    xc_s[...] = jnp.zeros(xc_s.shape, F32)
    g_s[...] = jnp.zeros(g_s.shape, F32)

    def fetch(r, slot):
        return pltpu.make_async_copy(hb_hbm.at[pl.ds(r * TB, TB)], xbuf.at[slot], sem.at[slot])

    def prime(first, end):
        for i in range(GATHER_BUFS - 1):
            @pl.when(first + i < end)
            def _(i=i):
                fetch(first + i, i).start()

    @pl.when(step == 0)
    def _():
        prime(lo, hi)

    part = ts // GATHER_PARTS
    slot_ids = lax.broadcasted_iota(I32, (part, TB), 0)

    def body(r, carry):
        idx = r - lo
        slot = idx & (GATHER_BUFS - 1)
        fetch(r, slot).wait()

        @pl.when(r + (GATHER_BUFS - 1) < hi)
        def _():
            fetch(r + (GATHER_BUFS - 1), (idx + (GATHER_BUFS - 1)) & (GATHER_BUFS - 1)).start()

        prow = pos_ref[pl.ds(r, 1), :]
        grow = gate_ref[pl.ds(r, 1), :]
        first = roff_ref[e * nr + r]
        last = rend_ref[e * nr + r]
        for p in range(GATHER_PARTS):
            base = j * ts + p * part

            @pl.when(jnp.logical_and(first < base + part, last > base))
            def _(p=p, base=base):
                hit = prow == slot_ids + base
                sel = jnp.where(hit, 1.0, 0.0).astype(BF16)
                rows = slice(p * part, (p + 1) * part)
                xc_s[rows, :] += jnp.dot(sel, xbuf[slot], preferred_element_type=F32)
                g_s[rows, :] += jnp.sum(jnp.where(hit, grow, 0.0), axis=1, keepdims=True)
        return carry

    lax.fori_loop(lo, hi, body, 0)

    @pl.when(step + 1 < nsteps)
    def _():
        prime(lo_ref[step + 1], hi_ref[step + 1])

    xc = xc_s[...].astype(BF16)
    hg = jnp.dot(xc, wg_ref[...], preferred_element_type=F32)
    hu = jnp.dot(xc, wu_ref[...], preferred_element_type=F32)
    hid = (_silu(hg) * hu).astype(BF16)
    y = jnp.dot(hid, wd_ref[...], preferred_element_type=F32) * g_s[...]
    y_ref[...] = y.astype(y_ref.dtype)


def expert_ffn(lo, hi, roff, rend, pos, gate, hb, wg, wu, wd, cap, ts):
    e, r, tb = pos.shape
    d = hb.shape[1]
    nj = cap // ts
    grid_spec = pltpu.PrefetchScalarGridSpec(
        num_scalar_prefetch=4,
        grid=(e, nj),
        in_specs=[
            pl.BlockSpec((None, r, tb), lambda ei, j, *_: (ei, 0, 0)),
            pl.BlockSpec((None, r, tb), lambda ei, j, *_: (ei, 0, 0)),
            pl.BlockSpec(memory_space=pl.ANY),
            pl.BlockSpec((None, d, D_FF), lambda ei, j, *_: (ei, 0, 0)),
            pl.BlockSpec((None, d, D_FF), lambda ei, j, *_: (ei, 0, 0)),
            pl.BlockSpec((None, D_FF, d), lambda ei, j, *_: (ei, 0, 0)),
        ],
        out_specs=pl.BlockSpec((None, ts, d), lambda ei, j, *_: (ei, j, 0)),
        scratch_shapes=[
            pltpu.VMEM((GATHER_BUFS, TB, d), BF16),
            pltpu.SemaphoreType.DMA((GATHER_BUFS,)),
            pltpu.VMEM((ts, d), F32),
            pltpu.VMEM((ts, 1), F32),
        ],
    )
    return pl.pallas_call(
        functools.partial(_ffn_kernel, ts=ts, nj=nj),
        grid_spec=grid_spec,
        out_shape=jax.ShapeDtypeStruct((e, cap, d), BF16),
        compiler_params=_cparams(("arbitrary", "arbitrary"), VMEM_LIMIT),
        name="expert_ffn",
    )(lo, hi, roff.reshape(-1), rend.reshape(-1), pos, gate, hb, wg, wu, wd)


WIN_MAIN = 64
WIN_OVER = TB + BF16_SUBLANE - WIN_MAIN
MXU_DEPTH = 256


def _combine_kernel(start_ref, ostart_ref, ovf_ref, x_ref, post_ref, nw_ref, *rest, final):
    y_main = rest[:N_EXPERTS]
    y_over = rest[N_EXPERTS:2 * N_EXPERTS]
    o_ref = rest[2 * N_EXPERTS]
    r = pl.program_id(0)
    nr = pl.num_programs(0)
    post = post_ref[...]
    rels = [post[:, e:e + 1] - start_ref[e * nr + r] for e in range(N_EXPERTS)]
    per = MXU_DEPTH // WIN_MAIN
    lane = lax.broadcasted_iota(I32, (TB, MXU_DEPTH), 1)
    acc = x_ref[...]
    for e0 in range(0, N_EXPERTS, per):
        want = rels[e0 + per - 1] + (per - 1) * WIN_MAIN
        for k in range(per - 2, -1, -1):
            want = jnp.where(lane < (k + 1) * WIN_MAIN, rels[e0 + k] + k * WIN_MAIN, want)
        sel = jnp.where(want == lane, 1.0, 0.0).astype(BF16)
        rows = jnp.concatenate([y_main[e0 + k][...] for k in range(per)], axis=0)
        acc = acc + jnp.dot(sel, rows, preferred_element_type=F32)
    o_ref[...] = acc

    @pl.when(ovf_ref[r] != 0)
    def _():
        lane_o = lax.broadcasted_iota(I32, (TB, WIN_OVER), 1) + WIN_MAIN
        more = o_ref[...]
        for e in range(N_EXPERTS):
            sel = jnp.where(rels[e] == lane_o, 1.0, 0.0).astype(BF16)
            more = more + jnp.dot(sel, y_over[e][...], preferred_element_type=F32)
        o_ref[...] = more

    if final:
        o_ref[...] = _rms(o_ref[...], nw_ref[...], EPS)


def combine(x, post, start, ostart, ovf, y, norm_w, final):
    t, d = x.shape
    e, cap, _ = y.shape
    nr = t // TB

    def y_spec(ei, rows, which):
        return pl.BlockSpec((pl.Element(rows), pl.Element(d)),
                            lambda r, st, ost, ov: (pl.multiple_of(ei * cap + (st, ost)[which][ei * nr + r],
                                                                   BF16_SUBLANE), 0))

    grid_spec = pltpu.PrefetchScalarGridSpec(
        num_scalar_prefetch=3,
        grid=(nr,),
        in_specs=([pl.BlockSpec((TB, d), lambda r, st, ost, ov: (r, 0)),
                   pl.BlockSpec((TB, e), lambda r, st, ost, ov: (r, 0)),
                   pl.BlockSpec((1, d), lambda r, st, ost, ov: (0, 0))]
                  + [y_spec(ei, WIN_MAIN, 0) for ei in range(e)]
                  + [y_spec(ei, WIN_OVER, 1) for ei in range(e)]),
        out_specs=pl.BlockSpec((TB, d), lambda r, st, ost, ov: (r, 0)),
    )
    y2 = y.reshape(e * cap, d)
    return pl.pallas_call(
        functools.partial(_combine_kernel, final=final),
        grid_spec=grid_spec,
        out_shape=jax.ShapeDtypeStruct((t, d), F32),
        compiler_params=_cparams(("arbitrary",), VMEM_LIMIT),
        name="combine",
    )(start, ostart, ovf, x, post, norm_w.reshape(1, d), *([y2] * (2 * e)))


def _tiles(t, s):
    def fit(n, pref):
        while n % pref:
            pref //= 2
        return pref

    cap = max(1, CAPACITY_FACTOR * t // N_EXPERTS)
    return dict(
        tm_proj=fit(t, 1024), tn_proj=PROJ_COLS // 4,
        tq=fit(s, 1024), tk=fit(s, 512),
        tm_conv=fit(s, 512), tc_conv=1024,
        tm_tok=fit(s, 512),
        cap=cap, ts=fit(cap, 256),
    )


def _prep_layer(l, p):
    w_in = p['w_in'][l]
    o_q, o_k, o_v, o_z, o_xbc, o_dt = 1024, 2048, 3072, 5120, 8192, 8256
    w_main = jnp.concatenate([w_in[:, o_v:o_z], w_in[:, o_dt:], w_in[:, o_z:o_xbc], w_in[:, :o_k]],
                             axis=1).astype(BF16)
    w_vt = w_in[:, o_k:o_v].T.astype(BF16)
    w_dt = jnp.pad(w_in[:, o_xbc:o_dt], ((0, 0), (0, LANE - 2 * SSM_HEADS))).astype(BF16)
    dt_bias = p['dt_bias'][l].reshape(-1).astype(F32)
    a_neg = -jnp.exp(p['a_log'][l].astype(F32)).reshape(-1)
    par = jnp.zeros((8, LANE), F32).at[0, :2 * SSM_HEADS].set(dt_bias).at[1, :2 * SSM_HEADS].set(a_neg)
    part = jnp.zeros((LANE, 8), F32).at[:2 * SSM_HEADS, 0].set(dt_bias).at[:2 * SSM_HEADS, 1].set(a_neg)
    lam_init = 0.8 - 0.6 * math.exp(-0.3 * l)
    lam = (jnp.exp(jnp.sum(p['lambda_q1'][l].astype(F32) * p['lambda_k1'][l].astype(F32)))
           - jnp.exp(jnp.sum(p['lambda_q2'][l].astype(F32) * p['lambda_k2'][l].astype(F32))) + lam_init)
    slopes = jnp.exp2(-8.0 * (jnp.arange(ATTN_HEADS, dtype=F32) + 1.0) / ATTN_HEADS)
    tail = jnp.log(2.0 / (1.0 - jnp.exp(-slopes)))
    scal = jnp.concatenate([slopes, lam.reshape(1), jnp.full((1,), 1.0 - lam_init, F32), tail]).astype(F32)
    return dict(
        norm_mix=p['norm_mix_w'][l], w_main=w_main, w_vt=w_vt, w_dt=w_dt, w_dtt=w_dt.T, par=par, part=part,
        scal=scal,
        subln=p['attn_subln_w'][l].astype(F32),
        conv_w=p['conv_w'][l].astype(F32), conv_b=p['conv_b'][l].astype(F32),
        dskip_x=jnp.repeat(p['d_skip'][l].astype(F32), SSM_HEAD_DIM).reshape(1, D_INNER),
        ssm_nw=p['ssm_norm_w'][l].astype(F32).reshape(1, D_INNER),
        wa=p['w_attn_branch'][l].astype(BF16), ws=p['w_ssm_branch'][l].astype(BF16),
        wo=p['w_out'][l].astype(BF16),
        norm_cross=p['norm_cross_w'][l], norm_mem=p['norm_mem_w'][l],
        wxq=p['w_xq'][l].astype(BF16), wxkv=p['w_xkv'][l].astype(BF16), wxo=p['w_xo'][l].astype(BF16),
        norm_ffn=p['norm_ffn_w'][l], wrt=p['w_router'][l].astype(F32).T, br=p['b_router'][l].astype(F32),
        wg=p['w_gate'][l].astype(BF16), wu=p['w_up'][l].astype(BF16), wd=p['w_down'][l].astype(BF16),
    )


def _layer(x, mem, lw, b, s, norm_final_w, last):
    t = b * s
    tl = _tiles(t, s)
    proj = norm_matmul(x, lw['norm_mix'], lw['w_main'], BF16, tl['tm_proj'], tl['tn_proj'])
    vt, dt, dtt = proj_transposed(x, lw['norm_mix'], lw['w_vt'], lw['w_dt'], lw['w_dtt'], tl['tm_tok'])
    proj3 = proj.reshape(b, s, PROJ_COLS)
    dt3 = dt.reshape(b, s, LANE)
    attn = diff_attention(proj3, vt, lw['scal'], lw['subln'], tl['tq'], tl['tk'])
    xbc = conv_silu(proj3, lw['conv_w'], lw['conv_b'], tl['tm_conv'], tl['tc_conv'])
    hf, hb = ssd_states(xbc, dt3, lw['par'])
    ssm = ssd_output(xbc, proj3, dt3, dtt, lw['par'], lw['part'], lw['dskip_x'], lw['ssm_nw'], hf, hb)
    x = merge_out(x, attn.reshape(t, D_MODEL), ssm.reshape(t, D_INNER), proj, lw['wa'], lw['ws'], lw['wo'],
                  tl['tm_tok'])
    nm = mem.shape[0] * mem.shape[1]
    kv = norm_matmul(mem.reshape(nm, D_MODEL), lw['norm_mem'], lw['wxkv'], BF16, min(nm, 512), 2 * X_WIDTH)
    x = cross_attention(x, lw['norm_cross'], lw['wxq'], kv.reshape(b, MEM_TOKENS, 2 * X_WIDTH), lw['wxo'], s,
                        tl['tm_tok'])
    cap, ts = tl['cap'], tl['ts']
    nr = t // TB
    hbf, aff = router(x, lw['norm_ffn'], lw['wrt'], lw['br'], tl['tm_tok'])
    pos, gate, roff = topk_select(aff.reshape(N_EXPERTS, nr, TB), cap)
    roff = roff[:, :, 0]
    rend = jnp.concatenate([roff[:, 1:], jnp.full((N_EXPERTS, 1), cap, I32)], axis=1)
    edges = jnp.arange(cap // ts, dtype=I32) * ts
    lo = jnp.sum(rend[:, None, :] <= edges[None, :, None], axis=2).astype(I32).reshape(-1)
    hi = jnp.sum(roff[:, None, :] < (edges + ts)[None, :, None], axis=2).astype(I32).reshape(-1)
    y = expert_ffn(lo, hi, roff, rend, pos, gate, hbf, lw['wg'], lw['wu'], lw['wd'], cap, ts)
    start = jnp.minimum((roff // BF16_SUBLANE) * BF16_SUBLANE, cap - (WIN_MAIN + WIN_OVER)).astype(I32)
    ovf = jnp.any(rend - start > WIN_MAIN, axis=0).astype(I32)
    ostart = jnp.where(ovf[None, :] != 0, start + WIN_MAIN, 0).astype(I32)
    post = jnp.transpose(pos.reshape(N_EXPERTS, t))
    return combine(x, post, start.reshape(-1), ostart.reshape(-1), ovf, y, norm_final_w, last)


def _trunk(x, mem, layers, norm_final_w):
    b, s, d = x.shape
    xt = x.reshape(b * s, d)
    for l, lw in enumerate(layers):
        xt = _layer(xt, mem, lw, b, s, norm_final_w, l == len(layers) - 1)
    return xt.reshape(b, s, d)


def kernel(x_prompt, x_sample, mem_prompt, mem_sample, norm_mix_w, w_in, lambda_q1, lambda_k1, lambda_q2,
           lambda_k2, attn_subln_w, conv_w, conv_b, a_log, dt_bias, d_skip, ssm_norm_w, w_attn_branch,
           w_ssm_branch, w_out, norm_cross_w, norm_mem_w, w_xq, w_xkv, w_xo, norm_ffn_w, w_router, b_router,
           w_gate, w_up, w_down, norm_final_w):
    p = dict(norm_mix_w=norm_mix_w, w_in=w_in, lambda_q1=lambda_q1, lambda_k1=lambda_k1, lambda_q2=lambda_q2,
             lambda_k2=lambda_k2, attn_subln_w=attn_subln_w, conv_w=conv_w, conv_b=conv_b, a_log=a_log,
             dt_bias=dt_bias, d_skip=d_skip, ssm_norm_w=ssm_norm_w, w_attn_branch=w_attn_branch,
             w_ssm_branch=w_ssm_branch, w_out=w_out, norm_cross_w=norm_cross_w, norm_mem_w=norm_mem_w,
             w_xq=w_xq, w_xkv=w_xkv, w_xo=w_xo, norm_ffn_w=norm_ffn_w, w_router=w_router, b_router=b_router,
             w_gate=w_gate, w_up=w_up, w_down=w_down)
    layers = [_prep_layer(l, p) for l in range(w_in.shape[0])]
    y_prompt = _trunk(x_prompt, mem_prompt, layers, norm_final_w)
    y_sample = _trunk(x_sample, mem_sample, layers, norm_final_w)
    return (y_prompt, y_sample)
```

```python
import functools
import math

import jax
import jax.numpy as jnp
from jax import lax
from jax.experimental import pallas as pl
from jax.experimental.pallas import tpu as pltpu

F32 = jnp.float32
BF16 = jnp.bfloat16
I32 = jnp.int32

D_MODEL = 1024
ATTN_HEADS = 8
ATTN_HEAD_DIM = 64
ATTN_V_DIM = 128
SSM_HEADS = 32
SSM_HEAD_DIM = 64
SSM_GROUPS = 4
D_STATE = 128
D_INNER = 2048
D_CONV = 5
XBC_DIM = 3072
CHUNK = 128
MEM_TOKENS = 256
X_HEADS = 4
X_HEAD_DIM = 128
X_WIDTH = 512
N_EXPERTS = 16
CAPACITY_FACTOR = 2
D_FF = 2816
EPS = 1e-6
SUBLN_EPS = 1e-5

C_Z = 0
C_GATE = 2048
C_XBC = 4096
C_Q = 7168
C_K = 8192
PROJ_COLS = 9216

LANE = 128
BF16_SUBLANE = 16
VMEM_LIMIT = 56 * 1024 * 1024


def _cparams(sem, vmem=None):
    return pltpu.CompilerParams(dimension_semantics=sem, vmem_limit_bytes=vmem)


def _rms(x, w, eps):
    return x * lax.rsqrt(jnp.mean(x * x, axis=-1, keepdims=True) + eps) * w


def _softplus(x):
    return jnp.maximum(x, 0.0) + jnp.log(1.0 + jnp.exp(-jnp.abs(x)))


def _silu(x):
    return x * (1.0 / (1.0 + jnp.exp2(x * (-1.0 / math.log(2.0)))))


def _norm_matmul_kernel(x_ref, nw_ref, w_ref, o_ref, xn_ref):
    @pl.when(pl.program_id(1) == 0)
    def _():
        xn_ref[...] = _rms(x_ref[...], nw_ref[...], EPS).astype(xn_ref.dtype)

    o_ref[...] = jnp.dot(xn_ref[...], w_ref[...], preferred_element_type=F32).astype(o_ref.dtype)


def norm_matmul(x, nw, w, out_dtype, tm, tn):
    t, d = x.shape
    n = w.shape[1]
    return pl.pallas_call(
        _norm_matmul_kernel,
        grid=(t // tm, n // tn),
        in_specs=[
            pl.BlockSpec((tm, d), lambda i, j: (i, 0)),
            pl.BlockSpec((1, d), lambda i, j: (0, 0)),
            pl.BlockSpec((d, tn), lambda i, j: (0, j)),
        ],
        out_specs=pl.BlockSpec((tm, tn), lambda i, j: (i, j)),
        out_shape=jax.ShapeDtypeStruct((t, n), out_dtype),
        scratch_shapes=[pltpu.VMEM((tm, d), BF16)],
        compiler_params=_cparams(("parallel", "arbitrary"), VMEM_LIMIT),
        name="norm_matmul",
    )(x, nw.reshape(1, d), w)


ONES_ROWS = BF16_SUBLANE
SKIP_MARGIN = 17.0
POS_SPLIT = 32


FIXED_REF_MAX_RANGE = 60.0


def _attn_kernel(scal_ref, q_ref, k_ref, vt_ref, w_ref, qf_ref, kf_ref, o_ref, qa_s, m_s, acc_s, kn_s, ref_s,
                 *, tq, tk, seq):
    h = pl.program_id(1)
    qi = pl.program_id(2)
    slope = scal_ref[h]
    nk = seq // tk
    ndiag = tq // tk
    t0 = qi * tq
    dlo = qi * ndiag
    half = ATTN_HEAD_DIM
    nt = (((1,), (1,)), ((), ()))

    lane_k = lax.broadcasted_iota(I32, (tk, LANE), 1)
    pick = jnp.where(lax.broadcasted_iota(I32, (8, LANE), 1) // half == lax.broadcasted_iota(I32, (8, LANE), 0),
                     1.0, 0.0).astype(BF16)

    @pl.when(qi == 0)
    def _():
        def body(i, carry):
            kc = k_ref[pl.ds(pl.multiple_of(i * tk, tk), tk), :].astype(F32)
            rows = lax.dot_general(pick, (kc * kc).astype(BF16), nt, preferred_element_type=F32)
            return jnp.maximum(carry, jnp.max(rows, axis=1, keepdims=True))

        norms = lax.fori_loop(0, nk, body, jnp.zeros((8, 1), F32))
        kn_s[0] = jnp.broadcast_to(norms[0:1, :], (8, LANE))
        kn_s[1] = jnp.broadcast_to(norms[1:2, :], (8, LANE))

    qb = q_ref[...] * (ATTN_HEAD_DIM ** -0.5)
    lane_q = lax.broadcasted_iota(I32, (tq, LANE), 1)
    q = qb.astype(F32)
    kd = k_ref[pl.ds(pl.multiple_of(t0, tq), tq), :].astype(F32)
    qn_rows = lax.dot_general(pick, (q * q).astype(BF16), nt, preferred_element_type=F32)
    self_rows = lax.dot_general(pick, (q * kd).astype(BF16), nt, preferred_element_type=F32)
    bounds = []
    for c in range(2):
        data = (lane_q < half) if c == 0 else (lane_q >= half)
        qa_s[c] = jnp.where(data, qb, qf_ref[c])
        ref_c = jnp.sqrt(qn_rows[c:c + 1, :] * kn_s[c][0:1, 0:1]) * 1.02
        ref_s[c] = ref_c
        bounds.append(jnp.max(ref_c * (1.0 + 2.0 ** -8) - self_rows[c:c + 1, :], axis=1, keepdims=True))

    span = jnp.maximum(bounds[0], bounds[1])
    count = jnp.minimum(span * 0.0 + scal_ref[ATTN_HEADS + 2 + h], math.log(seq))
    dskip = jnp.minimum((span + count + SKIP_MARGIN) / slope, 2.0 * seq)
    t0f = jnp.full((1, 1), t0, I32).astype(F32)
    dlof = jnp.full((1, 1), dlo, I32).astype(F32)
    klo_f = jnp.clip(jnp.floor((t0f + 1.0 - dskip) / tk), 0.0, dlof)
    khi_f = jnp.clip(jnp.ceil((dskip + t0f + (tq - 1.0)) / tk), dlof + ndiag, float(nk))
    klo = jnp.max(klo_f).astype(I32)
    khi = jnp.max(khi_f).astype(I32)
    fixed_ref = jnp.max(jnp.where(span <= FIXED_REF_MAX_RANGE, 1.0, 0.0)).astype(I32)

    m_s[...] = jnp.full(m_s.shape, -jnp.inf, F32)
    acc_s[...] = jnp.zeros(acc_s.shape, F32)
    ones = jnp.ones((ONES_ROWS, tk), BF16)

    def block(ki, side, fixed, d=0):
        s0 = pl.multiple_of(ki * tk, tk)
        kblk = k_ref[pl.ds(s0, tk), :]
        vaug = jnp.concatenate([vt_ref[:, pl.ds(s0, tk)], ones], axis=0)
        off = jnp.full((1, tq), t0 - s0, I32).astype(F32) * slope
        if side == 'D':
            right = ndiag > 1 and d == ndiag - 1
            c_lo, c_hi = (d * tk, tq) if right else (0, (d + 1) * tk)
            rel = (lax.broadcasted_iota(I32, (tk, c_hi - c_lo), 0) + (d * tk - c_lo)
                   - lax.broadcasted_iota(I32, (tk, c_hi - c_lo), 1)).astype(F32)
            corr = jnp.maximum(-rel if right else rel, 0.0) * (-2.0 * slope)
            feats = 'R' if right else 'L'
        else:
            feats = side
        cblk = -off if feats == 'R' else off
        for c in range(2):
            data = (lane_k < half) if c == 0 else (lane_k >= half)
            kaug = jnp.where(data, kblk, kf_ref[2 * c + (1 if feats == 'R' else 0)])
            st = lax.dot_general(kaug, qa_s[c], nt, preferred_element_type=F32)
            if side == 'D':
                pieces = [st[:, c_lo:c_hi] + corr]
                if c_lo > 0:
                    pieces.insert(0, st[:, :c_lo])
                if c_hi < tq:
                    pieces.append(st[:, c_hi:])
                st = jnp.concatenate(pieces, axis=1) if len(pieces) > 1 else pieces[0]
            if fixed:
                p = jnp.exp(st - (ref_s[c] + cblk)).astype(BF16)
                acc_s[c] += jnp.dot(vaug, p, preferred_element_type=F32)
            else:
                m_prev = m_s[c]
                m_new = jnp.maximum(m_prev, jnp.max(st, axis=0, keepdims=True) - cblk)
                p = jnp.exp(st - (m_new + cblk)).astype(BF16)
                alpha = jnp.exp(m_prev - m_new)
                acc_s[c] = acc_s[c] * alpha + jnp.dot(vaug, p, preferred_element_type=F32)
                m_s[c] = m_new

    def sweep(fixed):
        def run(side):
            def body(ki, carry):
                block(ki, side, fixed)
                return carry
            return body

        lax.fori_loop(klo, dlo, run('L'), 0)
        for d in range(ndiag):
            block(dlo + d, 'D', fixed, d)
        lax.fori_loop(dlo + ndiag, khi, run('R'), 0)

    @pl.when(fixed_ref == 1)
    def _():
        sweep(True)

    @pl.when(fixed_ref == 0)
    def _():
        sweep(False)

    lam = scal_ref[ATTN_HEADS]
    post = scal_ref[ATTN_HEADS + 1]
    a0 = acc_s[0]
    a1 = acc_s[1]
    o = a0[0:ATTN_V_DIM] / a0[ATTN_V_DIM:ATTN_V_DIM + 1] - lam * (a1[0:ATTN_V_DIM] / a1[ATTN_V_DIM:ATTN_V_DIM + 1])
    o = o * lax.rsqrt(jnp.mean(o * o, axis=0, keepdims=True) + SUBLN_EPS) * (w_ref[...] * post)
    o_ref[...] = o.T.astype(o_ref.dtype)


def _position_features(slopes, tq, tk):
    half = ATTN_HEAD_DIM
    sl = slopes.reshape(-1, 1, 1).astype(F32)
    lane = jnp.arange(LANE)[None, None, :]
    one = jnp.ones((1, 1, 1), F32)

    def tile(n, vals, base):
        out = jnp.zeros((slopes.shape[0], n, LANE), F32)
        for i, v in enumerate(vals):
            out = jnp.where(lane == base + i, v, out)
        return out

    def split(n):
        pos = jnp.arange(n, dtype=F32)[None, :, None]
        lo = jnp.mod(pos, float(POS_SPLIT))
        return pos - lo, lo

    qhi, qlo = split(tq)
    khi, klo = split(tk)
    qf, kf = [], []
    for c in range(2):
        base = half if c == 0 else 0
        qf.append(tile(tq, (-sl * qhi, -sl * qlo, one, one), base))
        left = tile(tk, (one, one, sl * khi, sl * klo), base)
        kf += [left, -left]
    return jnp.stack(qf, axis=1).astype(BF16), jnp.stack(kf, axis=1).astype(BF16)


def diff_attention(proj3, vt, scal, subln_w, tq, tk):
    b, s, _ = proj3.shape
    kq, kk = C_Q // LANE, C_K // LANE
    qfeat, kfeat = _position_features(scal[:ATTN_HEADS], tq, tk)
    return pl.pallas_call(
        functools.partial(_attn_kernel, tq=tq, tk=tk, seq=s),
        grid=(b, ATTN_HEADS, s // tq),
        in_specs=[
            pl.BlockSpec(memory_space=pltpu.SMEM),
            pl.BlockSpec((None, tq, LANE), lambda bi, h, qi: (bi, qi, kq + h)),
            pl.BlockSpec((None, s, LANE), lambda bi, h, qi: (bi, 0, kk + h)),
            pl.BlockSpec((ATTN_V_DIM, s), lambda bi, h, qi: (h, bi)),
            pl.BlockSpec((ATTN_V_DIM, 1), lambda bi, h, qi: (0, 0)),
            pl.BlockSpec((None, 2, tq, LANE), lambda bi, h, qi: (h, 0, 0, 0)),
            pl.BlockSpec((None, 4, tk, LANE), lambda bi, h, qi: (h, 0, 0, 0)),
        ],
        out_specs=pl.BlockSpec((None, tq, LANE), lambda bi, h, qi: (bi, qi, h)),
        out_shape=jax.ShapeDtypeStruct((b, s, ATTN_HEADS * ATTN_V_DIM), BF16),
        scratch_shapes=[
            pltpu.VMEM((2, tq, LANE), BF16),
            pltpu.VMEM((2, 1, tq), F32),
            pltpu.VMEM((2, ATTN_V_DIM + ONES_ROWS, tq), F32),
            pltpu.VMEM((2, 8, LANE), F32),
            pltpu.VMEM((2, 1, tq), F32),
        ],
        compiler_params=_cparams(("parallel", "parallel", "arbitrary"), VMEM_LIMIT),
        name="diff_attention",
    )(scal, proj3, proj3, vt, subln_w.reshape(ATTN_V_DIM, 1), qfeat, kfeat)


def _proj_t_kernel(x_ref, nw_ref, wvt_ref, wdt_ref, wdtt_ref, vt_ref, dt_ref, dtt_ref):
    xn = _rms(x_ref[...], nw_ref[...], EPS).astype(BF16)
    nt = (((1,), (1,)), ((), ()))
    vt_ref[...] = lax.dot_general(wvt_ref[...], xn, nt, preferred_element_type=F32).astype(vt_ref.dtype)
    dt_ref[...] = jnp.dot(xn, wdt_ref[...], preferred_element_type=F32)
    dtt_ref[...] = lax.dot_general(wdtt_ref[...], xn, nt, preferred_element_type=F32)


def proj_transposed(x, nw, wvt, wdt, wdtt, tm):
    t, d = x.shape
    nv = wvt.shape[0]
    const = lambda i: (0, 0)
    return pl.pallas_call(
        _proj_t_kernel,
        grid=(t // tm,),
        in_specs=[
            pl.BlockSpec((tm, d), lambda i: (i, 0)),
            pl.BlockSpec((1, d), const),
            pl.BlockSpec((nv, d), const),
            pl.BlockSpec((d, LANE), const),
            pl.BlockSpec((LANE, d), const),
        ],
        out_specs=[pl.BlockSpec((nv, tm), lambda i: (0, i)), pl.BlockSpec((tm, LANE), lambda i: (i, 0)),
                   pl.BlockSpec((LANE, tm), lambda i: (0, i))],
        out_shape=[jax.ShapeDtypeStruct((nv, t), BF16), jax.ShapeDtypeStruct((t, LANE), F32),
                   jax.ShapeDtypeStruct((LANE, t), F32)],
        compiler_params=_cparams(("parallel",), VMEM_LIMIT),
        name="proj_transposed",
    )(x, nw.reshape(1, d), wvt, wdt, wdtt)


HALO = 16


CONV_SUB = 128


def _conv_kernel(prev_ref, cur_ref, next_ref, w_ref, b_ref, o_ref, ext_ref, *, tm):
    i = pl.program_id(1)
    last = pl.num_programs(1) - 1
    zero = jnp.zeros(prev_ref.shape, BF16)
    ext_ref[pl.ds(0, HALO), :] = jnp.where(i > 0, prev_ref[...], zero)
    ext_ref[pl.ds(HALO, tm), :] = cur_ref[...]
    ext_ref[pl.ds(HALO + tm, HALO), :] = jnp.where(i < last, next_ref[...], zero)
    w = w_ref[...]
    bias = b_ref[...]
    taps = [j for j in range(D_CONV) if j != D_CONV // 2]
    win = CONV_SUB + 2 * HALO
    r = lax.broadcasted_iota(I32, (len(taps) * CONV_SUB, win), 0)
    c = lax.broadcasted_iota(I32, (len(taps) * CONV_SUB, win), 1)
    n = r >> (CONV_SUB.bit_length() - 1)
    shift = jnp.where(n < D_CONV // 2, n, n + 1) - D_CONV // 2
    pick = jnp.where(c == HALO + (r - n * CONV_SUB) + shift, 1.0, 0.0).astype(BF16)
    for sb in range(tm // CONV_SUB):
        xe = ext_ref[pl.ds(sb * CONV_SUB, win), :]
        moved = jnp.dot(pick, xe, preferred_element_type=F32)
        acc = bias + xe[HALO:HALO + CONV_SUB].astype(F32) * w[D_CONV // 2:D_CONV // 2 + 1, :]
        for k, j in enumerate(taps):
            acc = acc + moved[k * CONV_SUB:(k + 1) * CONV_SUB] * w[j:j + 1, :]
        o_ref[pl.ds(sb * CONV_SUB, CONV_SUB), :] = _silu(acc).astype(o_ref.dtype)


def conv_silu(proj3, conv_w, conv_b, tm, tc):
    b, s, _ = proj3.shape
    c0 = C_XBC // tc
    hb = tm // HALO
    nh = s // HALO
    return pl.pallas_call(
        functools.partial(_conv_kernel, tm=tm),
        grid=(b, s // tm, XBC_DIM // tc),
        in_specs=[
            pl.BlockSpec((None, HALO, tc), lambda bi, i, j: (bi, jnp.maximum(i * hb - 1, 0), c0 + j)),
            pl.BlockSpec((None, tm, tc), lambda bi, i, j: (bi, i, c0 + j)),
            pl.BlockSpec((None, HALO, tc), lambda bi, i, j: (bi, jnp.minimum((i + 1) * hb, nh - 1), c0 + j)),
            pl.BlockSpec((D_CONV, tc), lambda bi, i, j: (0, j)),
            pl.BlockSpec((1, tc), lambda bi, i, j: (0, j)),
        ],
        out_specs=pl.BlockSpec((None, tm, tc), lambda bi, i, j: (bi, i, j)),
        out_shape=jax.ShapeDtypeStruct((b, s, XBC_DIM), BF16),
        scratch_shapes=[pltpu.VMEM((tm + 2 * HALO, tc), BF16)],
        compiler_params=_cparams(("parallel", "parallel", "parallel"), VMEM_LIMIT),
        name="conv_silu",
    )(proj3, proj3, proj3, conv_w, conv_b.reshape(1, XBC_DIM))


def _expand_heads(a):
    hi = a.astype(BF16)
    lo = (a - hi.astype(F32)).astype(BF16)
    head_of = lax.broadcasted_iota(I32, (SSM_HEADS, D_INNER), 1) // SSM_HEAD_DIM
    e = jnp.where(head_of == lax.broadcasted_iota(I32, (SSM_HEADS, D_INNER), 0), 1.0, 0.0).astype(BF16)
    return jnp.dot(hi, e, preferred_element_type=F32) + jnp.dot(lo, e, preferred_element_type=F32)


def _tri(n, fn):
    r = lax.broadcasted_iota(I32, (n, n), 0)
    c = lax.broadcasted_iota(I32, (n, n), 1)
    return jnp.where(fn(r, c), 1.0, 0.0).astype(F32)


def _dot_hi(a, b):
    return jnp.dot(a, b, preferred_element_type=F32, precision=lax.Precision.HIGHEST)


def _ssd_state_kernel(xf_ref, bf_ref, dtf_ref, xb_ref, bb_ref, dtb_ref, par_ref, hf_ref, hb_ref, sf_s, sb_s):
    j = pl.program_id(1)

    @pl.when(j == 0)
    def _():
        sf_s[...] = jnp.zeros(sf_s.shape, F32)
        sb_s[...] = jnp.zeros(sb_s.shape, F32)

    hf_ref[...] = sf_s[...].astype(hf_ref.dtype)
    hb_ref[...] = sb_s[...].astype(hb_ref.dtype)

    par = par_ref[...]
    lower_incl = _tri(CHUNK, lambda r, c: c <= r)
    lower_strict = _tri(CHUNK, lambda r, c: c < r)

    def one_direction(x_ref, b_ref, dt_ref, s_ref, col0, backward):
        raw = dt_ref[...][:, col0:col0 + SSM_HEADS]
        dt = _softplus(raw + par[0:1, col0:col0 + SSM_HEADS])
        a = dt * par[1:2, col0:col0 + SSM_HEADS]
        if backward:
            excl = _dot_hi(lower_strict, a)
            wgt = dt * jnp.exp(excl)
            total = excl[CHUNK - 1:CHUNK, :] + a[CHUNK - 1:CHUNK, :]
        else:
            cum = _dot_hi(lower_incl, a)
            total = cum[CHUNK - 1:CHUNK, :]
            wgt = dt * jnp.exp(total - cum)
        xw = (x_ref[...].astype(F32) * _expand_heads(wgt)).astype(BF16)
        dec = _expand_heads(jnp.exp(jnp.broadcast_to(total, (8, SSM_HEADS))))[0:1, :]
        bmat = b_ref[...]
        gw = D_INNER // SSM_GROUPS
        for g in range(SSM_GROUPS):
            contrib = lax.dot_general(bmat[:, g * D_STATE:(g + 1) * D_STATE], xw[:, g * gw:(g + 1) * gw],
                                      (((0,), (0,)), ((), ())), preferred_element_type=F32)
            s_ref[g] = s_ref[g] * dec[:, g * gw:(g + 1) * gw] + contrib

    one_direction(xf_ref, bf_ref, dtf_ref, sf_s, 0, False)
    one_direction(xb_ref, bb_ref, dtb_ref, sb_s, SSM_HEADS, True)


def ssd_states(xbc, dt3, par):
    b, s, _ = xbc.shape
    nc = s // CHUNK
    gw = D_INNER // SSM_GROUPS
    xblk = D_INNER // D_INNER
    bcol = D_INNER // (SSM_GROUPS * D_STATE)
    hshape = jax.ShapeDtypeStruct((b, nc, SSM_GROUPS, D_STATE, gw), BF16)
    hspec_f = pl.BlockSpec((None, None, SSM_GROUPS, D_STATE, gw), lambda bi, j: (bi, j, 0, 0, 0))
    hspec_b = pl.BlockSpec((None, None, SSM_GROUPS, D_STATE, gw), lambda bi, j: (bi, nc - 1 - j, 0, 0, 0))
    del xblk
    return pl.pallas_call(
        _ssd_state_kernel,
        grid=(b, nc),
        in_specs=[
            pl.BlockSpec((None, CHUNK, D_INNER), lambda bi, j: (bi, j, 0)),
            pl.BlockSpec((None, CHUNK, SSM_GROUPS * D_STATE), lambda bi, j: (bi, j, bcol)),
            pl.BlockSpec((None, CHUNK, LANE), lambda bi, j: (bi, j, 0)),
            pl.BlockSpec((None, CHUNK, D_INNER), lambda bi, j: (bi, nc - 1 - j, 0)),
            pl.BlockSpec((None, CHUNK, SSM_GROUPS * D_STATE), lambda bi, j: (bi, nc - 1 - j, bcol)),
            pl.BlockSpec((None, CHUNK, LANE), lambda bi, j: (bi, nc - 1 - j, 0)),
            pl.BlockSpec((8, LANE), lambda bi, j: (0, 0)),
        ],
        out_specs=[hspec_f, hspec_b],
        out_shape=[hshape, hshape],
        scratch_shapes=[pltpu.VMEM((SSM_GROUPS, D_STATE, gw), F32), pltpu.VMEM((SSM_GROUPS, D_STATE, gw), F32)],
        compiler_params=_cparams(("parallel", "arbitrary"), VMEM_LIMIT),
        name="ssd_states",
    )(xbc, xbc, dt3, xbc, xbc, dt3, par)


def _ssd_out_kernel(x_ref, b_ref, c_ref, z_ref, dt_ref, dtt_ref, par_ref, part_ref, dsk_ref, nw_ref,
                    hf_ref, hb_ref, o_ref):
    par = par_ref[...]
    part = part_ref[...]
    lower_incl = _tri(CHUNK, lambda r, c: c <= r)
    lower_strict = _tri(CHUNK, lambda r, c: c < r)
    upper_incl = _tri(CHUNK, lambda r, c: r <= c)
    upper_strict = _tri(CHUNK, lambda r, c: r < c)

    dtc = _softplus(dt_ref[...][:, 0:2 * SSM_HEADS] + par[0:1, 0:2 * SSM_HEADS])
    ac = dtc * par[1:2, 0:2 * SSM_HEADS]
    cum_c = _dot_hi(lower_incl, ac)
    excl_c = _dot_hi(lower_strict, ac)
    dtr = _softplus(dtt_ref[...][0:2 * SSM_HEADS, :] + part[0:2 * SSM_HEADS, 0:1])
    ar = dtr * part[0:2 * SSM_HEADS, 1:2]
    cum_r = _dot_hi(ar, upper_incl)
    excl_r = _dot_hi(ar, upper_strict)

    cumf_c = cum_c[:, 0:SSM_HEADS]
    exclb_c = excl_c[:, SSM_HEADS:2 * SSM_HEADS]
    totb = cum_c[CHUNK - 1:CHUNK, SSM_HEADS:2 * SSM_HEADS]
    dtf_r = dtr[0:SSM_HEADS, :]
    dtb_r = dtr[SSM_HEADS:2 * SSM_HEADS, :]
    fwd_r = cum_r[0:SSM_HEADS, :] - jnp.log(dtf_r)
    bwd_r = excl_r[SSM_HEADS:2 * SSM_HEADS, :] + jnp.log(dtb_r)
    diag_r = jnp.log(dtf_r + dtb_r)

    row = lax.broadcasted_iota(I32, (CHUNK, CHUNK), 0)
    col = lax.broadcasted_iota(I32, (CHUNK, CHUNK), 1)
    low = col < row
    diag = col == row
    lane = lax.broadcasted_iota(I32, (CHUNK, LANE), 1)
    first = lane < SSM_HEAD_DIM

    gw = D_INNER // SSM_GROUPS
    hpg = SSM_HEADS // SSM_GROUPS
    scale_f = _expand_heads(jnp.exp(cumf_c))
    scale_b = _expand_heads(jnp.exp(totb - exclb_c))
    for g in range(SSM_GROUPS):
        cg = c_ref[:, g * D_STATE:(g + 1) * D_STATE]
        bg = b_ref[:, g * D_STATE:(g + 1) * D_STATE]
        cb = lax.dot_general(cg, bg, (((1,), (1,)), ((), ())), preferred_element_type=F32)
        inter_f = jnp.dot(cg, hf_ref[g], preferred_element_type=F32)
        inter_b = jnp.dot(cg, hb_ref[g], preferred_element_type=F32)
        parts = []
        for pp in range(hpg // 2):
            gms = []
            for h in (g * hpg + 2 * pp, g * hpg + 2 * pp + 1):
                arg = jnp.where(low, cumf_c[:, h:h + 1] - fwd_r[h:h + 1, :],
                                bwd_r[h:h + 1, :] - exclb_c[:, h:h + 1])
                arg = jnp.where(diag, diag_r[h:h + 1, :], arg)
                gms.append((cb * jnp.exp(arg)).astype(BF16))
            ls = slice(g * gw + pp * LANE, g * gw + (pp + 1) * LANE)
            il = slice(pp * LANE, (pp + 1) * LANE)
            xp = x_ref[:, ls]
            rhs = jnp.concatenate([jnp.where(first, xp, 0), jnp.where(first, 0, xp)], axis=0)
            yp = jnp.dot(jnp.concatenate(gms, axis=1), rhs, preferred_element_type=F32)
            yp = (yp + inter_f[:, il] * scale_f[:, ls] + inter_b[:, il] * scale_b[:, ls]
                  + xp.astype(F32) * dsk_ref[:, ls])
            parts.append(yp * _silu(z_ref[:, ls].astype(F32)))
        yg = jnp.concatenate(parts, axis=1)
        sl = slice(g * gw, (g + 1) * gw)
        o_ref[:, sl] = (yg * lax.rsqrt(jnp.mean(yg * yg, axis=-1, keepdims=True) + EPS)
                        * nw_ref[:, sl]).astype(o_ref.dtype)


def ssd_output(xbc, proj3, dt3, dtt3, par, part, dskip_x, nw, hf, hb):
    b, s, _ = xbc.shape
    nc = s // CHUNK
    gw = D_INNER // SSM_GROUPS
    gs = SSM_GROUPS * D_STATE
    hspec = pl.BlockSpec((None, None, SSM_GROUPS, D_STATE, gw), lambda bi, j: (bi, j, 0, 0, 0))
    return pl.pallas_call(
        _ssd_out_kernel,
        grid=(b, nc),
        in_specs=[
            pl.BlockSpec((None, CHUNK, D_INNER), lambda bi, j: (bi, j, 0)),
            pl.BlockSpec((None, CHUNK, gs), lambda bi, j: (bi, j, D_INNER // gs)),
            pl.BlockSpec((None, CHUNK, gs), lambda bi, j: (bi, j, D_INNER // gs + 1)),
            pl.BlockSpec((None, CHUNK, D_INNER), lambda bi, j: (bi, j, C_Z // D_INNER)),
            pl.BlockSpec((None, CHUNK, LANE), lambda bi, j: (bi, j, 0)),
            pl.BlockSpec((LANE, CHUNK), lambda bi, j: (0, bi * nc + j)),
            pl.BlockSpec((8, LANE), lambda bi, j: (0, 0)),
            pl.BlockSpec((LANE, 8), lambda bi, j: (0, 0)),
            pl.BlockSpec((1, D_INNER), lambda bi, j: (0, 0)),
            pl.BlockSpec((1, D_INNER), lambda bi, j: (0, 0)),
            hspec, hspec,
        ],
        out_specs=pl.BlockSpec((None, CHUNK, D_INNER), lambda bi, j: (bi, j, 0)),
        out_shape=jax.ShapeDtypeStruct((b, s, D_INNER), BF16),
        compiler_params=_cparams(("parallel", "parallel"), VMEM_LIMIT),
        name="ssd_output",
    )(xbc, xbc, xbc, proj3, dt3, dtt3, par, part, dskip_x, nw, hf, hb)


def _merge_kernel(x_ref, a_ref, s_ref, g0_ref, g1_ref, wa_ref, ws_ref, wo_ref, o_ref):
    ya = jnp.dot(a_ref[...], wa_ref[...], preferred_element_type=F32)
    ys = jnp.dot(s_ref[...], ws_ref[...], preferred_element_type=F32)
    g0 = 1.0 / (1.0 + jnp.exp(-g0_ref[...].astype(F32)))
    g1 = 1.0 / (1.0 + jnp.exp(-g1_ref[...].astype(F32)))
    merged = (g0 * ya + g1 * ys).astype(BF16)
    o_ref[...] = x_ref[...] + jnp.dot(merged, wo_ref[...], preferred_element_type=F32)


def merge_out(x, attn, ssm, proj, wa, ws, wo, tm):
    t, d = x.shape
    gb = C_GATE // d
    const = lambda i: (0, 0)
    return pl.pallas_call(
        _merge_kernel,
        grid=(t // tm,),
        in_specs=[
            pl.BlockSpec((tm, d), lambda i: (i, 0)),
            pl.BlockSpec((tm, d), lambda i: (i, 0)),
            pl.BlockSpec((tm, D_INNER), lambda i: (i, 0)),
            pl.BlockSpec((tm, d), lambda i: (i, gb)),
            pl.BlockSpec((tm, d), lambda i: (i, gb + 1)),
            pl.BlockSpec((d, d), const),
            pl.BlockSpec((D_INNER, d), const),
            pl.BlockSpec((d, d), const),
        ],
        out_specs=pl.BlockSpec((tm, d), lambda i: (i, 0)),
        out_shape=jax.ShapeDtypeStruct((t, d), F32),
        compiler_params=_cparams(("parallel",), VMEM_LIMIT),
        name="merge_out",
    )(x, attn, ssm, proj, proj, wa, ws, wo)


def _cross_kernel(x_ref, nw_ref, wq_ref, kv_ref, wo_ref, o_ref):
    x = x_ref[...]
    h = _rms(x, nw_ref[...], EPS).astype(BF16)
    q = jnp.dot(h, wq_ref[...], preferred_element_type=F32).astype(BF16)
    kv = kv_ref[...]
    outs = []
    for hd in range(X_HEADS):
        qh = q[:, hd * X_HEAD_DIM:(hd + 1) * X_HEAD_DIM]
        kh = kv[:, hd * X_HEAD_DIM:(hd + 1) * X_HEAD_DIM]
        vh = kv[:, X_WIDTH + hd * X_HEAD_DIM:X_WIDTH + (hd + 1) * X_HEAD_DIM]
        s = lax.dot_general(qh, kh, (((1,), (1,)), ((), ())), preferred_element_type=F32) * (X_HEAD_DIM ** -0.5)
        s = s - jnp.max(s, axis=-1, keepdims=True)
        p = jnp.exp(s)
        p = (p / jnp.sum(p, axis=-1, keepdims=True)).astype(BF16)
        outs.append(jnp.dot(p, vh, preferred_element_type=F32).astype(BF16))
    o = jnp.concatenate(outs, axis=1)
    o_ref[...] = x + jnp.dot(o, wo_ref[...], preferred_element_type=F32)


def cross_attention(x, nw, wq, kv, wo, seq, tm):
    t, d = x.shape
    per_seq = seq // tm
    const = lambda i: (0, 0)
    return pl.pallas_call(
        _cross_kernel,
        grid=(t // tm,),
        in_specs=[
            pl.BlockSpec((tm, d), lambda i: (i, 0)),
            pl.BlockSpec((1, d), const),
            pl.BlockSpec((d, X_WIDTH), const),
            pl.BlockSpec((None, MEM_TOKENS, 2 * X_WIDTH), lambda i: (i // per_seq, 0, 0)),
            pl.BlockSpec((X_WIDTH, d), const),
        ],
        out_specs=pl.BlockSpec((tm, d), lambda i: (i, 0)),
        out_shape=jax.ShapeDtypeStruct((t, d), F32),
        compiler_params=_cparams(("parallel",), VMEM_LIMIT),
        name="cross_attention",
    )(x, nw.reshape(1, d), wq, kv, wo)


def _router_kernel(x_ref, nw_ref, wrt_ref, br_ref, hb_ref, aff_ref):
    h = _rms(x_ref[...], nw_ref[...], EPS)
    hb_ref[...] = h.astype(hb_ref.dtype)
    logits = lax.dot_general(wrt_ref[...], h, (((1,), (1,)), ((), ())), preferred_element_type=F32,
                             precision=lax.Precision.HIGHEST) + br_ref[...]
    e = jnp.exp(logits - jnp.max(logits, axis=0, keepdims=True))
    aff_ref[...] = e / jnp.sum(e, axis=0, keepdims=True)


def router(x, nw, wrt, br, tm):
    t, d = x.shape
    return pl.pallas_call(
        _router_kernel,
        grid=(t // tm,),
        in_specs=[
            pl.BlockSpec((tm, d), lambda i: (i, 0)),
            pl.BlockSpec((1, d), lambda i: (0, 0)),
            pl.BlockSpec((N_EXPERTS, d), lambda i: (0, 0)),
            pl.BlockSpec((N_EXPERTS, 1), lambda i: (0, 0)),
        ],
        out_specs=[pl.BlockSpec((tm, d), lambda i: (i, 0)), pl.BlockSpec((N_EXPERTS, tm), lambda i: (0, i))],
        out_shape=[jax.ShapeDtypeStruct((t, d), BF16), jax.ShapeDtypeStruct((N_EXPERTS, t), F32)],
        compiler_params=_cparams(("parallel",), VMEM_LIMIT),
        name="router",
    )(x, nw.reshape(1, d), wrt, br.reshape(N_EXPERTS, 1))


TB = 256


def _topk_kernel(aff_ref, pos_ref, gate_ref, roff_ref, *, cap):
    a = aff_ref[...]
    nr = a.shape[1]
    bits = pltpu.bitcast(a, I32)

    def count(mask):
        c = jnp.sum(jnp.where(mask, 1.0, 0.0), axis=2, keepdims=True)
        return jnp.sum(c, axis=1, keepdims=True)

    def body(i, thr):
        cand = thr | jnp.left_shift(jnp.int32(1), 30 - i)
        return jnp.where(count(bits >= cand) >= float(cap), cand, thr)

    thr = lax.fori_loop(0, 31, body, jnp.zeros((N_EXPERTS, 1, 1), I32))
    gt = bits > thr
    eq = bits == thr
    need = float(cap) - count(gt)

    ustrict = _tri(TB, lambda r, c: r < c).astype(BF16)
    ones = jnp.ones((TB, TB), BF16)
    lstrict = _tri(nr, lambda r, c: c < r).astype(BF16)

    def excl_prefix(m):
        mb = m.astype(BF16)
        within = jnp.dot(mb, ustrict, preferred_element_type=F32)
        rowsum = jnp.dot(mb, ones, preferred_element_type=F32)
        rowoff = jnp.dot(lstrict, rowsum.astype(BF16), preferred_element_type=F32)
        return within + rowoff, rowoff

    for e in range(N_EXPERTS):
        eq_e = jnp.where(eq[e], 1.0, 0.0)
        pe, _ = excl_prefix(eq_e)
        keep = jnp.where(pe < need[e], eq_e, 0.0)
        sel = jnp.where(gt[e], 1.0, keep)
        ps, roff = excl_prefix(sel)
        chosen = sel > 0.5
        pos_ref[e] = jnp.where(chosen, ps.astype(I32), -1)
        gate_ref[e] = jnp.where(chosen, a[e], 0.0)
        roff_ref[e] = roff.astype(I32)


def topk_select(aff3, cap):
    e, r, tb = aff3.shape
    full = lambda: (0, 0, 0)
    return pl.pallas_call(
        functools.partial(_topk_kernel, cap=cap),
        grid=(),
        in_specs=[pl.BlockSpec((e, r, tb), full)],
        out_specs=[pl.BlockSpec((e, r, tb), full)] * 3,
        out_shape=[jax.ShapeDtypeStruct((e, r, tb), I32), jax.ShapeDtypeStruct((e, r, tb), F32),
                   jax.ShapeDtypeStruct((e, r, tb), I32)],
        compiler_params=pltpu.CompilerParams(vmem_limit_bytes=VMEM_LIMIT),
        name="topk_select",
    )(aff3)


GATHER_BUFS = 8


GATHER_PARTS = 2


def _ffn_kernel(lo_ref, hi_ref, roff_ref, rend_ref, pos_ref, gate_ref, hb_hbm, wg_ref, wu_ref, wd_ref, y_ref,
                xbuf, sem, xc_s, g_s, *, ts, nj):
    e = pl.program_id(0)
    j = pl.program_id(1)
    nr = pos_ref.shape[0]
    step = e * nj + j
    nsteps = pl.num_programs(0) * nj
    lo = lo_ref[step]
    hi = hi_ref[step]
    xc_s[...] = jnp.zeros(xc_s.shape, F32)
    g_s[...] = jnp.zeros(g_s.shape, F32)

    def fetch(r, slot):
        return pltpu.make_async_copy(hb_hbm.at[pl.ds(r * TB, TB)], xbuf.at[slot], sem.at[slot])

    def prime(first, end):
        for i in range(GATHER_BUFS - 1):
            @pl.when(first + i < end)
            def _(i=i):
                fetch(first + i, i).start()

    @pl.when(step == 0)
    def _():
        prime(lo, hi)

    part = ts // GATHER_PARTS
    slot_ids = lax.broadcasted_iota(I32, (part, TB), 0)

    def body(r, carry):
        idx = r - lo
        slot = idx & (GATHER_BUFS - 1)
        fetch(r, slot).wait()

        @pl.when(r + (GATHER_BUFS - 1) < hi)
        def _():
            fetch(r + (GATHER_BUFS - 1), (idx + (GATHER_BUFS - 1)) & (GATHER_BUFS - 1)).start()

        prow = pos_ref[pl.ds(r, 1), :]
        grow = gate_ref[pl.ds(r, 1), :]
        first = roff_ref[e * nr + r]
        last = rend_ref[e * nr + r]
        for p in range(GATHER_PARTS):
            base = j * ts + p * part

            @pl.when(jnp.logical_and(first < base + part, last > base))
            def _(p=p, base=base):
                hit = prow == slot_ids + base
                sel = jnp.where(hit, 1.0, 0.0).astype(BF16)
                rows = slice(p * part, (p + 1) * part)
                xc_s[rows, :] += jnp.dot(sel, xbuf[slot], preferred_element_type=F32)
                g_s[rows, :] += jnp.sum(jnp.where(hit, grow, 0.0), axis=1, keepdims=True)
        return carry

    lax.fori_loop(lo, hi, body, 0)

    @pl.when(step + 1 < nsteps)
    def _():
        prime(lo_ref[step + 1], hi_ref[step + 1])

    xc = xc_s[...].astype(BF16)
    hg = jnp.dot(xc, wg_ref[...], preferred_element_type=F32)
    hu = jnp.dot(xc, wu_ref[...], preferred_element_type=F32)
    hid = (_silu(hg) * hu).astype(BF16)
    y = jnp.dot(hid, wd_ref[...], preferred_element_type=F32) * g_s[...]
    y_ref[...] = y.astype(y_ref.dtype)


def expert_ffn(lo, hi, roff, rend, pos, gate, hb, wg, wu, wd, cap, ts):
    e, r, tb = pos.shape
    d = hb.shape[1]
    nj = cap // ts
    grid_spec = pltpu.PrefetchScalarGridSpec(
        num_scalar_prefetch=4,
        grid=(e, nj),
        in_specs=[
            pl.BlockSpec((None, r, tb), lambda ei, j, *_: (ei, 0, 0)),
            pl.BlockSpec((None, r, tb), lambda ei, j, *_: (ei, 0, 0)),
            pl.BlockSpec(memory_space=pl.ANY),
            pl.BlockSpec((None, d, D_FF), lambda ei, j, *_: (ei, 0, 0)),
            pl.BlockSpec((None, d, D_FF), lambda ei, j, *_: (ei, 0, 0)),
            pl.BlockSpec((None, D_FF, d), lambda ei, j, *_: (ei, 0, 0)),
        ],
        out_specs=pl.BlockSpec((None, ts, d), lambda ei, j, *_: (ei, j, 0)),
        scratch_shapes=[
            pltpu.VMEM((GATHER_BUFS, TB, d), BF16),
            pltpu.SemaphoreType.DMA((GATHER_BUFS,)),
            pltpu.VMEM((ts, d), F32),
            pltpu.VMEM((ts, 1), F32),
        ],
    )
    return pl.pallas_call(
        functools.partial(_ffn_kernel, ts=ts, nj=nj),
        grid_spec=grid_spec,
        out_shape=jax.ShapeDtypeStruct((e, cap, d), BF16),
        compiler_params=_cparams(("arbitrary", "arbitrary"), VMEM_LIMIT),
        name="expert_ffn",
    )(lo, hi, roff.reshape(-1), rend.reshape(-1), pos, gate, hb, wg, wu, wd)


WIN_MAIN = 128
WIN_OVER = TB + BF16_SUBLANE - WIN_MAIN
MXU_DEPTH = 256


def _combine_kernel(start_ref, ostart_ref, ovf_ref, x_ref, post_ref, nw_ref, *rest, final):
    y_main = rest[:N_EXPERTS]
    y_over = rest[N_EXPERTS:2 * N_EXPERTS]
    o_ref = rest[2 * N_EXPERTS]
    r = pl.program_id(0)
    nr = pl.num_programs(0)
    post = post_ref[...]
    rels = [post[:, e:e + 1] - start_ref[e * nr + r] for e in range(N_EXPERTS)]
    per = MXU_DEPTH // WIN_MAIN
    lane = lax.broadcasted_iota(I32, (TB, MXU_DEPTH), 1)
    acc = x_ref[...]
    for e0 in range(0, N_EXPERTS, per):
        want = rels[e0 + per - 1] + (per - 1) * WIN_MAIN
        for k in range(per - 2, -1, -1):
            want = jnp.where(lane < (k + 1) * WIN_MAIN, rels[e0 + k] + k * WIN_MAIN, want)
        sel = jnp.where(want == lane, 1.0, 0.0).astype(BF16)
        rows = jnp.concatenate([y_main[e0 + k][...] for k in range(per)], axis=0)
        acc = acc + jnp.dot(sel, rows, preferred_element_type=F32)
    o_ref[...] = acc

    @pl.when(ovf_ref[r] != 0)
    def _():
        lane_o = lax.broadcasted_iota(I32, (TB, WIN_OVER), 1) + WIN_MAIN
        more = o_ref[...]
        for e in range(N_EXPERTS):
            sel = jnp.where(rels[e] == lane_o, 1.0, 0.0).astype(BF16)
            more = more + jnp.dot(sel, y_over[e][...], preferred_element_type=F32)
        o_ref[...] = more

    if final:
        o_ref[...] = _rms(o_ref[...], nw_ref[...], EPS)


def combine(x, post, start, ostart, ovf, y, norm_w, final):
    t, d = x.shape
    e, cap, _ = y.shape
    nr = t // TB

    def y_spec(ei, rows, which):
        return pl.BlockSpec((pl.Element(rows), pl.Element(d)),
                            lambda r, st, ost, ov: (pl.multiple_of(ei * cap + (st, ost)[which][ei * nr + r],
                                                                   BF16_SUBLANE), 0))

    grid_spec = pltpu.PrefetchScalarGridSpec(
        num_scalar_prefetch=3,
        grid=(nr,),
        in_specs=([pl.BlockSpec((TB, d), lambda r, st, ost, ov: (r, 0)),
                   pl.BlockSpec((TB, e), lambda r, st, ost, ov: (r, 0)),
                   pl.BlockSpec((1, d), lambda r, st, ost, ov: (0, 0))]
                  + [y_spec(ei, WIN_MAIN, 0) for ei in range(e)]
                  + [y_spec(ei, WIN_OVER, 1) for ei in range(e)]),
        out_specs=pl.BlockSpec((TB, d), lambda r, st, ost, ov: (r, 0)),
    )
    y2 = y.reshape(e * cap, d)
    return pl.pallas_call(
        functools.partial(_combine_kernel, final=final),
        grid_spec=grid_spec,
        out_shape=jax.ShapeDtypeStruct((t, d), F32),
        compiler_params=_cparams(("arbitrary",), VMEM_LIMIT),
        name="combine",
    )(start, ostart, ovf, x, post, norm_w.reshape(1, d), *([y2] * (2 * e)))


def _tiles(t, s):
    def fit(n, pref):
        while n % pref:
            pref //= 2
        return pref

    cap = max(1, CAPACITY_FACTOR * t // N_EXPERTS)
    return dict(
        tm_proj=fit(t, 1024), tn_proj=PROJ_COLS // 4,
        tq=fit(s, 1024), tk=fit(s, 512),
        tm_conv=fit(s, 512), tc_conv=1024,
        tm_tok=fit(s, 512),
        cap=cap, ts=fit(cap, 256),
    )


def _prep_layer(l, p):
    w_in = p['w_in'][l]
    o_q, o_k, o_v, o_z, o_xbc, o_dt = 1024, 2048, 3072, 5120, 8192, 8256
    w_main = jnp.concatenate([w_in[:, o_v:o_z], w_in[:, o_dt:], w_in[:, o_z:o_xbc], w_in[:, :o_k]],
                             axis=1).astype(BF16)
    w_vt = w_in[:, o_k:o_v].T.astype(BF16)
    w_dt = jnp.pad(w_in[:, o_xbc:o_dt], ((0, 0), (0, LANE - 2 * SSM_HEADS))).astype(BF16)
    dt_bias = p['dt_bias'][l].reshape(-1).astype(F32)
    a_neg = -jnp.exp(p['a_log'][l].astype(F32)).reshape(-1)
    par = jnp.zeros((8, LANE), F32).at[0, :2 * SSM_HEADS].set(dt_bias).at[1, :2 * SSM_HEADS].set(a_neg)
    part = jnp.zeros((LANE, 8), F32).at[:2 * SSM_HEADS, 0].set(dt_bias).at[:2 * SSM_HEADS, 1].set(a_neg)
    lam_init = 0.8 - 0.6 * math.exp(-0.3 * l)
    lam = (jnp.exp(jnp.sum(p['lambda_q1'][l].astype(F32) * p['lambda_k1'][l].astype(F32)))
           - jnp.exp(jnp.sum(p['lambda_q2'][l].astype(F32) * p['lambda_k2'][l].astype(F32))) + lam_init)
    slopes = jnp.exp2(-8.0 * (jnp.arange(ATTN_HEADS, dtype=F32) + 1.0) / ATTN_HEADS)
    tail = jnp.log(2.0 / (1.0 - jnp.exp(-slopes)))
    scal = jnp.concatenate([slopes, lam.reshape(1), jnp.full((1,), 1.0 - lam_init, F32), tail]).astype(F32)
    return dict(
        norm_mix=p['norm_mix_w'][l], w_main=w_main, w_vt=w_vt, w_dt=w_dt, w_dtt=w_dt.T, par=par, part=part,
        scal=scal,
        subln=p['attn_subln_w'][l].astype(F32),
        conv_w=p['conv_w'][l].astype(F32), conv_b=p['conv_b'][l].astype(F32),
        dskip_x=jnp.repeat(p['d_skip'][l].astype(F32), SSM_HEAD_DIM).reshape(1, D_INNER),
        ssm_nw=p['ssm_norm_w'][l].astype(F32).reshape(1, D_INNER),
        wa=p['w_attn_branch'][l].astype(BF16), ws=p['w_ssm_branch'][l].astype(BF16),
        wo=p['w_out'][l].astype(BF16),
        norm_cross=p['norm_cross_w'][l], norm_mem=p['norm_mem_w'][l],
        wxq=p['w_xq'][l].astype(BF16), wxkv=p['w_xkv'][l].astype(BF16), wxo=p['w_xo'][l].astype(BF16),
        norm_ffn=p['norm_ffn_w'][l], wrt=p['w_router'][l].astype(F32).T, br=p['b_router'][l].astype(F32),
        wg=p['w_gate'][l].astype(BF16), wu=p['w_up'][l].astype(BF16), wd=p['w_down'][l].astype(BF16),
    )


def _layer(x, mem, lw, b, s, norm_final_w, last):
    t = b * s
    tl = _tiles(t, s)
    proj = norm_matmul(x, lw['norm_mix'], lw['w_main'], BF16, tl['tm_proj'], tl['tn_proj'])
    vt, dt, dtt = proj_transposed(x, lw['norm_mix'], lw['w_vt'], lw['w_dt'], lw['w_dtt'], tl['tm_tok'])
    proj3 = proj.reshape(b, s, PROJ_COLS)
    dt3 = dt.reshape(b, s, LANE)
    attn = diff_attention(proj3, vt, lw['scal'], lw['subln'], tl['tq'], tl['tk'])
    xbc = conv_silu(proj3, lw['conv_w'], lw['conv_b'], tl['tm_conv'], tl['tc_conv'])
    hf, hb = ssd_states(xbc, dt3, lw['par'])
    ssm = ssd_output(xbc, proj3, dt3, dtt, lw['par'], lw['part'], lw['dskip_x'], lw['ssm_nw'], hf, hb)
    x = merge_out(x, attn.reshape(t, D_MODEL), ssm.reshape(t, D_INNER), proj, lw['wa'], lw['ws'], lw['wo'],
                  tl['tm_tok'])
    nm = mem.shape[0] * mem.shape[1]
    kv = norm_matmul(mem.reshape(nm, D_MODEL), lw['norm_mem'], lw['wxkv'], BF16, min(nm, 512), 2 * X_WIDTH)
    x = cross_attention(x, lw['norm_cross'], lw['wxq'], kv.reshape(b, MEM_TOKENS, 2 * X_WIDTH), lw['wxo'], s,
                        tl['tm_tok'])
    cap, ts = tl['cap'], tl['ts']
    nr = t // TB
    hbf, aff = router(x, lw['norm_ffn'], lw['wrt'], lw['br'], tl['tm_tok'])
    pos, gate, roff = topk_select(aff.reshape(N_EXPERTS, nr, TB), cap)
    roff = roff[:, :, 0]
    rend = jnp.concatenate([roff[:, 1:], jnp.full((N_EXPERTS, 1), cap, I32)], axis=1)
    edges = jnp.arange(cap // ts, dtype=I32) * ts
    lo = jnp.sum(rend[:, None, :] <= edges[None, :, None], axis=2).astype(I32).reshape(-1)
    hi = jnp.sum(roff[:, None, :] < (edges + ts)[None, :, None], axis=2).astype(I32).reshape(-1)
    y = expert_ffn(lo, hi, roff, rend, pos, gate, hbf, lw['wg'], lw['wu'], lw['wd'], cap, ts)
    start = jnp.minimum((roff // BF16_SUBLANE) * BF16_SUBLANE, cap - (WIN_MAIN + WIN_OVER)).astype(I32)
    ovf = jnp.any(rend - start > WIN_MAIN, axis=0).astype(I32)
    ostart = jnp.where(ovf[None, :] != 0, start + WIN_MAIN, 0).astype(I32)
    post = jnp.transpose(pos.reshape(N_EXPERTS, t))
    return combine(x, post, start.reshape(-1), ostart.reshape(-1), ovf, y, norm_final_w, last)


def _trunk(x, mem, layers, norm_final_w):
    b, s, d = x.shape
    xt = x.reshape(b * s, d)
    for l, lw in enumerate(layers):
        xt = _layer(xt, mem, lw, b, s, norm_final_w, l == len(layers) - 1)
    return xt.reshape(b, s, d)


def kernel(x_prompt, x_sample, mem_prompt, mem_sample, norm_mix_w, w_in, lambda_q1, lambda_k1, lambda_q2,
           lambda_k2, attn_subln_w, conv_w, conv_b, a_log, dt_bias, d_skip, ssm_norm_w, w_attn_branch,
           w_ssm_branch, w_out, norm_cross_w, norm_mem_w, w_xq, w_xkv, w_xo, norm_ffn_w, w_router, b_router,
           w_gate, w_up, w_down, norm_final_w):
    p = dict(norm_mix_w=norm_mix_w, w_in=w_in, lambda_q1=lambda_q1, lambda_k1=lambda_k1, lambda_q2=lambda_q2,
             lambda_k2=lambda_k2, attn_subln_w=attn_subln_w, conv_w=conv_w, conv_b=conv_b, a_log=a_log,
             dt_bias=dt_bias, d_skip=d_skip, ssm_norm_w=ssm_norm_w, w_attn_branch=w_attn_branch,
             w_ssm_branch=w_ssm_branch, w_out=w_out, norm_cross_w=norm_cross_w, norm_mem_w=norm_mem_w,
             w_xq=w_xq, w_xkv=w_xkv, w_xo=w_xo, norm_ffn_w=norm_ffn_w, w_router=w_router, b_router=b_router,
             w_gate=w_gate, w_up=w_up, w_down=w_down)
    layers = [_prep_layer(l, p) for l in range(w_in.shape[0])]
    y_prompt = _trunk(x_prompt, mem_prompt, layers, norm_final_w)
    y_sample = _trunk(x_sample, mem_sample, layers, norm_final_w)
    return (y_prompt, y_sample)
```

```python
import functools
import math

import jax
import jax.numpy as jnp
from jax import lax
from jax.experimental import pallas as pl
from jax.experimental.pallas import tpu as pltpu

F32 = jnp.float32
BF16 = jnp.bfloat16
I32 = jnp.int32

D_MODEL = 1024
ATTN_HEADS = 8
ATTN_HEAD_DIM = 64
ATTN_V_DIM = 128
SSM_HEADS = 32
SSM_HEAD_DIM = 64
SSM_GROUPS = 4
D_STATE = 128
D_INNER = 2048
D_CONV = 5
XBC_DIM = 3072
CHUNK = 128
MEM_TOKENS = 256
X_HEADS = 4
X_HEAD_DIM = 128
X_WIDTH = 512
N_EXPERTS = 16
CAPACITY_FACTOR = 2
D_FF = 2816
EPS = 1e-6
SUBLN_EPS = 1e-5

C_Z = 0
C_GATE = 2048
C_XBC = 4096
C_Q = 7168
C_K = 8192
PROJ_COLS = 9216

LANE = 128
BF16_SUBLANE = 16
VMEM_LIMIT = 56 * 1024 * 1024


def _cparams(sem, vmem=None):
    return pltpu.CompilerParams(dimension_semantics=sem, vmem_limit_bytes=vmem)


def _rms(x, w, eps):
    return x * lax.rsqrt(jnp.mean(x * x, axis=-1, keepdims=True) + eps) * w


def _softplus(x):
    return jnp.maximum(x, 0.0) + jnp.log(1.0 + jnp.exp(-jnp.abs(x)))


def _silu(x):
    return x * (1.0 / (1.0 + jnp.exp2(x * (-1.0 / math.log(2.0)))))


def _norm_matmul_kernel(x_ref, nw_ref, w_ref, o_ref, xn_ref):
    @pl.when(pl.program_id(1) == 0)
    def _():
        xn_ref[...] = _rms(x_ref[...], nw_ref[...], EPS).astype(xn_ref.dtype)

    o_ref[...] = jnp.dot(xn_ref[...], w_ref[...], preferred_element_type=F32).astype(o_ref.dtype)


def norm_matmul(x, nw, w, out_dtype, tm, tn):
    t, d = x.shape
    n = w.shape[1]
    return pl.pallas_call(
        _norm_matmul_kernel,
        grid=(t // tm, n // tn),
        in_specs=[
            pl.BlockSpec((tm, d), lambda i, j: (i, 0)),
            pl.BlockSpec((1, d), lambda i, j: (0, 0)),
            pl.BlockSpec((d, tn), lambda i, j: (0, j)),
        ],
        out_specs=pl.BlockSpec((tm, tn), lambda i, j: (i, j)),
        out_shape=jax.ShapeDtypeStruct((t, n), out_dtype),
        scratch_shapes=[pltpu.VMEM((tm, d), BF16)],
        compiler_params=_cparams(("parallel", "arbitrary"), VMEM_LIMIT),
        name="norm_matmul",
    )(x, nw.reshape(1, d), w)


ONES_ROWS = BF16_SUBLANE
SKIP_MARGIN = 17.0
POS_SPLIT = 32


FIXED_REF_MAX_RANGE = 60.0


def _attn_kernel(scal_ref, q_ref, k_ref, vt_ref, w_ref, qf_ref, kf_ref, o_ref, qa_s, m_s, acc_s, kn_s, ref_s,
                 *, tq, tk, seq):
    h = pl.program_id(1)
    qi = pl.program_id(2)
    slope = scal_ref[h]
    nk = seq // tk
    ndiag = tq // tk
    t0 = qi * tq
    dlo = qi * ndiag
    half = ATTN_HEAD_DIM
    nt = (((1,), (1,)), ((), ()))

    lane_k = lax.broadcasted_iota(I32, (tk, LANE), 1)
    pick = jnp.where(lax.broadcasted_iota(I32, (8, LANE), 1) // half == lax.broadcasted_iota(I32, (8, LANE), 0),
                     1.0, 0.0).astype(BF16)

    @pl.when(qi == 0)
    def _():
        def body(i, carry):
            kc = k_ref[pl.ds(pl.multiple_of(i * tk, tk), tk), :].astype(F32)
            rows = lax.dot_general(pick, (kc * kc).astype(BF16), nt, preferred_element_type=F32)
            return jnp.maximum(carry, jnp.max(rows, axis=1, keepdims=True))

        norms = lax.fori_loop(0, nk, body, jnp.zeros((8, 1), F32))
        kn_s[0] = jnp.broadcast_to(norms[0:1, :], (8, LANE))
        kn_s[1] = jnp.broadcast_to(norms[1:2, :], (8, LANE))

    qb = q_ref[...] * (ATTN_HEAD_DIM ** -0.5)
    lane_q = lax.broadcasted_iota(I32, (tq, LANE), 1)
    q = qb.astype(F32)
    kd = k_ref[pl.ds(pl.multiple_of(t0, tq), tq), :].astype(F32)
    qn_rows = lax.dot_general(pick, (q * q).astype(BF16), nt, preferred_element_type=F32)
    self_rows = lax.dot_general(pick, (q * kd).astype(BF16), nt, preferred_element_type=F32)
    bounds = []
    for c in range(2):
        data = (lane_q < half) if c == 0 else (lane_q >= half)
        qa_s[c] = jnp.where(data, qb, qf_ref[c])
        ref_c = jnp.sqrt(qn_rows[c:c + 1, :] * kn_s[c][0:1, 0:1]) * 1.02
        ref_s[c] = ref_c
        bounds.append(jnp.max(ref_c * (1.0 + 2.0 ** -8) - self_rows[c:c + 1, :], axis=1, keepdims=True))

    span = jnp.maximum(bounds[0], bounds[1])
    count = jnp.minimum(span * 0.0 + scal_ref[ATTN_HEADS + 2 + h], math.log(seq))
    dskip = jnp.minimum((span + count + SKIP_MARGIN) / slope, 2.0 * seq)
    t0f = jnp.full((1, 1), t0, I32).astype(F32)
    dlof = jnp.full((1, 1), dlo, I32).astype(F32)
    klo_f = jnp.clip(jnp.floor((t0f + 1.0 - dskip) / tk), 0.0, dlof)
    khi_f = jnp.clip(jnp.ceil((dskip + t0f + (tq - 1.0)) / tk), dlof + ndiag, float(nk))
    klo = jnp.max(klo_f).astype(I32)
    khi = jnp.max(khi_f).astype(I32)
    fixed_ref = jnp.max(jnp.where(span <= FIXED_REF_MAX_RANGE, 1.0, 0.0)).astype(I32)

    m_s[...] = jnp.full(m_s.shape, -jnp.inf, F32)
    acc_s[...] = jnp.zeros(acc_s.shape, F32)
    ones = jnp.ones((ONES_ROWS, tk), BF16)

    def block(ki, side, fixed, d=0):
        s0 = pl.multiple_of(ki * tk, tk)
        kblk = k_ref[pl.ds(s0, tk), :]
        vaug = jnp.concatenate([vt_ref[:, pl.ds(s0, tk)], ones], axis=0)
        off = jnp.full((1, tq), t0 - s0, I32).astype(F32) * slope
        if side == 'D':
            right = ndiag > 1 and d == ndiag - 1
            c_lo, c_hi = (d * tk, tq) if right else (0, (d + 1) * tk)
            rel = (lax.broadcasted_iota(I32, (tk, c_hi - c_lo), 0) + (d * tk - c_lo)
                   - lax.broadcasted_iota(I32, (tk, c_hi - c_lo), 1)).astype(F32)
            corr = jnp.maximum(-rel if right else rel, 0.0) * (-2.0 * slope)
            feats = 'R' if right else 'L'
        else:
            feats = side
        cblk = -off if feats == 'R' else off
        for c in range(2):
            data = (lane_k < half) if c == 0 else (lane_k >= half)
            kaug = jnp.where(data, kblk, kf_ref[2 * c + (1 if feats == 'R' else 0)])
            st = lax.dot_general(kaug, qa_s[c], nt, preferred_element_type=F32)
            if side == 'D':
                pieces = [st[:, c_lo:c_hi] + corr]
                if c_lo > 0:
                    pieces.insert(0, st[:, :c_lo])
                if c_hi < tq:
                    pieces.append(st[:, c_hi:])
                st = jnp.concatenate(pieces, axis=1) if len(pieces) > 1 else pieces[0]
            if fixed:
                p = jnp.exp(st - (ref_s[c] + cblk)).astype(BF16)
                acc_s[c] += jnp.dot(vaug, p, preferred_element_type=F32)
            else:
                m_prev = m_s[c]
                m_new = jnp.maximum(m_prev, jnp.max(st, axis=0, keepdims=True) - cblk)
                p = jnp.exp(st - (m_new + cblk)).astype(BF16)
                alpha = jnp.exp(m_prev - m_new)
                acc_s[c] = acc_s[c] * alpha + jnp.dot(vaug, p, preferred_element_type=F32)
                m_s[c] = m_new

    def sweep(fixed):
        def run(side):
            def body(ki, carry):
                block(ki, side, fixed)
                return carry
            return body

        def pairs(first, end, side):
            def body(i, carry):
                block(first + 2 * i, side, fixed)
                block(first + 2 * i + 1, side, fixed)
                return carry

            lax.fori_loop(0, (end - first) // 2, body, 0)

            @pl.when((end - first) % 2 == 1)
            def _():
                block(end - 1, side, fixed)

        pairs(klo, dlo, 'L')
        for d in range(ndiag):
            block(dlo + d, 'D', fixed, d)
        pairs(dlo + ndiag, khi, 'R')

    @pl.when(fixed_ref == 1)
    def _():
        sweep(True)

    @pl.when(fixed_ref == 0)
    def _():
        sweep(False)

    lam = scal_ref[ATTN_HEADS]
    post = scal_ref[ATTN_HEADS + 1]
    a0 = acc_s[0]
    a1 = acc_s[1]
    o = a0[0:ATTN_V_DIM] / a0[ATTN_V_DIM:ATTN_V_DIM + 1] - lam * (a1[0:ATTN_V_DIM] / a1[ATTN_V_DIM:ATTN_V_DIM + 1])
    o = o * lax.rsqrt(jnp.mean(o * o, axis=0, keepdims=True) + SUBLN_EPS) * (w_ref[...] * post)
    o_ref[...] = o.T.astype(o_ref.dtype)


def _position_features(slopes, tq, tk):
    half = ATTN_HEAD_DIM
    sl = slopes.reshape(-1, 1, 1).astype(F32)
    lane = jnp.arange(LANE)[None, None, :]
    one = jnp.ones((1, 1, 1), F32)

    def tile(n, vals, base):
        out = jnp.zeros((slopes.shape[0], n, LANE), F32)
        for i, v in enumerate(vals):
            out = jnp.where(lane == base + i, v, out)
        return out

    def split(n):
        pos = jnp.arange(n, dtype=F32)[None, :, None]
        lo = jnp.mod(pos, float(POS_SPLIT))
        return pos - lo, lo

    qhi, qlo = split(tq)
    khi, klo = split(tk)
    qf, kf = [], []
    for c in range(2):
        base = half if c == 0 else 0
        qf.append(tile(tq, (-sl * qhi, -sl * qlo, one, one), base))
        left = tile(tk, (one, one, sl * khi, sl * klo), base)
        kf += [left, -left]
    return jnp.stack(qf, axis=1).astype(BF16), jnp.stack(kf, axis=1).astype(BF16)


def diff_attention(proj3, vt, scal, subln_w, tq, tk):
    b, s, _ = proj3.shape
    kq, kk = C_Q // LANE, C_K // LANE
    qfeat, kfeat = _position_features(scal[:ATTN_HEADS], tq, tk)
    return pl.pallas_call(
        functools.partial(_attn_kernel, tq=tq, tk=tk, seq=s),
        grid=(b, ATTN_HEADS, s // tq),
        in_specs=[
            pl.BlockSpec(memory_space=pltpu.SMEM),
            pl.BlockSpec((None, tq, LANE), lambda bi, h, qi: (bi, qi, kq + h)),
            pl.BlockSpec((None, s, LANE), lambda bi, h, qi: (bi, 0, kk + h)),
            pl.BlockSpec((ATTN_V_DIM, s), lambda bi, h, qi: (h, bi)),
            pl.BlockSpec((ATTN_V_DIM, 1), lambda bi, h, qi: (0, 0)),
            pl.BlockSpec((None, 2, tq, LANE), lambda bi, h, qi: (h, 0, 0, 0)),
            pl.BlockSpec((None, 4, tk, LANE), lambda bi, h, qi: (h, 0, 0, 0)),
        ],
        out_specs=pl.BlockSpec((None, tq, LANE), lambda bi, h, qi: (bi, qi, h)),
        out_shape=jax.ShapeDtypeStruct((b, s, ATTN_HEADS * ATTN_V_DIM), BF16),
        scratch_shapes=[
            pltpu.VMEM((2, tq, LANE), BF16),
            pltpu.VMEM((2, 1, tq), F32),
            pltpu.VMEM((2, ATTN_V_DIM + ONES_ROWS, tq), F32),
            pltpu.VMEM((2, 8, LANE), F32),
            pltpu.VMEM((2, 1, tq), F32),
        ],
        compiler_params=_cparams(("parallel", "parallel", "arbitrary"), VMEM_LIMIT),
        name="diff_attention",
    )(scal, proj3, proj3, vt, subln_w.reshape(ATTN_V_DIM, 1), qfeat, kfeat)


def _proj_t_kernel(x_ref, nw_ref, wvt_ref, wdt_ref, wdtt_ref, vt_ref, dt_ref, dtt_ref):
    xn = _rms(x_ref[...], nw_ref[...], EPS).astype(BF16)
    nt = (((1,), (1,)), ((), ()))
    vt_ref[...] = lax.dot_general(wvt_ref[...], xn, nt, preferred_element_type=F32).astype(vt_ref.dtype)
    dt_ref[...] = jnp.dot(xn, wdt_ref[...], preferred_element_type=F32)
    dtt_ref[...] = lax.dot_general(wdtt_ref[...], xn, nt, preferred_element_type=F32)


def proj_transposed(x, nw, wvt, wdt, wdtt, tm):
    t, d = x.shape
    nv = wvt.shape[0]
    const = lambda i: (0, 0)
    return pl.pallas_call(
        _proj_t_kernel,
        grid=(t // tm,),
        in_specs=[
            pl.BlockSpec((tm, d), lambda i: (i, 0)),
            pl.BlockSpec((1, d), const),
            pl.BlockSpec((nv, d), const),
            pl.BlockSpec((d, LANE), const),
            pl.BlockSpec((LANE, d), const),
        ],
        out_specs=[pl.BlockSpec((nv, tm), lambda i: (0, i)), pl.BlockSpec((tm, LANE), lambda i: (i, 0)),
                   pl.BlockSpec((LANE, tm), lambda i: (0, i))],
        out_shape=[jax.ShapeDtypeStruct((nv, t), BF16), jax.ShapeDtypeStruct((t, LANE), F32),
                   jax.ShapeDtypeStruct((LANE, t), F32)],
        compiler_params=_cparams(("parallel",), VMEM_LIMIT),
        name="proj_transposed",
    )(x, nw.reshape(1, d), wvt, wdt, wdtt)


HALO = 16


CONV_SUB = 128


def _conv_kernel(prev_ref, cur_ref, next_ref, w_ref, b_ref, o_ref, ext_ref, *, tm):
    i = pl.program_id(1)
    last = pl.num_programs(1) - 1
    zero = jnp.zeros(prev_ref.shape, BF16)
    ext_ref[pl.ds(0, HALO), :] = jnp.where(i > 0, prev_ref[...], zero)
    ext_ref[pl.ds(HALO, tm), :] = cur_ref[...]
    ext_ref[pl.ds(HALO + tm, HALO), :] = jnp.where(i < last, next_ref[...], zero)
    w = w_ref[...]
    bias = b_ref[...]
    taps = [j for j in range(D_CONV) if j != D_CONV // 2]
    win = CONV_SUB + 2 * HALO
    r = lax.broadcasted_iota(I32, (len(taps) * CONV_SUB, win), 0)
    c = lax.broadcasted_iota(I32, (len(taps) * CONV_SUB, win), 1)
    n = r >> (CONV_SUB.bit_length() - 1)
    shift = jnp.where(n < D_CONV // 2, n, n + 1) - D_CONV // 2
    pick = jnp.where(c == HALO + (r - n * CONV_SUB) + shift, 1.0, 0.0).astype(BF16)
    for sb in range(tm // CONV_SUB):
        xe = ext_ref[pl.ds(sb * CONV_SUB, win), :]
        moved = jnp.dot(pick, xe, preferred_element_type=F32)
        acc = bias + xe[HALO:HALO + CONV_SUB].astype(F32) * w[D_CONV // 2:D_CONV // 2 + 1, :]
        for k, j in enumerate(taps):
            acc = acc + moved[k * CONV_SUB:(k + 1) * CONV_SUB] * w[j:j + 1, :]
        o_ref[pl.ds(sb * CONV_SUB, CONV_SUB), :] = _silu(acc).astype(o_ref.dtype)


def conv_silu(proj3, conv_w, conv_b, tm, tc):
    b, s, _ = proj3.shape
    c0 = C_XBC // tc
    hb = tm // HALO
    nh = s // HALO
    return pl.pallas_call(
        functools.partial(_conv_kernel, tm=tm),
        grid=(b, s // tm, XBC_DIM // tc),
        in_specs=[
            pl.BlockSpec((None, HALO, tc), lambda bi, i, j: (bi, jnp.maximum(i * hb - 1, 0), c0 + j)),
            pl.BlockSpec((None, tm, tc), lambda bi, i, j: (bi, i, c0 + j)),
            pl.BlockSpec((None, HALO, tc), lambda bi, i, j: (bi, jnp.minimum((i + 1) * hb, nh - 1), c0 + j)),
            pl.BlockSpec((D_CONV, tc), lambda bi, i, j: (0, j)),
            pl.BlockSpec((1, tc), lambda bi, i, j: (0, j)),
        ],
        out_specs=pl.BlockSpec((None, tm, tc), lambda bi, i, j: (bi, i, j)),
        out_shape=jax.ShapeDtypeStruct((b, s, XBC_DIM), BF16),
        scratch_shapes=[pltpu.VMEM((tm + 2 * HALO, tc), BF16)],
        compiler_params=_cparams(("parallel", "parallel", "parallel"), VMEM_LIMIT),
        name="conv_silu",
    )(proj3, proj3, proj3, conv_w, conv_b.reshape(1, XBC_DIM))


def _expand_heads(a):
    hi = a.astype(BF16)
    lo = (a - hi.astype(F32)).astype(BF16)
    head_of = lax.broadcasted_iota(I32, (SSM_HEADS, D_INNER), 1) // SSM_HEAD_DIM
    e = jnp.where(head_of == lax.broadcasted_iota(I32, (SSM_HEADS, D_INNER), 0), 1.0, 0.0).astype(BF16)
    return jnp.dot(hi, e, preferred_element_type=F32) + jnp.dot(lo, e, preferred_element_type=F32)


def _tri(n, fn):
    r = lax.broadcasted_iota(I32, (n, n), 0)
    c = lax.broadcasted_iota(I32, (n, n), 1)
    return jnp.where(fn(r, c), 1.0, 0.0).astype(F32)


def _dot_hi(a, b):
    return jnp.dot(a, b, preferred_element_type=F32, precision=lax.Precision.HIGHEST)


def _ssd_state_kernel(xf_ref, bf_ref, dtf_ref, xb_ref, bb_ref, dtb_ref, par_ref, hf_ref, hb_ref, sf_s, sb_s):
    j = pl.program_id(1)

    @pl.when(j == 0)
    def _():
        sf_s[...] = jnp.zeros(sf_s.shape, F32)
        sb_s[...] = jnp.zeros(sb_s.shape, F32)

    hf_ref[...] = sf_s[...].astype(hf_ref.dtype)
    hb_ref[...] = sb_s[...].astype(hb_ref.dtype)

    par = par_ref[...]
    lower_incl = _tri(CHUNK, lambda r, c: c <= r)
    lower_strict = _tri(CHUNK, lambda r, c: c < r)

    def one_direction(x_ref, b_ref, dt_ref, s_ref, col0, backward):
        raw = dt_ref[...][:, col0:col0 + SSM_HEADS]
        dt = _softplus(raw + par[0:1, col0:col0 + SSM_HEADS])
        a = dt * par[1:2, col0:col0 + SSM_HEADS]
        if backward:
            excl = _dot_hi(lower_strict, a)
            wgt = dt * jnp.exp(excl)
            total = excl[CHUNK - 1:CHUNK, :] + a[CHUNK - 1:CHUNK, :]
        else:
            cum = _dot_hi(lower_incl, a)
            total = cum[CHUNK - 1:CHUNK, :]
            wgt = dt * jnp.exp(total - cum)
        xw = (x_ref[...].astype(F32) * _expand_heads(wgt)).astype(BF16)
        dec = _expand_heads(jnp.exp(jnp.broadcast_to(total, (8, SSM_HEADS))))[0:1, :]
        bmat = b_ref[...]
        gw = D_INNER // SSM_GROUPS
        for g in range(SSM_GROUPS):
            contrib = lax.dot_general(bmat[:, g * D_STATE:(g + 1) * D_STATE], xw[:, g * gw:(g + 1) * gw],
                                      (((0,), (0,)), ((), ())), preferred_element_type=F32)
            s_ref[g] = s_ref[g] * dec[:, g * gw:(g + 1) * gw] + contrib

    one_direction(xf_ref, bf_ref, dtf_ref, sf_s, 0, False)
    one_direction(xb_ref, bb_ref, dtb_ref, sb_s, SSM_HEADS, True)


def ssd_states(xbc, dt3, par):
    b, s, _ = xbc.shape
    nc = s // CHUNK
    gw = D_INNER // SSM_GROUPS
    xblk = D_INNER // D_INNER
    bcol = D_INNER // (SSM_GROUPS * D_STATE)
    hshape = jax.ShapeDtypeStruct((b, nc, SSM_GROUPS, D_STATE, gw), BF16)
    hspec_f = pl.BlockSpec((None, None, SSM_GROUPS, D_STATE, gw), lambda bi, j: (bi, j, 0, 0, 0))
    hspec_b = pl.BlockSpec((None, None, SSM_GROUPS, D_STATE, gw), lambda bi, j: (bi, nc - 1 - j, 0, 0, 0))
    del xblk
    return pl.pallas_call(
        _ssd_state_kernel,
        grid=(b, nc),
        in_specs=[
            pl.BlockSpec((None, CHUNK, D_INNER), lambda bi, j: (bi, j, 0)),
            pl.BlockSpec((None, CHUNK, SSM_GROUPS * D_STATE), lambda bi, j: (bi, j, bcol)),
            pl.BlockSpec((None, CHUNK, LANE), lambda bi, j: (bi, j, 0)),
            pl.BlockSpec((None, CHUNK, D_INNER), lambda bi, j: (bi, nc - 1 - j, 0)),
            pl.BlockSpec((None, CHUNK, SSM_GROUPS * D_STATE), lambda bi, j: (bi, nc - 1 - j, bcol)),
            pl.BlockSpec((None, CHUNK, LANE), lambda bi, j: (bi, nc - 1 - j, 0)),
            pl.BlockSpec((8, LANE), lambda bi, j: (0, 0)),
        ],
        out_specs=[hspec_f, hspec_b],
        out_shape=[hshape, hshape],
        scratch_shapes=[pltpu.VMEM((SSM_GROUPS, D_STATE, gw), F32), pltpu.VMEM((SSM_GROUPS, D_STATE, gw), F32)],
        compiler_params=_cparams(("parallel", "arbitrary"), VMEM_LIMIT),
        name="ssd_states",
    )(xbc, xbc, dt3, xbc, xbc, dt3, par)


def _ssd_out_kernel(x_ref, b_ref, c_ref, z_ref, dt_ref, dtt_ref, par_ref, part_ref, dsk_ref, nw_ref,
                    hf_ref, hb_ref, o_ref):
    par = par_ref[...]
    part = part_ref[...]
    lower_incl = _tri(CHUNK, lambda r, c: c <= r)
    lower_strict = _tri(CHUNK, lambda r, c: c < r)
    upper_incl = _tri(CHUNK, lambda r, c: r <= c)
    upper_strict = _tri(CHUNK, lambda r, c: r < c)

    dtc = _softplus(dt_ref[...][:, 0:2 * SSM_HEADS] + par[0:1, 0:2 * SSM_HEADS])
    ac = dtc * par[1:2, 0:2 * SSM_HEADS]
    cum_c = _dot_hi(lower_incl, ac)
    excl_c = _dot_hi(lower_strict, ac)
    dtr = _softplus(dtt_ref[...][0:2 * SSM_HEADS, :] + part[0:2 * SSM_HEADS, 0:1])
    ar = dtr * part[0:2 * SSM_HEADS, 1:2]
    cum_r = _dot_hi(ar, upper_incl)
    excl_r = _dot_hi(ar, upper_strict)

    cumf_c = cum_c[:, 0:SSM_HEADS]
    exclb_c = excl_c[:, SSM_HEADS:2 * SSM_HEADS]
    totb = cum_c[CHUNK - 1:CHUNK, SSM_HEADS:2 * SSM_HEADS]
    dtf_r = dtr[0:SSM_HEADS, :]
    dtb_r = dtr[SSM_HEADS:2 * SSM_HEADS, :]
    fwd_r = cum_r[0:SSM_HEADS, :] - jnp.log(dtf_r)
    bwd_r = excl_r[SSM_HEADS:2 * SSM_HEADS, :] + jnp.log(dtb_r)
    diag_r = jnp.log(dtf_r + dtb_r)

    row = lax.broadcasted_iota(I32, (CHUNK, CHUNK), 0)
    col = lax.broadcasted_iota(I32, (CHUNK, CHUNK), 1)
    low = col < row
    diag = col == row
    lane = lax.broadcasted_iota(I32, (CHUNK, LANE), 1)
    first = lane < SSM_HEAD_DIM

    gw = D_INNER // SSM_GROUPS
    hpg = SSM_HEADS // SSM_GROUPS
    scale_f = _expand_heads(jnp.exp(cumf_c))
    scale_b = _expand_heads(jnp.exp(totb - exclb_c))
    for g in range(SSM_GROUPS):
        cg = c_ref[:, g * D_STATE:(g + 1) * D_STATE]
        bg = b_ref[:, g * D_STATE:(g + 1) * D_STATE]
        cb = lax.dot_general(cg, bg, (((1,), (1,)), ((), ())), preferred_element_type=F32)
        inter_f = jnp.dot(cg, hf_ref[g], preferred_element_type=F32)
        inter_b = jnp.dot(cg, hb_ref[g], preferred_element_type=F32)
        parts = []
        for pp in range(hpg // 2):
            gms = []
            for h in (g * hpg + 2 * pp, g * hpg + 2 * pp + 1):
                arg = jnp.where(low, cumf_c[:, h:h + 1] - fwd_r[h:h + 1, :],
                                bwd_r[h:h + 1, :] - exclb_c[:, h:h + 1])
                arg = jnp.where(diag, diag_r[h:h + 1, :], arg)
                gms.append((cb * jnp.exp(arg)).astype(BF16))
            ls = slice(g * gw + pp * LANE, g * gw + (pp + 1) * LANE)
            il = slice(pp * LANE, (pp + 1) * LANE)
            xp = x_ref[:, ls]
            rhs = jnp.concatenate([jnp.where(first, xp, 0), jnp.where(first, 0, xp)], axis=0)
            yp = jnp.dot(jnp.concatenate(gms, axis=1), rhs, preferred_element_type=F32)
            yp = (yp + inter_f[:, il] * scale_f[:, ls] + inter_b[:, il] * scale_b[:, ls]
                  + xp.astype(F32) * dsk_ref[:, ls])
            parts.append(yp * _silu(z_ref[:, ls].astype(F32)))
        yg = jnp.concatenate(parts, axis=1)
        sl = slice(g * gw, (g + 1) * gw)
        o_ref[:, sl] = (yg * lax.rsqrt(jnp.mean(yg * yg, axis=-1, keepdims=True) + EPS)
                        * nw_ref[:, sl]).astype(o_ref.dtype)


def ssd_output(xbc, proj3, dt3, dtt3, par, part, dskip_x, nw, hf, hb):
    b, s, _ = xbc.shape
    nc = s // CHUNK
    gw = D_INNER // SSM_GROUPS
    gs = SSM_GROUPS * D_STATE
    hspec = pl.BlockSpec((None, None, SSM_GROUPS, D_STATE, gw), lambda bi, j: (bi, j, 0, 0, 0))
    return pl.pallas_call(
        _ssd_out_kernel,
        grid=(b, nc),
        in_specs=[
            pl.BlockSpec((None, CHUNK, D_INNER), lambda bi, j: (bi, j, 0)),
            pl.BlockSpec((None, CHUNK, gs), lambda bi, j: (bi, j, D_INNER // gs)),
            pl.BlockSpec((None, CHUNK, gs), lambda bi, j: (bi, j, D_INNER // gs + 1)),
            pl.BlockSpec((None, CHUNK, D_INNER), lambda bi, j: (bi, j, C_Z // D_INNER)),
            pl.BlockSpec((None, CHUNK, LANE), lambda bi, j: (bi, j, 0)),
            pl.BlockSpec((LANE, CHUNK), lambda bi, j: (0, bi * nc + j)),
            pl.BlockSpec((8, LANE), lambda bi, j: (0, 0)),
            pl.BlockSpec((LANE, 8), lambda bi, j: (0, 0)),
            pl.BlockSpec((1, D_INNER), lambda bi, j: (0, 0)),
            pl.BlockSpec((1, D_INNER), lambda bi, j: (0, 0)),
            hspec, hspec,
        ],
        out_specs=pl.BlockSpec((None, CHUNK, D_INNER), lambda bi, j: (bi, j, 0)),
        out_shape=jax.ShapeDtypeStruct((b, s, D_INNER), BF16),
        compiler_params=_cparams(("parallel", "parallel"), VMEM_LIMIT),
        name="ssd_output",
    )(xbc, xbc, xbc, proj3, dt3, dtt3, par, part, dskip_x, nw, hf, hb)


def _merge_kernel(x_ref, a_ref, s_ref, g0_ref, g1_ref, wa_ref, ws_ref, wo_ref, o_ref):
    ya = jnp.dot(a_ref[...], wa_ref[...], preferred_element_type=F32)
    ys = jnp.dot(s_ref[...], ws_ref[...], preferred_element_type=F32)
    g0 = 1.0 / (1.0 + jnp.exp(-g0_ref[...].astype(F32)))
    g1 = 1.0 / (1.0 + jnp.exp(-g1_ref[...].astype(F32)))
    merged = (g0 * ya + g1 * ys).astype(BF16)
    o_ref[...] = x_ref[...] + jnp.dot(merged, wo_ref[...], preferred_element_type=F32)


def merge_out(x, attn, ssm, proj, wa, ws, wo, tm):
    t, d = x.shape
    gb = C_GATE // d
    const = lambda i: (0, 0)
    return pl.pallas_call(
        _merge_kernel,
        grid=(t // tm,),
        in_specs=[
            pl.BlockSpec((tm, d), lambda i: (i, 0)),
            pl.BlockSpec((tm, d), lambda i: (i, 0)),
            pl.BlockSpec((tm, D_INNER), lambda i: (i, 0)),
            pl.BlockSpec((tm, d), lambda i: (i, gb)),
            pl.BlockSpec((tm, d), lambda i: (i, gb + 1)),
            pl.BlockSpec((d, d), const),
            pl.BlockSpec((D_INNER, d), const),
            pl.BlockSpec((d, d), const),
        ],
        out_specs=pl.BlockSpec((tm, d), lambda i: (i, 0)),
        out_shape=jax.ShapeDtypeStruct((t, d), F32),
        compiler_params=_cparams(("parallel",), VMEM_LIMIT),
        name="merge_out",
    )(x, attn, ssm, proj, proj, wa, ws, wo)


def _cross_kernel(x_ref, nw_ref, wq_ref, kv_ref, wo_ref, o_ref):
    x = x_ref[...]
    h = _rms(x, nw_ref[...], EPS).astype(BF16)
    q = jnp.dot(h, wq_ref[...], preferred_element_type=F32).astype(BF16)
    kv = kv_ref[...]
    outs = []
    for hd in range(X_HEADS):
        qh = q[:, hd * X_HEAD_DIM:(hd + 1) * X_HEAD_DIM]
        kh = kv[:, hd * X_HEAD_DIM:(hd + 1) * X_HEAD_DIM]
        vh = kv[:, X_WIDTH + hd * X_HEAD_DIM:X_WIDTH + (hd + 1) * X_HEAD_DIM]
        s = lax.dot_general(qh, kh, (((1,), (1,)), ((), ())), preferred_element_type=F32) * (X_HEAD_DIM ** -0.5)
        s = s - jnp.max(s, axis=-1, keepdims=True)
        p = jnp.exp(s)
        p = (p / jnp.sum(p, axis=-1, keepdims=True)).astype(BF16)
        outs.append(jnp.dot(p, vh, preferred_element_type=F32).astype(BF16))
    o = jnp.concatenate(outs, axis=1)
    o_ref[...] = x + jnp.dot(o, wo_ref[...], preferred_element_type=F32)


def cross_attention(x, nw, wq, kv, wo, seq, tm):
    t, d = x.shape
    per_seq = seq // tm
    const = lambda i: (0, 0)
    return pl.pallas_call(
        _cross_kernel,
        grid=(t // tm,),
        in_specs=[
            pl.BlockSpec((tm, d), lambda i: (i, 0)),
            pl.BlockSpec((1, d), const),
            pl.BlockSpec((d, X_WIDTH), const),
            pl.BlockSpec((None, MEM_TOKENS, 2 * X_WIDTH), lambda i: (i // per_seq, 0, 0)),
            pl.BlockSpec((X_WIDTH, d), const),
        ],
        out_specs=pl.BlockSpec((tm, d), lambda i: (i, 0)),
        out_shape=jax.ShapeDtypeStruct((t, d), F32),
        compiler_params=_cparams(("parallel",), VMEM_LIMIT),
        name="cross_attention",
    )(x, nw.reshape(1, d), wq, kv, wo)


def _router_kernel(x_ref, nw_ref, wrt_ref, br_ref, hb_ref, aff_ref):
    h = _rms(x_ref[...], nw_ref[...], EPS)
    hb_ref[...] = h.astype(hb_ref.dtype)
    logits = lax.dot_general(wrt_ref[...], h, (((1,), (1,)), ((), ())), preferred_element_type=F32,
                             precision=lax.Precision.HIGHEST) + br_ref[...]
    e = jnp.exp(logits - jnp.max(logits, axis=0, keepdims=True))
    aff_ref[...] = e / jnp.sum(e, axis=0, keepdims=True)


def router(x, nw, wrt, br, tm):
    t, d = x.shape
    return pl.pallas_call(
        _router_kernel,
        grid=(t // tm,),
        in_specs=[
            pl.BlockSpec((tm, d), lambda i: (i, 0)),
            pl.BlockSpec((1, d), lambda i: (0, 0)),
            pl.BlockSpec((N_EXPERTS, d), lambda i: (0, 0)),
            pl.BlockSpec((N_EXPERTS, 1), lambda i: (0, 0)),
        ],
        out_specs=[pl.BlockSpec((tm, d), lambda i: (i, 0)), pl.BlockSpec((N_EXPERTS, tm), lambda i: (0, i))],
        out_shape=[jax.ShapeDtypeStruct((t, d), BF16), jax.ShapeDtypeStruct((N_EXPERTS, t), F32)],
        compiler_params=_cparams(("parallel",), VMEM_LIMIT),
        name="router",
    )(x, nw.reshape(1, d), wrt, br.reshape(N_EXPERTS, 1))


TB = 256


def _topk_kernel(aff_ref, pos_ref, gate_ref, roff_ref, *, cap):
    a = aff_ref[...]
    nr = a.shape[1]
    bits = pltpu.bitcast(a, I32)

    def count(mask):
        c = jnp.sum(jnp.where(mask, 1.0, 0.0), axis=2, keepdims=True)
        return jnp.sum(c, axis=1, keepdims=True)

    def body(i, thr):
        cand = thr | jnp.left_shift(jnp.int32(1), 30 - i)
        return jnp.where(count(bits >= cand) >= float(cap), cand, thr)

    thr = lax.fori_loop(0, 31, body, jnp.zeros((N_EXPERTS, 1, 1), I32))
    gt = bits > thr
    eq = bits == thr
    need = float(cap) - count(gt)

    ustrict = _tri(TB, lambda r, c: r < c).astype(BF16)
    ones = jnp.ones((TB, TB), BF16)
    lstrict = _tri(nr, lambda r, c: c < r).astype(BF16)

    def excl_prefix(m):
        mb = m.astype(BF16)
        within = jnp.dot(mb, ustrict, preferred_element_type=F32)
        rowsum = jnp.dot(mb, ones, preferred_element_type=F32)
        rowoff = jnp.dot(lstrict, rowsum.astype(BF16), preferred_element_type=F32)
        return within + rowoff, rowoff

    for e in range(N_EXPERTS):
        eq_e = jnp.where(eq[e], 1.0, 0.0)
        pe, _ = excl_prefix(eq_e)
        keep = jnp.where(pe < need[e], eq_e, 0.0)
        sel = jnp.where(gt[e], 1.0, keep)
        ps, roff = excl_prefix(sel)
        chosen = sel > 0.5
        pos_ref[e] = jnp.where(chosen, ps.astype(I32), -1)
        gate_ref[e] = jnp.where(chosen, a[e], 0.0)
        roff_ref[e] = roff.astype(I32)


def topk_select(aff3, cap):
    e, r, tb = aff3.shape
    full = lambda: (0, 0, 0)
    return pl.pallas_call(
        functools.partial(_topk_kernel, cap=cap),
        grid=(),
        in_specs=[pl.BlockSpec((e, r, tb), full)],
        out_specs=[pl.BlockSpec((e, r, tb), full)] * 3,
        out_shape=[jax.ShapeDtypeStruct((e, r, tb), I32), jax.ShapeDtypeStruct((e, r, tb), F32),
                   jax.ShapeDtypeStruct((e, r, tb), I32)],
        compiler_params=pltpu.CompilerParams(vmem_limit_bytes=VMEM_LIMIT),
        name="topk_select",
    )(aff3)


GATHER_BUFS = 8


GATHER_PARTS = 2


def _ffn_kernel(lo_ref, hi_ref, roff_ref, rend_ref, pos_ref, gate_ref, hb_hbm, wg_ref, wu_ref, wd_ref, y_ref,
                xbuf, sem, xc_s, g_s, *, ts, nj):
    e = pl.program_id(0)
    j = pl.program_id(1)
    nr = pos_ref.shape[0]
    step = e * nj + j
    nsteps = pl.num_programs(0) * nj
    lo = lo_ref[step]
    hi = hi_ref[step]
    xc_s[...] = jnp.zeros(xc_s.shape, F32)
    g_s[...] = jnp.zeros(g_s.shape, F32)

    def fetch(r, slot):
        return pltpu.make_async_copy(hb_hbm.at[pl.ds(r * TB, TB)], xbuf.at[slot], sem.at[slot])

    def prime(first, end):
        for i in range(GATHER_BUFS - 1):
            @pl.when(first + i < end)
            def _(i=i):
                fetch(first + i, i).start()

    @pl.when(step == 0)
    def _():
        prime(lo, hi)

    part = ts // GATHER_PARTS
    slot_ids = lax.broadcasted_iota(I32, (part, TB), 0)

    def body(r, carry):
        idx = r - lo
        slot = idx & (GATHER_BUFS - 1)
        fetch(r, slot).wait()

        @pl.when(r + (GATHER_BUFS - 1) < hi)
        def _():
            fetch(r + (GATHER_BUFS - 1), (idx + (GATHER_BUFS - 1)) & (GATHER_BUFS - 1)).start()

        prow = pos_ref[pl.ds(r, 1), :]
        grow = gate_ref[pl.ds(r, 1), :]
        first = roff_ref[e * nr + r]
        last = rend_ref[e * nr + r]
        for p in range(GATHER_PARTS):
            base = j * ts + p * part

            @pl.when(jnp.logical_and(first < base + part, last > base))
            def _(p=p, base=base):
                hit = prow == slot_ids + base
                sel = jnp.where(hit, 1.0, 0.0).astype(BF16)
                rows = slice(p * part, (p + 1) * part)
                xc_s[rows, :] += jnp.dot(sel, xbuf[slot], preferred_element_type=F32)
                g_s[rows, :] += jnp.sum(jnp.where(hit, grow, 0.0), axis=1, keepdims=True)
        return carry

    lax.fori_loop(lo, hi, body, 0)

    @pl.when(step + 1 < nsteps)
    def _():
        prime(lo_ref[step + 1], hi_ref[step + 1])

    xc = xc_s[...].astype(BF16)
    hg = jnp.dot(xc, wg_ref[...], preferred_element_type=F32)
    hu = jnp.dot(xc, wu_ref[...], preferred_element_type=F32)
    hid = (_silu(hg) * hu).astype(BF16)
    y = jnp.dot(hid, wd_ref[...], preferred_element_type=F32) * g_s[...]
    y_ref[...] = y.astype(y_ref.dtype)


def expert_ffn(lo, hi, roff, rend, pos, gate, hb, wg, wu, wd, cap, ts):
    e, r, tb = pos.shape
    d = hb.shape[1]
    nj = cap // ts
    grid_spec = pltpu.PrefetchScalarGridSpec(
        num_scalar_prefetch=4,
        grid=(e, nj),
        in_specs=[
            pl.BlockSpec((None, r, tb), lambda ei, j, *_: (ei, 0, 0)),
            pl.BlockSpec((None, r, tb), lambda ei, j, *_: (ei, 0, 0)),
            pl.BlockSpec(memory_space=pl.ANY),
            pl.BlockSpec((None, d, D_FF), lambda ei, j, *_: (ei, 0, 0)),
            pl.BlockSpec((None, d, D_FF), lambda ei, j, *_: (ei, 0, 0)),
            pl.BlockSpec((None, D_FF, d), lambda ei, j, *_: (ei, 0, 0)),
        ],
        out_specs=pl.BlockSpec((None, ts, d), lambda ei, j, *_: (ei, j, 0)),
        scratch_shapes=[
            pltpu.VMEM((GATHER_BUFS, TB, d), BF16),
            pltpu.SemaphoreType.DMA((GATHER_BUFS,)),
            pltpu.VMEM((ts, d), F32),
            pltpu.VMEM((ts, 1), F32),
        ],
    )
    return pl.pallas_call(
        functools.partial(_ffn_kernel, ts=ts, nj=nj),
        grid_spec=grid_spec,
        out_shape=jax.ShapeDtypeStruct((e, cap, d), BF16),
        compiler_params=_cparams(("arbitrary", "arbitrary"), VMEM_LIMIT),
        name="expert_ffn",
    )(lo, hi, roff.reshape(-1), rend.reshape(-1), pos, gate, hb, wg, wu, wd)


WIN_MAIN = 128
WIN_OVER = TB + BF16_SUBLANE - WIN_MAIN
MXU_DEPTH = 256


def _combine_kernel(start_ref, ostart_ref, ovf_ref, x_ref, post_ref, nw_ref, *rest, final):
    y_main = rest[:N_EXPERTS]
    y_over = rest[N_EXPERTS:2 * N_EXPERTS]
    o_ref = rest[2 * N_EXPERTS]
    r = pl.program_id(0)
    nr = pl.num_programs(0)
    post = post_ref[...]
    rels = [post[:, e:e + 1] - start_ref[e * nr + r] for e in range(N_EXPERTS)]
    per = MXU_DEPTH // WIN_MAIN
    lane = lax.broadcasted_iota(I32, (TB, MXU_DEPTH), 1)
    acc = x_ref[...]
    for e0 in range(0, N_EXPERTS, per):
        want = rels[e0 + per - 1] + (per - 1) * WIN_MAIN
        for k in range(per - 2, -1, -1):
            want = jnp.where(lane < (k + 1) * WIN_MAIN, rels[e0 + k] + k * WIN_MAIN, want)
        sel = jnp.where(want == lane, 1.0, 0.0).astype(BF16)
        rows = jnp.concatenate([y_main[e0 + k][...] for k in range(per)], axis=0)
        acc = acc + jnp.dot(sel, rows, preferred_element_type=F32)
    o_ref[...] = acc

    @pl.when(ovf_ref[r] != 0)
    def _():
        lane_o = lax.broadcasted_iota(I32, (TB, WIN_OVER), 1) + WIN_MAIN
        more = o_ref[...]
        for e in range(N_EXPERTS):
            sel = jnp.where(rels[e] == lane_o, 1.0, 0.0).astype(BF16)
            more = more + jnp.dot(sel, y_over[e][...], preferred_element_type=F32)
        o_ref[...] = more

    if final:
        o_ref[...] = _rms(o_ref[...], nw_ref[...], EPS)


def combine(x, post, start, ostart, ovf, y, norm_w, final):
    t, d = x.shape
    e, cap, _ = y.shape
    nr = t // TB

    def y_spec(ei, rows, which):
        return pl.BlockSpec((pl.Element(rows), pl.Element(d)),
                            lambda r, st, ost, ov: (pl.multiple_of(ei * cap + (st, ost)[which][ei * nr + r],
                                                                   BF16_SUBLANE), 0))

    grid_spec = pltpu.PrefetchScalarGridSpec(
        num_scalar_prefetch=3,
        grid=(nr,),
        in_specs=([pl.BlockSpec((TB, d), lambda r, st, ost, ov: (r, 0)),
                   pl.BlockSpec((TB, e), lambda r, st, ost, ov: (r, 0)),
                   pl.BlockSpec((1, d), lambda r, st, ost, ov: (0, 0))]
                  + [y_spec(ei, WIN_MAIN, 0) for ei in range(e)]
                  + [y_spec(ei, WIN_OVER, 1) for ei in range(e)]),
        out_specs=pl.BlockSpec((TB, d), lambda r, st, ost, ov: (r, 0)),
    )
    y2 = y.reshape(e * cap, d)
    return pl.pallas_call(
        functools.partial(_combine_kernel, final=final),
        grid_spec=grid_spec,
        out_shape=jax.ShapeDtypeStruct((t, d), F32),
        compiler_params=_cparams(("arbitrary",), VMEM_LIMIT),
        name="combine",
    )(start, ostart, ovf, x, post, norm_w.reshape(1, d), *([y2] * (2 * e)))


def _tiles(t, s):
    def fit(n, pref):
        while n % pref:
            pref //= 2
        return pref

    cap = max(1, CAPACITY_FACTOR * t // N_EXPERTS)
    return dict(
        tm_proj=fit(t, 1024), tn_proj=PROJ_COLS // 4,
        tq=fit(s, 1024), tk=fit(s, 512),
        tm_conv=fit(s, 512), tc_conv=1024,
        tm_tok=fit(s, 512),
        cap=cap, ts=fit(cap, 256),
    )


def _prep_layer(l, p):
    w_in = p['w_in'][l]
    o_q, o_k, o_v, o_z, o_xbc, o_dt = 1024, 2048, 3072, 5120, 8192, 8256
    w_main = jnp.concatenate([w_in[:, o_v:o_z], w_in[:, o_dt:], w_in[:, o_z:o_xbc], w_in[:, :o_k]],
                             axis=1).astype(BF16)
    w_vt = w_in[:, o_k:o_v].T.astype(BF16)
    w_dt = jnp.pad(w_in[:, o_xbc:o_dt], ((0, 0), (0, LANE - 2 * SSM_HEADS))).astype(BF16)
    dt_bias = p['dt_bias'][l].reshape(-1).astype(F32)
    a_neg = -jnp.exp(p['a_log'][l].astype(F32)).reshape(-1)
    par = jnp.zeros((8, LANE), F32).at[0, :2 * SSM_HEADS].set(dt_bias).at[1, :2 * SSM_HEADS].set(a_neg)
    part = jnp.zeros((LANE, 8), F32).at[:2 * SSM_HEADS, 0].set(dt_bias).at[:2 * SSM_HEADS, 1].set(a_neg)
    lam_init = 0.8 - 0.6 * math.exp(-0.3 * l)
    lam = (jnp.exp(jnp.sum(p['lambda_q1'][l].astype(F32) * p['lambda_k1'][l].astype(F32)))
           - jnp.exp(jnp.sum(p['lambda_q2'][l].astype(F32) * p['lambda_k2'][l].astype(F32))) + lam_init)
    slopes = jnp.exp2(-8.0 * (jnp.arange(ATTN_HEADS, dtype=F32) + 1.0) / ATTN_HEADS)
    tail = jnp.log(2.0 / (1.0 - jnp.exp(-slopes)))
    scal = jnp.concatenate([slopes, lam.reshape(1), jnp.full((1,), 1.0 - lam_init, F32), tail]).astype(F32)
    return dict(
        norm_mix=p['norm_mix_w'][l], w_main=w_main, w_vt=w_vt, w_dt=w_dt, w_dtt=w_dt.T, par=par, part=part,
        scal=scal,
        subln=p['attn_subln_w'][l].astype(F32),
        conv_w=p['conv_w'][l].astype(F32), conv_b=p['conv_b'][l].astype(F32),
        dskip_x=jnp.repeat(p['d_skip'][l].astype(F32), SSM_HEAD_DIM).reshape(1, D_INNER),
        ssm_nw=p['ssm_norm_w'][l].astype(F32).reshape(1, D_INNER),
        wa=p['w_attn_branch'][l].astype(BF16), ws=p['w_ssm_branch'][l].astype(BF16),
        wo=p['w_out'][l].astype(BF16),
        norm_cross=p['norm_cross_w'][l], norm_mem=p['norm_mem_w'][l],
        wxq=p['w_xq'][l].astype(BF16), wxkv=p['w_xkv'][l].astype(BF16), wxo=p['w_xo'][l].astype(BF16),
        norm_ffn=p['norm_ffn_w'][l], wrt=p['w_router'][l].astype(F32).T, br=p['b_router'][l].astype(F32),
        wg=p['w_gate'][l].astype(BF16), wu=p['w_up'][l].astype(BF16), wd=p['w_down'][l].astype(BF16),
    )


def _layer(x, mem, lw, b, s, norm_final_w, last):
    t = b * s
    tl = _tiles(t, s)
    proj = norm_matmul(x, lw['norm_mix'], lw['w_main'], BF16, tl['tm_proj'], tl['tn_proj'])
    vt, dt, dtt = proj_transposed(x, lw['norm_mix'], lw['w_vt'], lw['w_dt'], lw['w_dtt'], tl['tm_tok'])
    proj3 = proj.reshape(b, s, PROJ_COLS)
    dt3 = dt.reshape(b, s, LANE)
    attn = diff_attention(proj3, vt, lw['scal'], lw['subln'], tl['tq'], tl['tk'])
    xbc = conv_silu(proj3, lw['conv_w'], lw['conv_b'], tl['tm_conv'], tl['tc_conv'])
    hf, hb = ssd_states(xbc, dt3, lw['par'])
    ssm = ssd_output(xbc, proj3, dt3, dtt, lw['par'], lw['part'], lw['dskip_x'], lw['ssm_nw'], hf, hb)
    x = merge_out(x, attn.reshape(t, D_MODEL), ssm.reshape(t, D_INNER), proj, lw['wa'], lw['ws'], lw['wo'],
                  tl['tm_tok'])
    nm = mem.shape[0] * mem.shape[1]
    kv = norm_matmul(mem.reshape(nm, D_MODEL), lw['norm_mem'], lw['wxkv'], BF16, min(nm, 512), 2 * X_WIDTH)
    x = cross_attention(x, lw['norm_cross'], lw['wxq'], kv.reshape(b, MEM_TOKENS, 2 * X_WIDTH), lw['wxo'], s,
                        tl['tm_tok'])
    cap, ts = tl['cap'], tl['ts']
    nr = t // TB
    hbf, aff = router(x, lw['norm_ffn'], lw['wrt'], lw['br'], tl['tm_tok'])
    pos, gate, roff = topk_select(aff.reshape(N_EXPERTS, nr, TB), cap)
    roff = roff[:, :, 0]
    rend = jnp.concatenate([roff[:, 1:], jnp.full((N_EXPERTS, 1), cap, I32)], axis=1)
    edges = jnp.arange(cap // ts, dtype=I32) * ts
    lo = jnp.sum(rend[:, None, :] <= edges[None, :, None], axis=2).astype(I32).reshape(-1)
    hi = jnp.sum(roff[:, None, :] < (edges + ts)[None, :, None], axis=2).astype(I32).reshape(-1)
    y = expert_ffn(lo, hi, roff, rend, pos, gate, hbf, lw['wg'], lw['wu'], lw['wd'], cap, ts)
    start = jnp.minimum((roff // BF16_SUBLANE) * BF16_SUBLANE, cap - (WIN_MAIN + WIN_OVER)).astype(I32)
    ovf = jnp.any(rend - start > WIN_MAIN, axis=0).astype(I32)
    ostart = jnp.where(ovf[None, :] != 0, start + WIN_MAIN, 0).astype(I32)
    post = jnp.transpose(pos.reshape(N_EXPERTS, t))
    return combine(x, post, start.reshape(-1), ostart.reshape(-1), ovf, y, norm_final_w, last)


def _trunk(x, mem, layers, norm_final_w):
    b, s, d = x.shape
    xt = x.reshape(b * s, d)
    for l, lw in enumerate(layers):
        xt = _layer(xt, mem, lw, b, s, norm_final_w, l == len(layers) - 1)
    return xt.reshape(b, s, d)


def kernel(x_prompt, x_sample, mem_prompt, mem_sample, norm_mix_w, w_in, lambda_q1, lambda_k1, lambda_q2,
           lambda_k2, attn_subln_w, conv_w, conv_b, a_log, dt_bias, d_skip, ssm_norm_w, w_attn_branch,
           w_ssm_branch, w_out, norm_cross_w, norm_mem_w, w_xq, w_xkv, w_xo, norm_ffn_w, w_router, b_router,
           w_gate, w_up, w_down, norm_final_w):
    p = dict(norm_mix_w=norm_mix_w, w_in=w_in, lambda_q1=lambda_q1, lambda_k1=lambda_k1, lambda_q2=lambda_q2,
             lambda_k2=lambda_k2, attn_subln_w=attn_subln_w, conv_w=conv_w, conv_b=conv_b, a_log=a_log,
             dt_bias=dt_bias, d_skip=d_skip, ssm_norm_w=ssm_norm_w, w_attn_branch=w_attn_branch,
             w_ssm_branch=w_ssm_branch, w_out=w_out, norm_cross_w=norm_cross_w, norm_mem_w=norm_mem_w,
             w_xq=w_xq, w_xkv=w_xkv, w_xo=w_xo, norm_ffn_w=norm_ffn_w, w_router=w_router, b_router=b_router,
             w_gate=w_gate, w_up=w_up, w_down=w_down)
    layers = [_prep_layer(l, p) for l in range(w_in.shape[0])]
    y_prompt = _trunk(x_prompt, mem_prompt, layers, norm_final_w)
    y_sample = _trunk(x_sample, mem_sample, layers, norm_final_w)
    return (y_prompt, y_sample)
```
